```python
import math
import jax, jax.numpy as jnp
from jax import lax
import numpy as np

D_MODEL = 1024
BATCH = 2
SEQ = 16384
DEPTH = 2

GRID_W = 64
CTX_LEN = 256
HEAD_DIM = 64
BRANCH_WIDTH = D_MODEL // 2
N_BRANCH = 4
NA_HEADS = BRANCH_WIDTH // HEAD_DIM
NA_KR = 8
NA_KC = 16
SW_HEADS = BRANCH_WIDTH // HEAD_DIM
SW_KV_HEADS = 2
SW_WINDOW = 128
SW_BLOCK = 128
GLA_HEADS = 4
GLA_DK = BRANCH_WIDTH // 2
GLA_DV = BRANCH_WIDTH
GLA_RANK = 16
GLA_TAU = 16.0
GLA_CHUNK = 64
HY_WIDTH = BRANCH_WIDTH
HY_CONV = 3
HY_EMB = 33
HY_FFN = 64
HY_TARGET = 1e-2
HY_FAST_PCT = 0.3
HY_SLOW_PCT = 1.5
ROPE_BASE = 10000.0
EPS = 1e-6
NEG_INF = -1e30
F32 = jnp.float32

IN_WIDTHS = (
    BRANCH_WIDTH, BRANCH_WIDTH, BRANCH_WIDTH, BRANCH_WIDTH,
    BRANCH_WIDTH, SW_KV_HEADS * HEAD_DIM, SW_KV_HEADS * HEAD_DIM, BRANCH_WIDTH,
    GLA_DK, GLA_DK, GLA_DV, GLA_RANK, GLA_RANK, BRANCH_WIDTH,
    3 * HY_WIDTH, BRANCH_WIDTH,
    N_BRANCH * D_MODEL,
)
IN_TOTAL = sum(IN_WIDTHS)

kernel_name = "hybrid_dit_parallel_mixers"


def rmsnorm(x, w):
    xf = x.astype(F32)
    y = xf * lax.rsqrt(jnp.mean(xf * xf, axis=-1, keepdims=True) + EPS)
    return (y * w.astype(F32)).astype(x.dtype)


def split_cols(p):
    idx = [int(i) for i in np.cumsum(IN_WIDTHS)[:-1]]
    return jnp.split(p, idx, axis=-1)


def heads(x, n):
    b, t, _ = x.shape
    return x.reshape(b, t, n, -1).transpose(0, 2, 1, 3)


def merge_heads(x):
    b, h, t, d = x.shape
    return x.transpose(0, 2, 1, 3).reshape(b, t, h * d)


def softmax_f32(s):
    return jax.nn.softmax(s.astype(F32), axis=-1)


def axial_rope(length):
    t = jnp.arange(length, dtype=jnp.int32)
    row = (t // GRID_W).astype(F32)
    col = (t % GRID_W).astype(F32)
    n_freq = HEAD_DIM // 4
    inv = ROPE_BASE ** (-jnp.arange(n_freq, dtype=F32) / n_freq)
    ang = jnp.concatenate([row[:, None] * inv, col[:, None] * inv], axis=-1)
    return jnp.cos(ang), jnp.sin(ang)


def rope_2d(x, cos, sin):
    half = x.shape[-1] // 2
    x1, x2 = x[..., :half], x[..., half:]
    return jnp.concatenate([x1 * cos - x2 * sin, x1 * sin + x2 * cos], axis=-1).astype(x.dtype)


def ctx_attention(q, k, v, sink):
    b, hq, t, d = q.shape
    hkv = k.shape[1]
    g = hq // hkv
    qg = q.reshape(b, hkv, g, t, d)
    s = jnp.einsum('bkgqd,bkcd->bkgqc', qg, k).astype(F32) * (d ** -0.5)
    if sink is not None:
        s_sink = jnp.broadcast_to(sink.astype(F32).reshape(1, hkv, g, 1, 1), (b, hkv, g, t, 1))
        s = jnp.concatenate([s, s_sink], axis=-1)
    p = softmax_f32(s)[..., :k.shape[2]].astype(v.dtype)
    return jnp.einsum('bkgqc,bkcd->bkgqd', p, v).reshape(b, hq, t, d)


def natten_mixer(q_l, k_l, v_l, q_c, k_c, v_c, rpb, with_ctx_out):
    b, h, length, d = q_l.shape
    rows = length // GRID_W
    kr = min(NA_KR, rows)
    ncb = GRID_W // NA_KC
    kw = 2 * NA_KC
    scale = d ** -0.5
    qcol = np.arange(GRID_W).reshape(ncb, NA_KC)
    kc0 = np.clip(np.arange(ncb) * NA_KC - NA_KC // 2, 0, GRID_W - kw)
    kcol = kc0[:, None] + np.arange(kw)[None, :]
    cs = np.clip(qcol - NA_KC // 2, 0, GRID_W - NA_KC)
    colmask = (kcol[:, None, :] >= cs[..., None]) & (kcol[:, None, :] < cs[..., None] + NA_KC)
    dc_idx = np.clip(kcol[:, None, :] - qcol[..., None] + (NA_KC - 1), 0, 2 * NA_KC - 2)
    mask = jnp.asarray(np.broadcast_to(colmask[:, :, None, :], (ncb, NA_KC, kr, kw)).reshape(ncb, NA_KC, kr * kw))
    qg = q_l.reshape(b, h, rows, GRID_W, d)
    kg = k_l.reshape(b, h, rows, GRID_W, d)
    vg = v_l.reshape(b, h, rows, GRID_W, d)
    n_nb = kr * kw

    def row_block(r):
        rs = jnp.clip(r - kr // 2, 0, rows - kr)
        q_r = lax.dynamic_index_in_dim(qg, r, axis=2, keepdims=False).reshape(b, h, ncb, NA_KC, d)

        def gather(t):
            t_r = lax.dynamic_slice_in_dim(t, rs, kr, axis=2)
            t_b = t_r[:, :, :, kcol]
            return t_b.transpose(0, 1, 3, 2, 4, 5).reshape(b, h, ncb, n_nb, d)

        k_b, v_b = gather(kg), gather(vg)
        dr_idx = rs + jnp.arange(kr) - r + (NA_KR - 1)
        bias = rpb[:, dr_idx[None, None, :, None], dc_idx[:, :, None, :]].reshape(h, ncb, NA_KC, n_nb)
        s_nb = jnp.einsum('bhjqd,bhjkd->bhjqk', q_r, k_b).astype(F32) * scale + bias.astype(F32)
        s_nb = jnp.where(mask, s_nb, NEG_INF)
        s_cx = jnp.einsum('bhjqd,bhkd->bhjqk', q_r, k_c).astype(F32) * scale
        p = softmax_f32(jnp.concatenate([s_nb, s_cx], axis=-1)).astype(v_l.dtype)
        o = (jnp.einsum('bhjqk,bhjkd->bhjqd', p[..., :n_nb], v_b)
             + jnp.einsum('bhjqk,bhkd->bhjqd', p[..., n_nb:], v_c))
        return o.reshape(b, h, GRID_W, d)

    o = lax.map(row_block, jnp.arange(rows))
    o_l = o.transpose(1, 2, 0, 3, 4).reshape(b, h, length, d)
    o_c = ctx_attention(q_c, k_c, v_c, None) if with_ctx_out else None
    return o_c, o_l


def swa_mixer(q_l, k_l, v_l, q_c, k_c, v_c, sink, with_ctx_out):
    b, hq, length, d = q_l.shape
    hkv = k_l.shape[1]
    g = hq // hkv
    nb = length // SW_BLOCK
    kw = 3 * SW_BLOCK
    lc = k_c.shape[2]
    scale = d ** -0.5
    qb = q_l.reshape(b, hkv, g, nb, SW_BLOCK, d)

    def band(t):
        tp = jnp.pad(t, ((0, 0), (0, 0), (SW_BLOCK, SW_BLOCK), (0, 0))).reshape(b, hkv, nb + 2, SW_BLOCK, d)
        return jnp.concatenate([tp[:, :, :-2], tp[:, :, 1:-1], tp[:, :, 2:]], axis=3)

    k_w, v_w = band(k_l), band(v_l)
    qpos = jnp.arange(length).reshape(nb, SW_BLOCK)
    kpos = (jnp.arange(nb)[:, None] - 1) * SW_BLOCK + jnp.arange(kw)[None, :]
    mask = ((jnp.abs(qpos[:, :, None] - kpos[:, None, :]) <= SW_WINDOW)
            & (kpos[:, None, :] >= 0) & (kpos[:, None, :] < length))
    s_w = jnp.einsum('bkgnqd,bknwd->bkgnqw', qb, k_w).astype(F32) * scale
    s_w = jnp.where(mask, s_w, NEG_INF)
    s_c = jnp.einsum('bkgnqd,bkcd->bkgnqc', qb, k_c).astype(F32) * scale
    s_sink = jnp.broadcast_to(sink.astype(F32).reshape(1, hkv, g, 1, 1, 1), (b, hkv, g, nb, SW_BLOCK, 1))
    p = softmax_f32(jnp.concatenate([s_w, s_c, s_sink], axis=-1)).astype(v_l.dtype)
    o = (jnp.einsum('bkgnqw,bknwd->bkgnqd', p[..., :kw], v_w)
         + jnp.einsum('bkgnqc,bkcd->bkgnqd', p[..., kw:kw + lc], v_c))
    o_l = o.reshape(b, hq, length, d)
    o_c = ctx_attention(q_c, k_c, v_c, sink) if with_ctx_out else None
    return o_c, o_l


def gla_direction(q, k, v, log_a, s0, with_out):
    b, h, t, dk = k.shape
    dv = v.shape[-1]
    n = t // GLA_CHUNK
    kc = k.reshape(b, h, n, GLA_CHUNK, dk)
    vc = v.reshape(b, h, n, GLA_CHUNK, dv)
    cum = jnp.cumsum(log_a.reshape(b, h, n, GLA_CHUNK, dk), axis=3)
    cum_end = cum[:, :, :, -1:]
    kv = jnp.einsum('bhncd,bhnce->bhnde', kc * jnp.exp(cum_end - cum), vc)
    g = jnp.exp(cum_end[:, :, :, 0])

    def step(s, inp):
        g_n, kv_n = inp
        return g_n[..., None] * s + kv_n, s

    s_fin, s_prev = lax.scan(step, s0, (jnp.moveaxis(g, 2, 0), jnp.moveaxis(kv, 2, 0)))
    if not with_out:
        return None, s_fin
    qc = q.reshape(b, h, n, GLA_CHUNK, dk)
    qd = qc * jnp.exp(cum)
    kd = kc * jnp.exp(-cum)
    tril = jnp.tril(jnp.ones((GLA_CHUNK, GLA_CHUNK), dtype=bool))
    a = jnp.where(tril, jnp.einsum('bhnid,bhnjd->bhnij', qd, kd), 0.0)
    o = jnp.einsum('bhnij,bhnje->bhnie', a, vc) + jnp.einsum('bhnid,nbhde->bhnie', qd, s_prev)
    return o.reshape(b, h, t, dv), s_fin


def gla_mixer(q_l, k_l, v_l, laf_l, lab_l, q_c, k_c, v_c, laf_c, lab_c, norm_w, with_ctx_out):
    b, h, _, dk = k_l.shape
    dv = v_l.shape[-1]
    s0 = jnp.zeros((b, h, dk, dv), F32)
    flip = lambda t: t[:, :, ::-1]
    o_cf, s_cf = gla_direction(q_c, k_c, v_c, laf_c, s0, with_ctx_out)
    o_cb, s_cb = gla_direction(flip(q_c), flip(k_c), flip(v_c), flip(lab_c), s0, with_ctx_out)
    o_lf, _ = gla_direction(q_l, k_l, v_l, laf_l, s_cf, True)
    o_lb, _ = gla_direction(flip(q_l), flip(k_l), flip(v_l), flip(lab_l), s_cb, True)
    o_l = rmsnorm(o_lf + flip(o_lb), norm_w)
    o_c = rmsnorm(o_cf + flip(o_cb), norm_w) if with_ctx_out else None
    return o_c, o_l


def short_conv(u, w, bias):
    t = u.shape[1]
    pad = HY_CONV // 2
    up = jnp.pad(u, ((0, 0), (pad, HY_CONV - 1 - pad), (0, 0)))
    out = up[:, 0:t] * w[0]
    for i in range(1, HY_CONV):
        out = out + up[:, i:i + t] * w[i]
    return out + bias


def hyena_filter(t_len, w1, b1, w2, b2, w3, b3, freq, wout):
    t = jnp.linspace(0.0, 1.0, t_len, dtype=F32)[:, None]
    bands = (HY_EMB - 1) // 2
    w_ang = 2.0 * math.pi * jnp.arange(t_len, dtype=F32)[:, None] / t_len
    f = jnp.linspace(1e-4, bands - 1, bands, dtype=F32)[None, :]
    z = jnp.concatenate([t, jnp.cos(f * w_ang), -jnp.sin(f * w_ang)], axis=-1)
    fr = freq.astype(F32)
    hh = jnp.sin(fr * (z @ w1.astype(F32) + b1.astype(F32)))
    hh = jnp.sin(fr * (hh @ w2.astype(F32) + b2.astype(F32)))
    hh = jnp.sin(fr * (hh @ w3.astype(F32) + b3.astype(F32)))
    hh = hh @ wout.astype(F32)
    deltas = jnp.abs(jnp.linspace(math.log(HY_TARGET) / HY_SLOW_PCT, math.log(HY_TARGET) / HY_FAST_PCT,
                                  HY_WIDTH, dtype=F32))
    decay = jnp.exp(-t * deltas)
    h_f = hh[:, :HY_WIDTH] * decay
    h_b = hh[:, HY_WIDTH:] * decay
    l1 = jnp.sum(jnp.abs(h_f), axis=0) + jnp.sum(jnp.abs(h_b[1:]), axis=0)
    k2 = jnp.concatenate([h_f, jnp.zeros((1, HY_WIDTH), F32), h_b[:0:-1]], axis=0)
    return k2 / l1


def hyena_mixer(u_proj, conv_w, conv_b, w1, b1, w2, b2, w3, b3, freq, wout, filt_bias):
    t_len = u_proj.shape[1]
    u = short_conv(u_proj, conv_w, conv_b)
    x0, x1, v = jnp.split(u, 3, axis=-1)
    z = x1 * v
    k2 = hyena_filter(t_len, w1, b1, w2, b2, w3, b3, freq, wout)
    zf = jnp.fft.rfft(z.astype(F32), n=2 * t_len, axis=1)
    kf = jnp.fft.rfft(k2, axis=0)
    y = jnp.fft.irfft(zf * kf[None], n=2 * t_len, axis=1)[:, :t_len].astype(z.dtype)
    return x0 * (y + z * filt_bias)


def layer(xc, xl, c, c_ctx, cos, sin, norm_w, w_mod, b_mod, w_in, rpb, sink, w_alpha_up, b_alpha,
          gla_norm_w, conv_w, conv_b, filt_w1, filt_b1, filt_w2, filt_b2, filt_w3, filt_b3, filt_freq,
          filt_wout, filt_bias, w_branch, w_out, with_ctx_out):
    mod_l = jax.nn.silu(c) @ w_mod + b_mod
    mod_c = jax.nn.silu(c_ctx) @ w_mod + b_mod
    sh_l, sc_l, g_l = jnp.split(mod_l[:, None, :], 3, axis=-1)
    sh_c, sc_c, g_c = jnp.split(mod_c, 3, axis=-1)
    hl = rmsnorm(xl, norm_w) * (1.0 + sc_l) + sh_l
    hc = rmsnorm(xc, norm_w) * (1.0 + sc_c) + sh_c
    (qa_l, ka_l, va_l, ga_l, qb_l, kb_l, vb_l, gb_l, qg_l, kg_l, vg_l, lrf_l, lrb_l, gg_l,
     uh_l, gh_l, gm_l) = split_cols(hl @ w_in)
    (qa_c, ka_c, va_c, ga_c, qb_c, kb_c, vb_c, gb_c, qg_c, kg_c, vg_c, lrf_c, lrb_c, gg_c,
     uh_c, gh_c, gm_c) = split_cols(hc @ w_in)

    a_c, a_l = natten_mixer(heads(qa_l, NA_HEADS), heads(ka_l, NA_HEADS), heads(va_l, NA_HEADS),
                            heads(qa_c, NA_HEADS), heads(ka_c, NA_HEADS), heads(va_c, NA_HEADS),
                            rpb, with_ctx_out)
    b_c, b_l = swa_mixer(rope_2d(heads(qb_l, SW_HEADS), cos, sin), rope_2d(heads(kb_l, SW_KV_HEADS), cos, sin),
                         heads(vb_l, SW_KV_HEADS),
                         heads(qb_c, SW_HEADS), heads(kb_c, SW_KV_HEADS), heads(vb_c, SW_KV_HEADS),
                         sink, with_ctx_out)
    dk_h = GLA_DK // GLA_HEADS

    def log_decay(lr, direction):
        return heads(jax.nn.log_sigmoid((lr @ w_alpha_up[direction] + b_alpha[direction]).astype(F32)) / GLA_TAU,
                     GLA_HEADS)

    c_c, c_l = gla_mixer(heads(qg_l, GLA_HEADS) * dk_h ** -0.5, heads(kg_l, GLA_HEADS), heads(vg_l, GLA_HEADS),
                         log_decay(lrf_l, 0), log_decay(lrb_l, 1),
                         heads(qg_c, GLA_HEADS) * dk_h ** -0.5, heads(kg_c, GLA_HEADS), heads(vg_c, GLA_HEADS),
                         log_decay(lrf_c, 0), log_decay(lrb_c, 1), gla_norm_w, with_ctx_out)
    d_l = hyena_mixer(uh_l, conv_w, conv_b, filt_w1, filt_b1, filt_w2, filt_b2, filt_w3, filt_b3,
                      filt_freq, filt_wout, filt_bias)

    def merge(ys, gm):
        gates = jax.nn.sigmoid(gm.reshape(gm.shape[:-1] + (N_BRANCH, D_MODEL)))
        acc = gates[..., 0, :] * (ys[0] @ w_branch[0])
        for i in range(1, N_BRANCH):
            acc = acc + gates[..., i, :] * (ys[i] @ w_branch[i])
        return acc @ w_out

    ys_l = [merge_heads(a_l) * jax.nn.silu(ga_l), merge_heads(b_l) * jax.nn.silu(gb_l),
            merge_heads(c_l) * jax.nn.silu(gg_l), d_l * jax.nn.silu(gh_l)]
    xl_new = xl + g_l * merge(ys_l, gm_l)
    if with_ctx_out:
        d_c = hyena_mixer(uh_c, conv_w, conv_b, filt_w1, filt_b1, filt_w2, filt_b2, filt_w3, filt_b3,
                          filt_freq, filt_wout, filt_bias)
        ys_c = [merge_heads(a_c) * jax.nn.silu(ga_c), merge_heads(b_c) * jax.nn.silu(gb_c),
                merge_heads(c_c) * jax.nn.silu(gg_c), d_c * jax.nn.silu(gh_c)]
        xc = xc + g_c * merge(ys_c, gm_c)
    return xc, xl_new


def setup_inputs(seed: int = 0) -> dict:
    key = jax.random.key(seed)
    ks = jax.random.split(key, 28)
    nrm = lambda k, shape, scale: jax.random.normal(k, shape, jnp.float32) * scale
    d = D_MODEL
    return {
        "x": nrm(ks[0], (BATCH, SEQ, d), 1.0),
        "c": nrm(ks[1], (BATCH, d), 1.0),
        "ctx": nrm(ks[2], (BATCH, CTX_LEN, d), 1.0),
        "c_ctx": nrm(ks[3], (d,), 1.0),
        "norm_w": 1.0 + nrm(ks[4], (DEPTH, d), 0.02),
        "w_mod": nrm(ks[5], (DEPTH, d, 3 * d), d ** -0.5),
        "b_mod": nrm(ks[6], (DEPTH, 3 * d), 0.01),
        "w_in": nrm(ks[7], (DEPTH, d, IN_TOTAL), d ** -0.5),
        "rpb": nrm(ks[8], (DEPTH, NA_HEADS, 2 * NA_KR - 1, 2 * NA_KC - 1), 0.5),
        "sink": nrm(ks[9], (DEPTH, SW_HEADS), 0.5),
        "w_alpha_up": nrm(ks[10], (DEPTH, 2, GLA_RANK, GLA_DK), GLA_RANK ** -0.5),
        "b_alpha": nrm(ks[11], (DEPTH, 2, GLA_DK), 0.1),
        "gla_norm_w": 1.0 + nrm(ks[12], (DEPTH, GLA_DV // GLA_HEADS), 0.02),
        "conv_w": nrm(ks[13], (DEPTH, HY_CONV, 3 * HY_WIDTH), HY_CONV ** -0.5),
        "conv_b": nrm(ks[14], (DEPTH, 3 * HY_WIDTH), 0.01),
        "filt_w1": nrm(ks[15], (DEPTH, HY_EMB, HY_FFN), HY_EMB ** -0.5),
        "filt_b1": nrm(ks[16], (DEPTH, HY_FFN), 0.1),
        "filt_w2": nrm(ks[17], (DEPTH, HY_FFN, HY_FFN), HY_FFN ** -0.5),
        "filt_b2": nrm(ks[18], (DEPTH, HY_FFN), 0.1),
        "filt_w3": nrm(ks[19], (DEPTH, HY_FFN, HY_FFN), HY_FFN ** -0.5),
        "filt_b3": nrm(ks[20], (DEPTH, HY_FFN), 0.1),
        "filt_freq": 1.0 + nrm(ks[21], (DEPTH, HY_FFN), 0.02),
        "filt_wout": nrm(ks[22], (DEPTH, HY_FFN, 2 * HY_WIDTH), HY_FFN ** -0.5),
        "filt_bias": nrm(ks[23], (DEPTH, HY_WIDTH), 0.5),
        "w_branch": nrm(ks[24], (DEPTH, N_BRANCH, BRANCH_WIDTH, d), BRANCH_WIDTH ** -0.5),
        "w_out": nrm(ks[25], (DEPTH, d, d), d ** -0.5),
        "final_norm_w": 1.0 + nrm(ks[26], (d,), 0.02),
    }


def reference(x, c, ctx, c_ctx, norm_w, w_mod, b_mod, w_in, rpb, sink, w_alpha_up, b_alpha, gla_norm_w,
              conv_w, conv_b, filt_w1, filt_b1, filt_w2, filt_b2, filt_w3, filt_b3, filt_freq, filt_wout,
              filt_bias, w_branch, w_out, final_norm_w):
    cos, sin = axial_rope(x.shape[1])
    xc, xl = ctx, x
    for i in range(DEPTH):
        xc, xl = layer(xc, xl, c, c_ctx, cos, sin, norm_w[i], w_mod[i], b_mod[i], w_in[i], rpb[i], sink[i],
                       w_alpha_up[i], b_alpha[i], gla_norm_w[i], conv_w[i], conv_b[i], filt_w1[i], filt_b1[i],
                       filt_w2[i], filt_b2[i], filt_w3[i], filt_b3[i], filt_freq[i], filt_wout[i], filt_bias[i],
                       w_branch[i], w_out[i], with_ctx_out=(i < DEPTH - 1))
    return rmsnorm(xl, final_norm_w)
```

```python
import functools
import math

import numpy as np
import jax
import jax.numpy as jnp
from jax import lax
from jax.experimental import pallas as pl
from jax.experimental.pallas import tpu as pltpu

F32 = jnp.float32
BF16 = jnp.bfloat16

D_MODEL = 1024
GRID_W = 64
HEAD_DIM = 64
BRANCH_WIDTH = D_MODEL // 2
N_BRANCH = 4
NA_HEADS = 8
NA_KR = 8
NA_KC = 16
SW_HEADS = 8
SW_KV_HEADS = 2
SW_WINDOW = 128
SW_BLOCK = 128
GLA_HEADS = 4
GLA_DK = BRANCH_WIDTH // 2
GLA_DV = BRANCH_WIDTH
GLA_RANK = 16
GLA_TAU = 16.0
GLA_CHUNK = 64
HY_WIDTH = BRANCH_WIDTH
HY_EMB = 33
HY_FFN = 64
HY_TARGET = 1e-2
HY_FAST_PCT = 0.3
HY_SLOW_PCT = 1.5
ROPE_BASE = 10000.0
EPS = 1e-6
NEG_INF = -1e30
LANES = 128
FFT_N2 = 256
VMEM_LIMIT = 56 * 1024 * 1024

_IN_WIDTHS = (512, 512, 512, 512, 512, 128, 128, 512, 256, 256, 512, 16, 16, 512, 1536, 512, 4096)
_IN_OFF = np.concatenate([[0], np.cumsum(_IN_WIDTHS)])
(_O_AQ, _O_AK, _O_AV, _O_AG, _O_BQ, _O_BK, _O_BV, _O_BG, _O_CQ, _O_CK, _O_CV, _O_LRF, _O_LRB, _O_CG,
 _O_DU, _O_DG, _O_GM) = [int(v) for v in _IN_OFF[:-1]]
IN_TOTAL = int(_IN_OFF[-1])

P_AQ, P_AK, P_AV, P_AG = 0, 512, 1024, 1536
P_BQ, P_BG = 2048, 2560
P_CV, P_CG = 3072, 3584
P_DU, P_DG = 4096, 5632
P_GM = 6144
P_CQ, P_CK = 10240, 10496
P_BK, P_BV = 10752, 10880
P_LR = 11008
NP_COLS = 11264


def _swa_q_perm():
    idx = np.zeros(512, np.int64)
    for t in range(4):
        a, b = t, t + 4
        base = 128 * t
        idx[base + 0:base + 32] = 64 * a + np.arange(32)
        idx[base + 32:base + 64] = 64 * b + np.arange(32)
        idx[base + 64:base + 96] = 64 * a + 32 + np.arange(32)
        idx[base + 96:base + 128] = 64 * b + 32 + np.arange(32)
    return idx


def _swa_k_perm():
    idx = np.zeros(128, np.int64)
    idx[0:32] = np.arange(32)
    idx[32:64] = 64 + np.arange(32)
    idx[64:96] = 32 + np.arange(32)
    idx[96:128] = 96 + np.arange(32)
    return idx


def _swa_out_perm():
    idx = np.zeros(512, np.int64)
    for t in range(4):
        idx[128 * t:128 * t + 64] = 64 * t + np.arange(64)
        idx[128 * t + 64:128 * t + 128] = 64 * (t + 4) + np.arange(64)
    return idx


def _build_in_perm():
    perm = np.full(NP_COLS, -1, np.int64)

    def put(p, o, w):
        perm[p:p + w] = o + np.arange(w)

    put(P_AQ, _O_AQ, 512); put(P_AK, _O_AK, 512); put(P_AV, _O_AV, 512); put(P_AG, _O_AG, 512)
    perm[P_BQ:P_BQ + 512] = _O_BQ + _swa_q_perm()
    perm[P_BG:P_BG + 512] = _O_BG + _swa_out_perm()
    put(P_CV, _O_CV, 512); put(P_CG, _O_CG, 512)
    put(P_DU, _O_DU, 1536); put(P_DG, _O_DG, 512)
    put(P_GM, _O_GM, 4096)
    put(P_CQ, _O_CQ, 256); put(P_CK, _O_CK, 256)
    perm[P_BK:P_BK + 128] = _O_BK + _swa_k_perm()
    put(P_BV, _O_BV, 128)
    put(P_LR, _O_LRF, 16); put(P_LR + 16, _O_LRB, 16)
    return perm


_IN_PERM = _build_in_perm()
_SWA_OUT_PERM = _swa_out_perm()


def _cparams(sem):
    return pltpu.CompilerParams(dimension_semantics=sem, vmem_limit_bytes=VMEM_LIMIT)


def _sigmoid(x):
    return 1.0 / (1.0 + jnp.exp(-x))


def _split3(a):
    hi = a.astype(BF16)
    r1 = a - hi.astype(F32)
    mid = r1.astype(BF16)
    lo = (r1 - mid.astype(F32)).astype(BF16)
    return hi, mid, lo


def _dot(a, b):
    return jnp.dot(a, b, preferred_element_type=F32)


def _dot_nt(a, b):
    return lax.dot_general(a, b, (((1,), (1,)), ((), ())), preferred_element_type=F32)


def _dot_tn(a, b):
    return lax.dot_general(a, b, (((0,), (0,)), ((), ())), preferred_element_type=F32)


def _dot_x3(a, b):
    ah, am, _ = _split3(a)
    bh, bm, _ = _split3(b)
    return _dot(ah, bh) + (_dot(ah, bm) + _dot(am, bh))


def _mod_kernel(c_ref, w_ref, b_ref, o_ref):
    c = c_ref[...]
    s = c * _sigmoid(c)
    o_ref[0] = _dot_x3(s, w_ref[0]) + b_ref[0]


def _modulation(cvec, w_mod, b_mod):
    depth, d, n = w_mod.shape
    tn = 512
    return pl.pallas_call(
        _mod_kernel,
        grid=(depth, n // tn),
        in_specs=[pl.BlockSpec((8, d), lambda l, j: (0, 0)),
                  pl.BlockSpec((1, d, tn), lambda l, j: (l, 0, j)),
                  pl.BlockSpec((1, 1, tn), lambda l, j: (l, 0, j))],
        out_specs=pl.BlockSpec((1, 8, tn), lambda l, j: (l, 0, j)),
        out_shape=jax.ShapeDtypeStruct((depth, 8, n), F32),
        compiler_params=_cparams(("parallel", "parallel")),
        name="modulation",
    )(cvec, w_mod, b_mod.reshape(depth, 1, n))


def _inproj_kernel(x_ref, nw_ref, sc_ref, sh_ref, w_ref, o_ref, h_ref):
    @pl.when(pl.program_id(2) == 0)
    def _():
        x = x_ref[0]
        ms = jnp.mean(x * x, axis=-1, keepdims=True)
        y = x * lax.rsqrt(ms + EPS) * nw_ref[...]
        h_ref[...] = (y * (1.0 + sc_ref[0]) + sh_ref[0]).astype(BF16)

    o_ref[0] = _dot(h_ref[...], w_ref[...]).astype(BF16)


def _inproj(x, norm_w, scale, shift, w_perm):
    b, t, d = x.shape
    n = w_perm.shape[1]
    tm = min(t, 1024)
    tn = 1024
    return pl.pallas_call(
        _inproj_kernel,
        grid=(b, t // tm, n // tn),
        in_specs=[pl.BlockSpec((1, tm, d), lambda bi, i, j: (bi, i, 0)),
                  pl.BlockSpec((1, d), lambda bi, i, j: (0, 0)),
                  pl.BlockSpec((1, 1, d), lambda bi, i, j: (bi, 0, 0)),
                  pl.BlockSpec((1, 1, d), lambda bi, i, j: (bi, 0, 0)),
                  pl.BlockSpec((d, tn), lambda bi, i, j: (0, j))],
        out_specs=pl.BlockSpec((1, tm, tn), lambda bi, i, j: (bi, i, j)),
        out_shape=jax.ShapeDtypeStruct((b, t, n), BF16),
        scratch_shapes=[pltpu.VMEM((tm, d), BF16)],
        compiler_params=_cparams(("parallel", "parallel", "arbitrary")),
        name="inproj",
    )(x, norm_w.reshape(1, d), scale, shift, w_perm)


def _merge_kernel(ya, yb, yc, yd, ga, gb, gc, gd, m0, m1, m2, m3, wb_ref, wo_ref, g_ref, x_ref, fw_ref,
                  o_ref, *, final):
    acc = None
    for i, (y, g, gm) in enumerate(((ya, ga, m0), (yb, gb, m1), (yc, gc, m2), (yd, gd, m3))):
        gv = g[0].astype(F32)
        ys = (y[0].astype(F32) * (gv * _sigmoid(gv))).astype(BF16)
        t = _sigmoid(gm[0].astype(F32)) * _dot(ys, wb_ref[i])
        acc = t if acc is None else acc + t
    out = x_ref[0] + g_ref[0] * _dot(acc.astype(BF16), wo_ref[...])
    if final:
        ms = jnp.mean(out * out, axis=-1, keepdims=True)
        out = out * lax.rsqrt(ms + EPS) * fw_ref[...]
    o_ref[0] = out


def _merge(ys, proj, w_branch, w_out, gate, x, final_w, final):
    b, t, d = x.shape
    bw = BRANCH_WIDTH
    tm = min(t, 512)
    yspec = pl.BlockSpec((1, tm, bw), lambda bi, i: (bi, i, 0))

    def pspec(col, width):
        blk = col // width
        return pl.BlockSpec((1, tm, width), lambda bi, i: (bi, i, blk))

    in_specs = ([yspec] * 4
                + [pspec(P_AG, bw), pspec(P_BG, bw), pspec(P_CG, bw), pspec(P_DG, bw)]
                + [pspec(P_GM + k * d, d) for k in range(N_BRANCH)]
                + [pl.BlockSpec((N_BRANCH, bw, d), lambda bi, i: (0, 0, 0)),
                   pl.BlockSpec((d, d), lambda bi, i: (0, 0)),
                   pl.BlockSpec((1, 1, d), lambda bi, i: (bi, 0, 0)),
                   pl.BlockSpec((1, tm, d), lambda bi, i: (bi, i, 0)),
                   pl.BlockSpec((1, d), lambda bi, i: (0, 0))])
    return pl.pallas_call(
        functools.partial(_merge_kernel, final=final),
        grid=(b, t // tm),
        in_specs=in_specs,
        out_specs=pl.BlockSpec((1, tm, d), lambda bi, i: (bi, i, 0)),
        out_shape=jax.ShapeDtypeStruct((b, t, d), F32),
        compiler_params=_cparams(("parallel", "parallel")),
        name="merge",
    )(*ys, *([proj] * 8), w_branch, w_out, gate, x, final_w.reshape(1, d))


def _rope_kernel(q_ref, k_ref, cos_ref, sin_ref, qo_ref, ko_ref):
    cos = cos_ref[...]
    sin = sin_ref[...]

    def rot(x):
        return x * cos + pltpu.roll(x, 64, 1) * sin

    for t in range(4):
        q = q_ref[0, :, 128 * t:128 * (t + 1)].astype(F32)
        qo_ref[0, :, 128 * t:128 * (t + 1)] = (rot(q) * HEAD_DIM ** -0.5).astype(BF16)
    ko_ref[0] = rot(k_ref[0].astype(F32)).astype(BF16)


def _rope(proj, cos128, sin128):
    b, t, _ = proj.shape
    tm = min(t, 1024)
    return pl.pallas_call(
        _rope_kernel,
        grid=(b, t // tm),
        in_specs=[pl.BlockSpec((1, tm, 512), lambda bi, i: (bi, i, P_BQ // 512)),
                  pl.BlockSpec((1, tm, 128), lambda bi, i: (bi, i, P_BK // 128)),
                  pl.BlockSpec((tm, 128), lambda bi, i: (i, 0)),
                  pl.BlockSpec((tm, 128), lambda bi, i: (i, 0))],
        out_specs=[pl.BlockSpec((1, tm, 512), lambda bi, i: (bi, i, 0)),
                   pl.BlockSpec((1, tm, 128), lambda bi, i: (bi, i, 0))],
        out_shape=[jax.ShapeDtypeStruct((b, t, 512), BF16), jax.ShapeDtypeStruct((b, t, 128), BF16)],
        compiler_params=_cparams(("parallel", "parallel")),
        name="rope",
    )(proj, proj, cos128, sin128)


def _rope_tables(length):
    t = jnp.arange(length, dtype=jnp.int32)
    row = (t // GRID_W).astype(F32)
    col = (t % GRID_W).astype(F32)
    n_freq = HEAD_DIM // 4
    inv = ROPE_BASE ** (-jnp.arange(n_freq, dtype=F32) / n_freq)
    ang = jnp.concatenate([row[:, None] * inv, col[:, None] * inv], axis=-1)
    cos, sin = jnp.cos(ang), jnp.sin(ang)
    return jnp.tile(cos, (1, 4)), jnp.concatenate([-sin, -sin, sin, sin], axis=-1)


def _swa_kernel(q_ref, k_ref, v_ref, kc_ref, vc_ref, sink_ref, o_ref, *, length):
    i = pl.program_id(1)
    blk = SW_BLOCK
    kw = 3 * blk
    start = pl.multiple_of(jnp.clip((i - 1) * blk, 0, length - kw), blk)
    kwin = k_ref[0, pl.ds(start, kw), :]
    vwin = v_ref[0, pl.ds(start, kw), :]
    kc = kc_ref[0]
    vc = vc_ref[0]
    qpos = i * blk + lax.broadcasted_iota(jnp.int32, (2 * blk, kw), 0) % blk
    kpos = start + lax.broadcasted_iota(jnp.int32, (2 * blk, kw), 1)
    wmask = jnp.abs(qpos - kpos) <= SW_WINDOW
    lane = lax.broadcasted_iota(jnp.int32, (blk, LANES), 1)
    a_lanes = (lane // 32) % 2 == 0
    lo_lanes = lane < 64
    top = lax.broadcasted_iota(jnp.int32, (2 * blk, 1), 0) < blk
    for t in range(4):
        qt = q_ref[0, :, 128 * t:128 * (t + 1)]
        zero = jnp.zeros_like(qt)
        qq = jnp.concatenate([jnp.where(a_lanes, qt, zero), jnp.where(a_lanes, zero, qt)], axis=0)
        sw = jnp.where(wmask, _dot_nt(qq, kwin), NEG_INF)
        sc = _dot_nt(qq, kc)
        sk = jnp.where(top, sink_ref[t:t + 1, 0:1], sink_ref[t + 4:t + 5, 0:1])
        m = jnp.maximum(jnp.maximum(jnp.max(sw, axis=-1, keepdims=True), jnp.max(sc, axis=-1, keepdims=True)), sk)
        pw = jnp.exp(sw - m)
        pc = jnp.exp(sc - m)
        den = jnp.sum(pw, axis=-1, keepdims=True) + jnp.sum(pc, axis=-1, keepdims=True) + jnp.exp(sk - m)
        o = (_dot(pw.astype(BF16), vwin) + _dot(pc.astype(BF16), vc)) / den
        o_ref[0, :, 128 * t:128 * (t + 1)] = jnp.where(lo_lanes, o[:blk], o[blk:]).astype(BF16)


def _swa(qr, kr, proj_l, kc_rot, proj_c, sink128):
    b, length, _ = qr.shape
    lc = proj_c.shape[1]
    return pl.pallas_call(
        functools.partial(_swa_kernel, length=length),
        grid=(b, length // SW_BLOCK),
        in_specs=[pl.BlockSpec((1, SW_BLOCK, 512), lambda bi, i: (bi, i, 0)),
                  pl.BlockSpec((1, length, 128), lambda bi, i: (bi, 0, 0)),
                  pl.BlockSpec((1, length, 128), lambda bi, i: (bi, 0, P_BV // 128)),
                  pl.BlockSpec((1, lc, 128), lambda bi, i: (bi, 0, P_BK // 128)),
                  pl.BlockSpec((1, lc, 128), lambda bi, i: (bi, 0, P_BV // 128)),
                  pl.BlockSpec((8, 128), lambda bi, i: (0, 0))],
        out_specs=pl.BlockSpec((1, SW_BLOCK, 512), lambda bi, i: (bi, i, 0)),
        out_shape=jax.ShapeDtypeStruct((b, length, 512), BF16),
        compiler_params=_cparams(("parallel", "arbitrary")),
        name="swa",
    )(qr, kr, proj_l, kc_rot, proj_c, sink128)


def _ctx_attn_kernel(q_ref, k_ref, v_ref, sink_ref, o_ref, *, swa):
    t_len = q_ref.shape[1]
    lane = lax.broadcasted_iota(jnp.int32, (t_len, LANES), 1)
    lo_lanes = lane < 64
    a_lanes = ((lane // 32) % 2 == 0) if swa else lo_lanes
    top = lax.broadcasted_iota(jnp.int32, (2 * t_len, 1), 0) < t_len
    for t in range(4):
        qt = (q_ref[0, :, 128 * t:128 * (t + 1)].astype(F32) * HEAD_DIM ** -0.5).astype(BF16)
        zero = jnp.zeros_like(qt)
        qq = jnp.concatenate([jnp.where(a_lanes, qt, zero), jnp.where(a_lanes, zero, qt)], axis=0)
        if swa:
            kt, vt = k_ref[0], v_ref[0]
        else:
            kt, vt = k_ref[0, :, 128 * t:128 * (t + 1)], v_ref[0, :, 128 * t:128 * (t + 1)]
        s = _dot_nt(qq, kt)
        m = jnp.max(s, axis=-1, keepdims=True)
        if swa:
            sk = jnp.where(top, sink_ref[t:t + 1, 0:1], sink_ref[t + 4:t + 5, 0:1])
            m = jnp.maximum(m, sk)
        p = jnp.exp(s - m)
        den = jnp.sum(p, axis=-1, keepdims=True)
        if swa:
            den = den + jnp.exp(sk - m)
        o = _dot(p.astype(BF16), vt) / den
        o_ref[0, :, 128 * t:128 * (t + 1)] = jnp.where(lo_lanes, o[:t_len], o[t_len:]).astype(BF16)


def _ctx_attn(proj_c, sink128, swa):
    b, lc, _ = proj_c.shape
    if swa:
        qs = pl.BlockSpec((1, lc, 512), lambda bi: (bi, 0, P_BQ // 512))
        ks = pl.BlockSpec((1, lc, 128), lambda bi: (bi, 0, P_BK // 128))
        vs = pl.BlockSpec((1, lc, 128), lambda bi: (bi, 0, P_BV // 128))
    else:
        qs = pl.BlockSpec((1, lc, 512), lambda bi: (bi, 0, P_AQ // 512))
        ks = pl.BlockSpec((1, lc, 512), lambda bi: (bi, 0, P_AK // 512))
        vs = pl.BlockSpec((1, lc, 512), lambda bi: (bi, 0, P_AV // 512))
    return pl.pallas_call(
        functools.partial(_ctx_attn_kernel, swa=swa),
        grid=(b,),
        in_specs=[qs, ks, vs, pl.BlockSpec((8, 128), lambda bi: (0, 0))],
        out_specs=pl.BlockSpec((1, lc, 512), lambda bi: (bi, 0, 0)),
        out_shape=jax.ShapeDtypeStruct((b, lc, 512), BF16),
        compiler_params=_cparams(("parallel",)),
        name="ctx_attn_swa" if swa else "ctx_attn_na",
    )(proj_c, proj_c, proj_c, sink128)


def _natten_kernel(q_ref, k_ref, v_ref, kc_ref, vc_ref, bias_ref, o_ref, *, rows, rows_per_step):
    blk = pl.program_id(2)
    kc = kc_ref[0]
    vc = vc_ref[0]
    lane = lax.broadcasted_iota(jnp.int32, (GRID_W, LANES), 1)
    lo_lanes = lane < 64
    nkeys = NA_KR * GRID_W

    def one_row(rr, carry):
        r = blk * rows_per_step + rr
        rs = jnp.clip(r - NA_KR // 2, 0, rows - NA_KR)
        off = rs - r + (NA_KR - 1)
        k0 = pl.multiple_of(rs * GRID_W, GRID_W)
        krows = k_ref[0, pl.ds(k0, nkeys), :]
        vrows = v_ref[0, pl.ds(k0, nkeys), :]
        q0 = pl.multiple_of(rr * GRID_W, GRID_W)
        q = (q_ref[0, pl.ds(q0, GRID_W), :].astype(F32) * HEAD_DIM ** -0.5).astype(BF16)
        zero = jnp.zeros_like(q)
        outs = []
        for h in range(2):
            qh = jnp.where(lo_lanes, q, zero) if h == 0 else jnp.where(lo_lanes, zero, q)
            s = _dot_nt(qh, krows) + bias_ref[off, h]
            sc = _dot_nt(qh, kc)
            m = jnp.maximum(jnp.max(s, axis=-1, keepdims=True), jnp.max(sc, axis=-1, keepdims=True))
            p = jnp.exp(s - m)
            pc = jnp.exp(sc - m)
            den = jnp.sum(p, axis=-1, keepdims=True) + jnp.sum(pc, axis=-1, keepdims=True)
            outs.append((_dot(p.astype(BF16), vrows) + _dot(pc.astype(BF16), vc)) / den)
        o_ref[0, pl.ds(q0, GRID_W), :] = jnp.where(lo_lanes, outs[0], outs[1]).astype(BF16)
        return carry

    lax.fori_loop(0, rows_per_step, one_row, 0)


def _natten(proj_l, proj_c, bias_tab):
    b, length, _ = proj_l.shape
    lc = proj_c.shape[1]
    rows = length // GRID_W
    assert rows >= NA_KR
    rps = 8
    tq = rps * GRID_W
    nk = NA_KR * GRID_W
    return pl.pallas_call(
        functools.partial(_natten_kernel, rows=rows, rows_per_step=rps),
        grid=(b, 4, rows // rps),
        in_specs=[pl.BlockSpec((1, tq, 128), lambda bi, hp, i: (bi, i, P_AQ // 128 + hp)),
                  pl.BlockSpec((1, length, 128), lambda bi, hp, i: (bi, 0, P_AK // 128 + hp)),
                  pl.BlockSpec((1, length, 128), lambda bi, hp, i: (bi, 0, P_AV // 128 + hp)),
                  pl.BlockSpec((1, lc, 128), lambda bi, hp, i: (bi, 0, P_AK // 128 + hp)),
                  pl.BlockSpec((1, lc, 128), lambda bi, hp, i: (bi, 0, P_AV // 128 + hp)),
                  pl.BlockSpec((NA_KR, 2, GRID_W, nk), lambda bi, hp, i: (0, hp, 0, 0))],
        out_specs=pl.BlockSpec((1, tq, 128), lambda bi, hp, i: (bi, i, hp)),
        out_shape=jax.ShapeDtypeStruct((b, length, 512), BF16),
        compiler_params=_cparams(("parallel", "parallel", "arbitrary")),
        name="natten",
    )(proj_l, proj_l, proj_l, proj_c, proj_c, bias_tab)


def _natten_bias_table(rpb):
    qc = np.arange(GRID_W)[:, None]
    kcol = np.arange(GRID_W)[None, :]
    dc = np.clip(kcol - qc + (NA_KC - 1), 0, 2 * NA_KC - 2)
    cs = np.clip(qc - NA_KC // 2, 0, GRID_W - NA_KC)
    cmask = (kcol >= cs) & (kcol < cs + NA_KC)
    dr = np.arange(NA_KR)[:, None] + np.arange(NA_KR)[None, :]
    g = rpb[:, dr[:, :, None, None], dc[None, None, :, :]]
    g = jnp.where(cmask[None, None, None], g.astype(F32), NEG_INF)
    return g.transpose(1, 0, 3, 2, 4).reshape(NA_KR, NA_HEADS, GRID_W, NA_KR * GRID_W)


def _gla_kernel(*refs, rev, ncb, final):
    if final:
        (q_ref, k_ref, v_ref, lr_ref, whi_ref, wlo_ref, b_ref, s0_ref, of_ref, gnw_ref,
         o_ref, sfin_ref, st_ref) = refs
    else:
        (q_ref, k_ref, v_ref, lr_ref, whi_ref, wlo_ref, b_ref, s0_ref,
         o_ref, sfin_ref, st_ref) = refs
    ch = GLA_CHUNK

    @pl.when(pl.program_id(2) == 0)
    def _():
        st_ref[...] = s0_ref[0, 0]

    rix = lax.broadcasted_iota(jnp.int32, (ch, ch), 0)
    cix = lax.broadcasted_iota(jnp.int32, (ch, ch), 1)
    tri = (rix <= cix) if rev else (rix >= cix)
    tmat = tri.astype(BF16)
    ones = jnp.ones((ch, LANES), BF16)
    lane = lax.broadcasted_iota(jnp.int32, (ch, LANES), 1)
    lo_lanes = lane < 64
    whi = whi_ref[0]
    wlo = wlo_ref[0]
    bias = b_ref[0]
    end = 0 if rev else ch - 1

    def chunk(cc, carry):
        c = (ncb - 1 - cc) if rev else cc
        r0 = pl.multiple_of(c * ch, ch)
        lr = lr_ref[0, pl.ds(r0, ch), :]
        arg = _dot(lr, whi) + _dot(lr, wlo) + bias
        la = (jnp.minimum(arg, 0.0) - jnp.log(1.0 + jnp.exp(-jnp.abs(arg)))) * (1.0 / GLA_TAU)
        la3 = _split3(la)
        cum = _dot(tmat, la3[0]) + (_dot(tmat, la3[1]) + _dot(tmat, la3[2]))
        tot = _dot_tn(la3[0], ones) + (_dot_tn(la3[1], ones) + _dot_tn(la3[2], ones))
        gdec = jnp.exp(tot)
        cum_end = cum[end:end + 1, :]
        q = q_ref[0, pl.ds(r0, ch), :].astype(F32)
        k = k_ref[0, pl.ds(r0, ch), :].astype(F32)
        qd = (q * (HEAD_DIM ** -0.5) * jnp.exp(cum)).astype(BF16)
        kd = (k * jnp.exp(-cum)).astype(BF16)
        kdec = (k * jnp.exp(cum_end - cum)).astype(BF16)
        zero = jnp.zeros_like(qd)
        for h in range(2):
            qh = jnp.where(lo_lanes, qd, zero) if h == 0 else jnp.where(lo_lanes, zero, qd)
            a = jnp.where(tri, _dot_nt(qh, kd), 0.0)
            vh = v_ref[0, pl.ds(r0, ch), 128 * h:128 * (h + 1)]
            s_prev = st_ref[h]
            o = _dot(a.astype(BF16), vh) + _dot(qh, s_prev.astype(BF16))
            st_ref[h] = gdec * s_prev + _dot_tn(kdec, vh)
            if final:
                o = o + of_ref[0, pl.ds(r0, ch), 128 * h:128 * (h + 1)]
                ms = jnp.mean(o * o, axis=-1, keepdims=True)
                o = o * lax.rsqrt(ms + EPS) * gnw_ref[...]
                o_ref[0, pl.ds(r0, ch), 128 * h:128 * (h + 1)] = o.astype(BF16)
            else:
                o_ref[0, pl.ds(r0, ch), 128 * h:128 * (h + 1)] = o
        return carry

    lax.fori_loop(0, ncb, chunk, 0)
    sfin_ref[0, 0] = st_ref[...]


def _gla_pass(proj, w_pad_hi, w_pad_lo, b_pad, s0, rev, o_fwd=None, gnw=None):
    b, t, _ = proj.shape
    final = o_fwd is not None
    tb = min(t, 512)
    nblk = t // tb
    ncb = tb // GLA_CHUNK
    d = 1 if rev else 0

    def bi_map(i):
        return (nblk - 1 - i) if rev else i

    in_specs = [pl.BlockSpec((1, tb, 128), lambda bb, p, i: (bb, bi_map(i), P_CQ // 128 + p)),
                pl.BlockSpec((1, tb, 128), lambda bb, p, i: (bb, bi_map(i), P_CK // 128 + p)),
                pl.BlockSpec((1, tb, 256), lambda bb, p, i: (bb, bi_map(i), P_CV // 256 + p)),
                pl.BlockSpec((1, tb, 128), lambda bb, p, i: (bb, bi_map(i), P_LR // 128)),
                pl.BlockSpec((1, 128, 128), lambda bb, p, i: (2 * d + p, 0, 0)),
                pl.BlockSpec((1, 128, 128), lambda bb, p, i: (2 * d + p, 0, 0)),
                pl.BlockSpec((1, 1, 128), lambda bb, p, i: (2 * d + p, 0, 0)),
                pl.BlockSpec((1, 1, 2, 128, 128), lambda bb, p, i: (bb, p, 0, 0, 0))]
    args = [proj, proj, proj, proj, w_pad_hi, w_pad_lo, b_pad, s0]
    if final:
        in_specs += [pl.BlockSpec((1, tb, 256), lambda bb, p, i: (bb, bi_map(i), p)),
                     pl.BlockSpec((1, 128), lambda bb, p, i: (0, 0))]
        args += [o_fwd, gnw.reshape(1, 128)]
    out, s_fin = pl.pallas_call(
        functools.partial(_gla_kernel, rev=rev, ncb=ncb, final=final),
        grid=(b, 2, nblk),
        in_specs=in_specs,
        out_specs=[pl.BlockSpec((1, tb, 256), lambda bb, p, i: (bb, bi_map(i), p)),
                   pl.BlockSpec((1, 1, 2, 128, 128), lambda bb, p, i: (bb, p, 0, 0, 0))],
        out_shape=[jax.ShapeDtypeStruct((b, t, 512), BF16 if final else F32),
                   jax.ShapeDtypeStruct((b, 2, 2, 128, 128), F32)],
        scratch_shapes=[pltpu.VMEM((2, 128, 128), F32)],
        compiler_params=_cparams(("parallel", "parallel", "arbitrary")),
        name="gla_rev" if rev else "gla_fwd",
    )(*args)
    return out, s_fin


def _gla_decay_weights(w_alpha_up, b_alpha):
    w = jnp.zeros((2, 2, 128, 128), F32)
    for d in range(2):
        for p in range(2):
            w = w.at[d, p, 16 * d:16 * d + 16, :].set(w_alpha_up[d][:, 128 * p:128 * (p + 1)].astype(F32))
    w = w.reshape(4, 128, 128)
    hi = w.astype(BF16)
    lo = (w - hi.astype(F32)).astype(BF16)
    return hi, lo, b_alpha.astype(F32).reshape(4, 1, 128)


def _hy_pre_kernel(u0, u1, u2, p0, p1, p2, n0, n1, n2, w_ref, b_ref, x0_ref, z_ref, *, nblk):
    i = pl.program_id(1)
    tm = u0.shape[1]
    row = lax.broadcasted_iota(jnp.int32, (tm, 1), 0)
    has_prev = (i > 0).astype(F32)
    has_next = (i < nblk - 1).astype(F32)

    def conv(u_ref, p_ref, n_ref, j):
        u = u_ref[0].astype(F32)
        prev = p_ref[0, 7:8, :].astype(F32) * has_prev
        nxt = n_ref[0, 0:1, :].astype(F32) * has_next
        up = jnp.where(row == 0, prev, pltpu.roll(u, 1, 0))
        dn = jnp.where(row == tm - 1, nxt, pltpu.roll(u, tm - 1, 0))
        w = w_ref[:, 512 * j:512 * (j + 1)]
        return up * w[0:1] + u * w[1:2] + dn * w[2:3] + b_ref[:, 512 * j:512 * (j + 1)]

    x0_ref[0] = conv(u0, p0, n0, 0).astype(BF16)
    z_ref[0] = (conv(u1, p1, n1, 1) * conv(u2, p2, n2, 2)).astype(BF16)


def _hy_pre(proj, conv_w, conv_b):
    b, t, _ = proj.shape
    tm = min(t, 512)
    nblk = t // tm
    hb = tm // 8
    nrb = t // 8
    c0 = P_DU // 512
    main = [pl.BlockSpec((1, tm, 512), functools.partial(lambda bi, i, j: (bi, i, c0 + j), j=j)) for j in range(3)]
    prev = [pl.BlockSpec((1, 8, 512), functools.partial(lambda bi, i, j: (bi, jnp.maximum(i * hb - 1, 0), c0 + j), j=j))
            for j in range(3)]
    nxt = [pl.BlockSpec((1, 8, 512),
                        functools.partial(lambda bi, i, j: (bi, jnp.minimum((i + 1) * hb, nrb - 1), c0 + j), j=j))
           for j in range(3)]
    return pl.pallas_call(
        functools.partial(_hy_pre_kernel, nblk=nblk),
        grid=(b, nblk),
        in_specs=main + prev + nxt + [pl.BlockSpec((3, 1536), lambda bi, i: (0, 0)),
                                      pl.BlockSpec((1, 1536), lambda bi, i: (0, 0))],
        out_specs=[pl.BlockSpec((1, tm, 512), lambda bi, i: (bi, i, 0))] * 2,
        out_shape=[jax.ShapeDtypeStruct((b, t, 512), BF16)] * 2,
        compiler_params=_cparams(("parallel", "parallel")),
        name="hyena_pre",
    )(*([proj] * 9), conv_w.astype(F32), conv_b.astype(F32).reshape(1, 1536))


def _hy_filter_kernel(z_ref, w1, b1, w2, b2, w3, b3, fr, wo, dl, h_ref, l1_ref, *, tm, t_len):
    i = pl.program_id(0)
    f = fr[...]
    hh = jnp.sin(f * (_dot_x3(z_ref[...], w1[...]) + b1[...]))
    hh = jnp.sin(f * (_dot_x3(hh, w2[...]) + b2[...]))
    hh = jnp.sin(f * (_dot_x3(hh, w3[...]) + b3[...]))
    hh = _dot_x3(hh, wo[...])
    row = i * tm + lax.broadcasted_iota(jnp.int32, (tm, 1), 0)
    t = row.astype(F32) / (t_len - 1)
    decay = jnp.exp(-t * dl[...])
    h_f = hh[:, :HY_WIDTH] * decay
    h_b = jnp.where(row == 0, 0.0, hh[:, HY_WIDTH:] * decay)
    h_ref[0] = h_f.astype(BF16)
    h_ref[1] = h_b.astype(BF16)
    part = jnp.sum(jnp.abs(h_f) + jnp.abs(h_b), axis=0, keepdims=True)

    @pl.when(i == 0)
    def _():
        l1_ref[...] = part

    @pl.when(i > 0)
    def _():
        l1_ref[...] = l1_ref[...] + part


def _hy_filter(t_len, w1, b1, w2, b2, w3, b3, freq, wout):
    tm = min(t_len, 512)
    t = jnp.linspace(0.0, 1.0, t_len, dtype=F32)[:, None]
    bands = (HY_EMB - 1) // 2
    w_ang = 2.0 * math.pi * jnp.arange(t_len, dtype=F32)[:, None] / t_len
    f = jnp.linspace(1e-4, bands - 1, bands, dtype=F32)[None, :]
    z = jnp.concatenate([t, jnp.cos(f * w_ang), -jnp.sin(f * w_ang),
                         jnp.zeros((t_len, LANES - HY_EMB), F32)], axis=-1)
    w1p = jnp.concatenate([w1.astype(F32), jnp.zeros((LANES - HY_EMB, HY_FFN), F32)], axis=0)
    deltas = jnp.abs(jnp.linspace(math.log(HY_TARGET) / HY_SLOW_PCT, math.log(HY_TARGET) / HY_FAST_PCT,
                                  HY_WIDTH, dtype=F32))[None, :]
    full = lambda shape: pl.BlockSpec(shape, lambda i: tuple(0 for _ in shape))
    r = lambda v: v.astype(F32).reshape(1, -1)
    return pl.pallas_call(
        functools.partial(_hy_filter_kernel, tm=tm, t_len=t_len),
        grid=(t_len // tm,),
        in_specs=[pl.BlockSpec((tm, LANES), lambda i: (i, 0)),
                  full((LANES, HY_FFN)), full((1, HY_FFN)), full((HY_FFN, HY_FFN)), full((1, HY_FFN)),
                  full((HY_FFN, HY_FFN)), full((1, HY_FFN)), full((1, HY_FFN)), full((HY_FFN, 2 * HY_WIDTH)),
                  full((1, HY_WIDTH))],
        out_specs=[pl.BlockSpec((2, tm, HY_WIDTH), lambda i: (0, i, 0)),
                   pl.BlockSpec((1, HY_WIDTH), lambda i: (0, 0))],
        out_shape=[jax.ShapeDtypeStruct((2, t_len, HY_WIDTH), BF16), jax.ShapeDtypeStruct((1, HY_WIDTH), F32)],
        compiler_params=_cparams(("arbitrary",)),
        name="hyena_filter",
    )(z, w1p, r(b1), w2.astype(F32), r(b2), w3.astype(F32), r(b3), r(freq), wout.astype(F32), deltas)


def _dft_tables(t_len):
    n = 2 * t_len
    n2 = FFT_N2
    n1 = n // n2
    k1 = np.arange(n1)[:, None]
    j1 = np.arange(n1 // 2)[None, :]
    ang1 = 2.0 * np.pi * ((k1 * j1) % n1) / n1
    f1 = np.concatenate([np.cos(ang1), -np.sin(ang1)], axis=0)
    i1 = np.concatenate([np.cos(ang1).T, -np.sin(ang1).T], axis=1)
    a = np.arange(n2)
    ang2 = 2.0 * np.pi * ((a[:, None] * a[None, :]) % n2) / n2
    fr, fi = np.cos(ang2), -np.sin(ang2)
    fwd = np.block([[fr, -fi], [fi, fr]])
    inv = np.block([[fr, fi], [-fi, fr]])
    angt = 2.0 * np.pi * ((np.arange(n1)[:, None] * a[None, :]) % n) / n
    twc = np.cos(angt)[:, :, None]
    tws = np.sin(angt)[:, :, None]
    return (jnp.asarray(f1, F32).astype(BF16), jnp.asarray(i1, F32).astype(BF16), jnp.asarray(fwd, F32).astype(BF16),
            jnp.asarray(inv, F32).astype(BF16), jnp.asarray(twc, F32), jnp.asarray(tws, F32))


def _hy_s1_kernel(z_ref, f_ref, xr_ref, xi_ref):
    n1 = xr_ref.shape[1]
    y = _dot(f_ref[...], z_ref[0])
    xr_ref[0] = y[:n1].astype(BF16)
    xi_ref[0] = y[n1:].astype(BF16)


def _hy_s1(zv, f1):
    b, half, cols = zv.shape
    n1 = 2 * half
    tn = min(cols, 4096)
    return pl.pallas_call(
        _hy_s1_kernel,
        grid=(b, cols // tn),
        in_specs=[pl.BlockSpec((1, half, tn), lambda bi, j: (bi, 0, j)),
                  pl.BlockSpec((2 * n1, half), lambda bi, j: (0, 0))],
        out_specs=[pl.BlockSpec((1, n1, tn), lambda bi, j: (bi, 0, j))] * 2,
        out_shape=[jax.ShapeDtypeStruct((b, n1, cols), BF16)] * 2,
        compiler_params=_cparams(("parallel", "parallel")),
        name="hyena_dft_outer",
    )(zv, f1)


def _twiddle_fwd(xr, xi, c, s):
    return xr * c + xi * s, xi * c - xr * s


def _hy_filt_spec_kernel(xr_ref, xi_ref, twc_ref, tws_ref, fwd_ref, l1_ref, kr_ref, ki_ref, *, n):
    c = twc_ref[0]
    s = tws_ref[0]
    n2 = FFT_N2
    parts = []
    for j in range(2):
        ar, ai = _twiddle_fwd(xr_ref[j, 0].astype(F32), xi_ref[j, 0].astype(F32), c, s)
        y = _dot(fwd_ref[...], jnp.concatenate([ar, ai], axis=0).astype(BF16))
        parts.append((y[:n2], y[n2:]))
    scale = 1.0 / (l1_ref[...] * n)
    kr_ref[0] = (parts[0][0] + parts[1][0]) * scale
    ki_ref[0] = (parts[0][1] - parts[1][1]) * scale


def _hy_filt_spec(xr, xi, twc, tws, fwd, l1, n):
    _, n1, n2, w = xr.shape
    return pl.pallas_call(
        functools.partial(_hy_filt_spec_kernel, n=n),
        grid=(n1,),
        in_specs=[pl.BlockSpec((2, 1, n2, w), lambda k: (0, k, 0, 0)),
                  pl.BlockSpec((2, 1, n2, w), lambda k: (0, k, 0, 0)),
                  pl.BlockSpec((1, n2, 1), lambda k: (k, 0, 0)),
                  pl.BlockSpec((1, n2, 1), lambda k: (k, 0, 0)),
                  pl.BlockSpec((2 * n2, 2 * n2), lambda k: (0, 0)),
                  pl.BlockSpec((1, w), lambda k: (0, 0))],
        out_specs=[pl.BlockSpec((1, n2, w), lambda k: (k, 0, 0))] * 2,
        out_shape=[jax.ShapeDtypeStruct((n1, n2, w), F32)] * 2,
        compiler_params=_cparams(("parallel",)),
        name="hyena_filter_spectrum",
    )(xr, xi, twc, tws, fwd, l1)


def _hy_mid_kernel(xr_ref, xi_ref, twc_ref, tws_ref, fwd_ref, inv_ref, kr_ref, ki_ref, tr_ref, ti_ref):
    c = twc_ref[0]
    s = tws_ref[0]
    n2 = FFT_N2
    ar, ai = _twiddle_fwd(xr_ref[0, 0].astype(F32), xi_ref[0, 0].astype(F32), c, s)
    y = _dot(fwd_ref[...], jnp.concatenate([ar, ai], axis=0).astype(BF16))
    yr, yi = y[:n2], y[n2:]
    kr = kr_ref[0]
    ki = ki_ref[0]
    zr = yr * kr - yi * ki
    zi = yr * ki + yi * kr
    u = _dot(inv_ref[...], jnp.concatenate([zr, zi], axis=0).astype(BF16))
    ur, ui = u[:n2], u[n2:]
    tr_ref[0, 0] = (ur * c - ui * s).astype(BF16)
    ti_ref[0, 0] = (ur * s + ui * c).astype(BF16)


def _hy_mid(xr, xi, twc, tws, fwd, inv, kr, ki):
    b, n1, n2, w = xr.shape
    xspec = pl.BlockSpec((1, 1, n2, w), lambda bi, k: (bi, k, 0, 0))
    return pl.pallas_call(
        _hy_mid_kernel,
        grid=(b, n1),
        in_specs=[xspec, xspec,
                  pl.BlockSpec((1, n2, 1), lambda bi, k: (k, 0, 0)),
                  pl.BlockSpec((1, n2, 1), lambda bi, k: (k, 0, 0)),
                  pl.BlockSpec((2 * n2, 2 * n2), lambda bi, k: (0, 0)),
                  pl.BlockSpec((2 * n2, 2 * n2), lambda bi, k: (0, 0)),
                  pl.BlockSpec((1, n2, w), lambda bi, k: (k, 0, 0)),
                  pl.BlockSpec((1, n2, w), lambda bi, k: (k, 0, 0))],
        out_specs=[xspec, xspec],
        out_shape=[jax.ShapeDtypeStruct((b, n1, n2, w), BF16)] * 2,
        compiler_params=_cparams(("parallel", "parallel")),
        name="hyena_dft_inner",
    )(xr, xi, twc, tws, fwd, inv, kr, ki)


def _hy_post_kernel(tr_ref, ti_ref, i1_ref, x0_ref, z_ref, fb_ref, o_ref):
    t = jnp.concatenate([tr_ref[0], ti_ref[0]], axis=0)
    y = _dot(i1_ref[...], t)
    z = z_ref[0].astype(F32)
    o_ref[0] = (x0_ref[0].astype(F32) * (y + z * fb_ref[...])).astype(BF16)


def _hy_post(tr, ti, i1, x0v, zv, fb_tiled):
    b, n1, cols = tr.shape
    half = n1 // 2
    tn = fb_tiled.shape[1]
    hspec = pl.BlockSpec((1, half, tn), lambda bi, j: (bi, 0, j))
    tspec = pl.BlockSpec((1, n1, tn), lambda bi, j: (bi, 0, j))
    return pl.pallas_call(
        _hy_post_kernel,
        grid=(b, cols // tn),
        in_specs=[tspec, tspec, pl.BlockSpec((half, 2 * n1), lambda bi, j: (0, 0)), hspec, hspec,
                  pl.BlockSpec((1, tn), lambda bi, j: (0, 0))],
        out_specs=hspec,
        out_shape=jax.ShapeDtypeStruct((b, half, cols), BF16),
        compiler_params=_cparams(("parallel", "parallel")),
        name="hyena_dft_outer_inv",
    )(tr, ti, i1, x0v, zv, fb_tiled)


def _hyena_long(proj, conv_w, conv_b, filt, filt_bias):
    b, t, _ = proj.shape
    w = HY_WIDTH
    n = 2 * t
    n2 = FFT_N2
    n1 = n // n2
    f1, i1, fwd, inv, twc, tws = _dft_tables(t)
    hfb, l1 = _hy_filter(t, *filt)
    hr, hi = _hy_s1(hfb.reshape(2, n1 // 2, n2 * w), f1)
    kr, ki = _hy_filt_spec(hr.reshape(2, n1, n2, w), hi.reshape(2, n1, n2, w), twc, tws, fwd, l1, n)
    x0, z = _hy_pre(proj, conv_w, conv_b)
    zv = z.reshape(b, n1 // 2, n2 * w)
    xr, xi = _hy_s1(zv, f1)
    tr, ti = _hy_mid(xr.reshape(b, n1, n2, w), xi.reshape(b, n1, n2, w), twc, tws, fwd, inv, kr, ki)
    tn = min(n2 * w, 4096)
    fb_tiled = jnp.tile(filt_bias.astype(F32).reshape(1, w), (1, tn // w))
    d = _hy_post(tr.reshape(b, n1, n2 * w), ti.reshape(b, n1, n2 * w), i1, x0.reshape(b, n1 // 2, n2 * w), zv, fb_tiled)
    return d.reshape(b, t, w)


def _hy_ctx_kernel(x0_ref, z_ref, h_ref, l1_ref, cm_ref, sm_ref, cmt_ref, smt_ref, fb_ref, o_ref, *, n):
    cm = cm_ref[...]
    sm = sm_ref[...]

    def spectrum(v):
        return _dot(cm, v), -_dot(sm, v)

    z = z_ref[0]
    zr, zi = spectrum(z)
    ar, ai = spectrum(h_ref[0])
    br, bi = spectrum(h_ref[1])
    scale = 1.0 / (l1_ref[...] * n)
    kr = (ar + br) * scale
    ki = (ai - bi) * scale
    yr = (zr * kr - zi * ki).astype(BF16)
    yi = (zr * ki + zi * kr).astype(BF16)
    y = _dot(cmt_ref[...], yr) - _dot(smt_ref[...], yi)
    o_ref[0] = (x0_ref[0].astype(F32) * (y + z.astype(F32) * fb_ref[...])).astype(BF16)


def _hyena_short(proj, conv_w, conv_b, filt, filt_bias):
    b, t, _ = proj.shape
    w = HY_WIDTH
    n = 2 * t
    hfb, l1 = _hy_filter(t, *filt)
    x0, z = _hy_pre(proj, conv_w, conv_b)
    k = np.arange(n)[:, None]
    j = np.arange(t)[None, :]
    ang = 2.0 * np.pi * ((k * j) % n) / n
    cm, sm = np.cos(ang), np.sin(ang)
    tabs = [jnp.asarray(a, F32).astype(BF16) for a in (cm, sm, cm.T, sm.T)]
    full2 = lambda shape: pl.BlockSpec(shape, lambda bi: (0, 0))
    bspec = pl.BlockSpec((1, t, w), lambda bi: (bi, 0, 0))
    return pl.pallas_call(
        functools.partial(_hy_ctx_kernel, n=n),
        grid=(b,),
        in_specs=[bspec, bspec, pl.BlockSpec((2, t, w), lambda bi: (0, 0, 0)), full2((1, w)),
                  full2((n, t)), full2((n, t)), full2((t, n)), full2((t, n)), full2((1, w))],
        out_specs=bspec,
        out_shape=jax.ShapeDtypeStruct((b, t, w), BF16),
        compiler_params=_cparams(("parallel",)),
        name="hyena_ctx",
    )(x0, z, hfb, l1, *tabs, filt_bias.astype(F32).reshape(1, w))


def _layer(xc, xl, mod, cos128, sin128, norm_w, w_in, rpb, sink, w_alpha_up, b_alpha, gla_norm_w, conv_w, conv_b,
           filt, filt_bias, w_branch, w_out, final_w, with_ctx_out):
    b, length, d = xl.shape
    mod_l = mod[:b].reshape(b, 1, 3 * d)
    mod_c = jnp.broadcast_to(mod[b].reshape(1, 1, 3 * d), (b, 1, 3 * d))
    sh_l, sc_l, g_l = mod_l[..., :d], mod_l[..., d:2 * d], mod_l[..., 2 * d:]
    sh_c, sc_c, g_c = mod_c[..., :d], mod_c[..., d:2 * d], mod_c[..., 2 * d:]

    valid = jnp.asarray(_IN_PERM >= 0)
    w_perm = jnp.where(valid[None, :], jnp.take(w_in, jnp.asarray(np.maximum(_IN_PERM, 0)), axis=1), 0.0).astype(BF16)
    wb = w_branch.at[1].set(jnp.take(w_branch[1], jnp.asarray(_SWA_OUT_PERM), axis=0)).astype(BF16)
    wo = w_out.astype(BF16)

    proj_l = _inproj(xl, norm_w, sc_l, sh_l, w_perm)
    proj_c = _inproj(xc, norm_w, sc_c, sh_c, w_perm)

    y_a = _natten(proj_l, proj_c, _natten_bias_table(rpb))
    sink128 = jnp.broadcast_to(sink.astype(F32)[:, None], (SW_HEADS, LANES))
    qr, kr = _rope(proj_l, cos128, sin128)
    y_b = _swa(qr, kr, proj_l, proj_c, proj_c, sink128)
    whi, wlo, bpad = _gla_decay_weights(w_alpha_up, b_alpha)
    s_zero = jnp.zeros((b, 2, 2, 128, 128), F32)
    of_c, s_cf = _gla_pass(proj_c, whi, wlo, bpad, s_zero, rev=False)
    y_cc, s_cb = _gla_pass(proj_c, whi, wlo, bpad, s_zero, rev=True, o_fwd=of_c, gnw=gla_norm_w.astype(F32))
    of_l, _ = _gla_pass(proj_l, whi, wlo, bpad, s_cf, rev=False)
    y_c, _ = _gla_pass(proj_l, whi, wlo, bpad, s_cb, rev=True, o_fwd=of_l, gnw=gla_norm_w.astype(F32))
    y_d = _hyena_long(proj_l, conv_w, conv_b, filt, filt_bias)

    xl_new = _merge([y_a, y_b, y_c, y_d], proj_l, wb, wo, g_l, xl, final_w, final=not with_ctx_out)
    if with_ctx_out:
        ya_c = _ctx_attn(proj_c, sink128, swa=False)
        yb_c = _ctx_attn(proj_c, sink128, swa=True)
        yd_c = _hyena_short(proj_c, conv_w, conv_b, filt, filt_bias)
        xc = _merge([ya_c, yb_c, y_cc, yd_c], proj_c, wb, wo, g_c, xc, final_w, final=False)
    return xc, xl_new


def kernel(x, c, ctx, c_ctx, norm_w, w_mod, b_mod, w_in, rpb, sink, w_alpha_up, b_alpha, gla_norm_w, conv_w, conv_b,
           filt_w1, filt_b1, filt_w2, filt_b2, filt_w3, filt_b3, filt_freq, filt_wout, filt_bias, w_branch, w_out,
           final_norm_w):
    b, length, d = x.shape
    depth = norm_w.shape[0]
    cvec = jnp.zeros((8, d), F32).at[:b].set(c.astype(F32)).at[b].set(c_ctx.astype(F32))
    mod = _modulation(cvec, w_mod.astype(F32), b_mod.astype(F32))
    cos128, sin128 = _rope_tables(length)
    xc, xl = ctx, x
    for i in range(depth):
        filt = (filt_w1[i], filt_b1[i], filt_w2[i], filt_b2[i], filt_w3[i], filt_b3[i], filt_freq[i], filt_wout[i])
        xc, xl = _layer(xc, xl, mod[i], cos128, sin128, norm_w[i], w_in[i], rpb[i], sink[i], w_alpha_up[i],
                        b_alpha[i], gla_norm_w[i], conv_w[i], conv_b[i], filt, filt_bias[i], w_branch[i], w_out[i],
                        final_norm_w, with_ctx_out=(i < depth - 1))
    return xl
```

```python
import functools
import math

import numpy as np
import jax
import jax.numpy as jnp
from jax import lax
from jax.experimental import pallas as pl
from jax.experimental.pallas import tpu as pltpu

F32 = jnp.float32
BF16 = jnp.bfloat16

D_MODEL = 1024
GRID_W = 64
HEAD_DIM = 64
BRANCH_WIDTH = D_MODEL // 2
N_BRANCH = 4
NA_HEADS = 8
NA_KR = 8
NA_KC = 16
SW_HEADS = 8
SW_KV_HEADS = 2
SW_WINDOW = 128
SW_BLOCK = 128
GLA_HEADS = 4
GLA_DK = BRANCH_WIDTH // 2
GLA_DV = BRANCH_WIDTH
GLA_RANK = 16
GLA_TAU = 16.0
GLA_CHUNK = 64
HY_WIDTH = BRANCH_WIDTH
HY_EMB = 33
HY_FFN = 64
HY_TARGET = 1e-2
HY_FAST_PCT = 0.3
HY_SLOW_PCT = 1.5
ROPE_BASE = 10000.0
EPS = 1e-6
NEG_INF = -1e30
LANES = 128
FFT_N2 = 256
VMEM_LIMIT = 56 * 1024 * 1024

_IN_WIDTHS = (512, 512, 512, 512, 512, 128, 128, 512, 256, 256, 512, 16, 16, 512, 1536, 512, 4096)
_IN_OFF = np.concatenate([[0], np.cumsum(_IN_WIDTHS)])
(_O_AQ, _O_AK, _O_AV, _O_AG, _O_BQ, _O_BK, _O_BV, _O_BG, _O_CQ, _O_CK, _O_CV, _O_LRF, _O_LRB, _O_CG,
 _O_DU, _O_DG, _O_GM) = [int(v) for v in _IN_OFF[:-1]]
IN_TOTAL = int(_IN_OFF[-1])

P_AQ, P_AK, P_AV, P_AG = 0, 512, 1024, 1536
P_BQ, P_BG = 2048, 2560
P_CV, P_CG = 3072, 3584
P_DU, P_DG = 4096, 5632
P_GM = 6144
P_CQ, P_CK = 10240, 10496
P_BK, P_BV = 10752, 10880
P_LR = 11008
NP_COLS = 11264


def _swa_q_perm():
    idx = np.zeros(512, np.int64)
    for t in range(4):
        a, b = t, t + 4
        base = 128 * t
        idx[base + 0:base + 32] = 64 * a + np.arange(32)
        idx[base + 32:base + 64] = 64 * b + np.arange(32)
        idx[base + 64:base + 96] = 64 * a + 32 + np.arange(32)
        idx[base + 96:base + 128] = 64 * b + 32 + np.arange(32)
    return idx


def _swa_k_perm():
    idx = np.zeros(128, np.int64)
    idx[0:32] = np.arange(32)
    idx[32:64] = 64 + np.arange(32)
    idx[64:96] = 32 + np.arange(32)
    idx[96:128] = 96 + np.arange(32)
    return idx


def _swa_out_perm():
    idx = np.zeros(512, np.int64)
    for t in range(4):
        idx[128 * t:128 * t + 64] = 64 * t + np.arange(64)
        idx[128 * t + 64:128 * t + 128] = 64 * (t + 4) + np.arange(64)
    return idx


def _build_in_perm():
    perm = np.full(NP_COLS, -1, np.int64)

    def put(p, o, w):
        perm[p:p + w] = o + np.arange(w)

    put(P_AQ, _O_AQ, 512); put(P_AK, _O_AK, 512); put(P_AV, _O_AV, 512); put(P_AG, _O_AG, 512)
    perm[P_BQ:P_BQ + 512] = _O_BQ + _swa_q_perm()
    perm[P_BG:P_BG + 512] = _O_BG + _swa_out_perm()
    put(P_CV, _O_CV, 512); put(P_CG, _O_CG, 512)
    put(P_DU, _O_DU, 1536); put(P_DG, _O_DG, 512)
    put(P_GM, _O_GM, 4096)
    put(P_CQ, _O_CQ, 256); put(P_CK, _O_CK, 256)
    perm[P_BK:P_BK + 128] = _O_BK + _swa_k_perm()
    put(P_BV, _O_BV, 128)
    put(P_LR, _O_LRF, 16); put(P_LR + 16, _O_LRB, 16)
    return perm


def _build_half_cols():
    s = np.ones(NP_COLS, np.float32)
    for p, w in ((P_AG, 512), (P_BG, 512), (P_CG, 512), (P_DG, 512), (P_GM, 4096)):
        s[p:p + w] = 0.5
    return s


_HALF_COLS = _build_half_cols()
_IN_PERM = _build_in_perm()
_SWA_OUT_PERM = _swa_out_perm()


def _permute(w, perm, axis):
    pieces = []
    i = 0
    n = len(perm)
    while i < n:
        j = i + 1
        if perm[i] < 0:
            while j < n and perm[j] < 0:
                j += 1
            shape = list(w.shape)
            shape[axis] = j - i
            pieces.append(jnp.zeros(shape, w.dtype))
        else:
            while j < n and perm[j] == perm[j - 1] + 1:
                j += 1
            pieces.append(lax.slice_in_dim(w, int(perm[i]), int(perm[j - 1]) + 1, axis=axis))
        i = j
    return jnp.concatenate(pieces, axis=axis)


def _cparams(sem):
    return pltpu.CompilerParams(dimension_semantics=sem, vmem_limit_bytes=VMEM_LIMIT)


def _sigmoid(x):
    return 1.0 / (1.0 + jnp.exp(-x))


def _sigmoid_tanh(x):
    return 0.5 * jnp.tanh(0.5 * x) + 0.5


def _split3(a):
    hi = a.astype(BF16)
    r1 = a - hi.astype(F32)
    mid = r1.astype(BF16)
    lo = (r1 - mid.astype(F32)).astype(BF16)
    return hi, mid, lo


def _dot(a, b):
    return jnp.dot(a, b, preferred_element_type=F32)


def _dot_nt(a, b):
    return lax.dot_general(a, b, (((1,), (1,)), ((), ())), preferred_element_type=F32)


def _dot_tn(a, b):
    return lax.dot_general(a, b, (((0,), (0,)), ((), ())), preferred_element_type=F32)


def _dot_x3(a, b):
    ah, am, _ = _split3(a)
    bh, bm, _ = _split3(b)
    return _dot(ah, bh) + (_dot(ah, bm) + _dot(am, bh))


def _mod_kernel(c_ref, w_ref, b_ref, o_ref):
    c = c_ref[...]
    s = c * _sigmoid(c)
    o_ref[0] = _dot_x3(s, w_ref[0]) + b_ref[0]


def _modulation(cvec, w_mod, b_mod):
    depth, d, n = w_mod.shape
    tn = 512
    return pl.pallas_call(
        _mod_kernel,
        grid=(depth, n // tn),
        in_specs=[pl.BlockSpec((8, d), lambda l, j: (0, 0)),
                  pl.BlockSpec((1, d, tn), lambda l, j: (l, 0, j)),
                  pl.BlockSpec((1, 1, tn), lambda l, j: (l, 0, j))],
        out_specs=pl.BlockSpec((1, 8, tn), lambda l, j: (l, 0, j)),
        out_shape=jax.ShapeDtypeStruct((depth, 8, n), F32),
        compiler_params=_cparams(("parallel", "parallel")),
        name="modulation",
    )(cvec, w_mod, b_mod.reshape(depth, 1, n))


def _inproj_kernel(x_ref, nw_ref, sc_ref, sh_ref, w_ref, o_ref, h_ref):
    @pl.when(pl.program_id(2) == 0)
    def _():
        x = x_ref[0]
        ms = jnp.mean(x * x, axis=-1, keepdims=True)
        y = x * lax.rsqrt(ms + EPS) * nw_ref[...]
        h_ref[...] = (y * (1.0 + sc_ref[0]) + sh_ref[0]).astype(BF16)

    o_ref[0] = _dot(h_ref[...], w_ref[...]).astype(BF16)


def _inproj(x, norm_w, scale, shift, w_perm):
    b, t, d = x.shape
    n = w_perm.shape[1]
    tm = min(t, 1024)
    tn = 1024
    return pl.pallas_call(
        _inproj_kernel,
        grid=(b, t // tm, n // tn),
        in_specs=[pl.BlockSpec((1, tm, d), lambda bi, i, j: (bi, i, 0)),
                  pl.BlockSpec((1, d), lambda bi, i, j: (0, 0)),
                  pl.BlockSpec((1, 1, d), lambda bi, i, j: (bi, 0, 0)),
                  pl.BlockSpec((1, 1, d), lambda bi, i, j: (bi, 0, 0)),
                  pl.BlockSpec((d, tn), lambda bi, i, j: (0, j))],
        out_specs=pl.BlockSpec((1, tm, tn), lambda bi, i, j: (bi, i, j)),
        out_shape=jax.ShapeDtypeStruct((b, t, n), BF16),
        scratch_shapes=[pltpu.VMEM((tm, d), BF16)],
        compiler_params=_cparams(("parallel", "parallel", "arbitrary")),
        name="inproj",
    )(x, norm_w.reshape(1, d), scale, shift, w_perm)


def _merge_kernel(ya, yb, yc, yd, ga, gb, gc, gd, m0, m1, m2, m3, wb_ref, wo_ref, g_ref, x_ref, fw_ref,
                  o_ref, *, final):
    acc = None
    for i, (y, g, gm) in enumerate(((ya, ga, m0), (yb, gb, m1), (yc, gc, m2), (yd, gd, m3))):
        hg = g[0].astype(F32)
        yg = y[0].astype(F32) * hg
        ys = (yg + yg * jnp.tanh(hg)).astype(BF16)
        hp = _dot(ys, wb_ref[i])
        t = hp + hp * jnp.tanh(gm[0].astype(F32))
        acc = t if acc is None else acc + t
    out = x_ref[0] + g_ref[0] * _dot(acc.astype(BF16), wo_ref[...])
    if final:
        ms = jnp.mean(out * out, axis=-1, keepdims=True)
        out = out * lax.rsqrt(ms + EPS) * fw_ref[...]
    o_ref[0] = out


def _merge(ys, proj, w_branch, w_out, gate, x, final_w, final):
    b, t, d = x.shape
    bw = BRANCH_WIDTH
    tm = min(t, 512)
    yspec = pl.BlockSpec((1, tm, bw), lambda bi, i: (bi, i, 0))

    def pspec(col, width):
        blk = col // width
        return pl.BlockSpec((1, tm, width), lambda bi, i: (bi, i, blk))

    in_specs = ([yspec] * 4
                + [pspec(P_AG, bw), pspec(P_BG, bw), pspec(P_CG, bw), pspec(P_DG, bw)]
                + [pspec(P_GM + k * d, d) for k in range(N_BRANCH)]
                + [pl.BlockSpec((N_BRANCH, bw, d), lambda bi, i: (0, 0, 0)),
                   pl.BlockSpec((d, d), lambda bi, i: (0, 0)),
                   pl.BlockSpec((1, 1, d), lambda bi, i: (bi, 0, 0)),
                   pl.BlockSpec((1, tm, d), lambda bi, i: (bi, i, 0)),
                   pl.BlockSpec((1, d), lambda bi, i: (0, 0))])
    return pl.pallas_call(
        functools.partial(_merge_kernel, final=final),
        grid=(b, t // tm),
        in_specs=in_specs,
        out_specs=pl.BlockSpec((1, tm, d), lambda bi, i: (bi, i, 0)),
        out_shape=jax.ShapeDtypeStruct((b, t, d), F32),
        compiler_params=_cparams(("parallel", "parallel")),
        name="merge",
    )(*ys, *([proj] * 8), w_branch, w_out, gate, x, final_w.reshape(1, d))


def _rope_kernel(q_ref, k_ref, cos_ref, sin_ref, qo_ref, ko_ref):
    cos = cos_ref[...]
    sin = sin_ref[...]

    def rot(x):
        return x * cos + pltpu.roll(x, 64, 1) * sin

    for t in range(4):
        q = q_ref[0, :, 128 * t:128 * (t + 1)].astype(F32)
        qo_ref[0, :, 128 * t:128 * (t + 1)] = (rot(q) * HEAD_DIM ** -0.5).astype(BF16)
    ko_ref[0] = rot(k_ref[0].astype(F32)).astype(BF16)


def _rope(proj, cos128, sin128):
    b, t, _ = proj.shape
    tm = min(t, 1024)
    return pl.pallas_call(
        _rope_kernel,
        grid=(b, t // tm),
        in_specs=[pl.BlockSpec((1, tm, 512), lambda bi, i: (bi, i, P_BQ // 512)),
                  pl.BlockSpec((1, tm, 128), lambda bi, i: (bi, i, P_BK // 128)),
                  pl.BlockSpec((tm, 128), lambda bi, i: (i, 0)),
                  pl.BlockSpec((tm, 128), lambda bi, i: (i, 0))],
        out_specs=[pl.BlockSpec((1, tm, 512), lambda bi, i: (bi, i, 0)),
                   pl.BlockSpec((1, tm, 128), lambda bi, i: (bi, i, 0))],
        out_shape=[jax.ShapeDtypeStruct((b, t, 512), BF16), jax.ShapeDtypeStruct((b, t, 128), BF16)],
        compiler_params=_cparams(("parallel", "parallel")),
        name="rope",
    )(proj, proj, cos128, sin128)


def _rope_tables(length):
    t = jnp.arange(length, dtype=jnp.int32)
    row = (t // GRID_W).astype(F32)
    col = (t % GRID_W).astype(F32)
    n_freq = HEAD_DIM // 4
    inv = ROPE_BASE ** (-jnp.arange(n_freq, dtype=F32) / n_freq)
    ang = jnp.concatenate([row[:, None] * inv, col[:, None] * inv], axis=-1)
    cos, sin = jnp.cos(ang), jnp.sin(ang)
    return jnp.tile(cos, (1, 4)), jnp.concatenate([-sin, -sin, sin, sin], axis=-1)


def _swa_kernel(q_ref, k_ref, v_ref, kc_ref, vc_ref, sink_ref, mask_ref, o_ref, *, length):
    i = pl.program_id(1)
    blk = SW_BLOCK
    kw = 3 * blk
    nb = length // blk
    start = pl.multiple_of(jnp.clip((i - 1) * blk, 0, length - kw), blk)
    kwin = k_ref[0, pl.ds(start, kw), :]
    vwin = v_ref[0, pl.ds(start, kw), :]
    kc = kc_ref[0]
    vc = vc_ref[0]
    variant = jnp.where(i == 0, 0, jnp.where(i == nb - 1, 2, 1))
    lane = lax.broadcasted_iota(jnp.int32, (blk, LANES), 1)
    a_lanes = (lane // 32) % 2 == 0
    lo_lanes = lane < 64
    top = lax.broadcasted_iota(jnp.int32, (2 * blk, 1), 0) < blk
    for t in range(4):
        qt = q_ref[0, :, 128 * t:128 * (t + 1)]
        zero = jnp.zeros_like(qt)
        qq = jnp.concatenate([jnp.where(a_lanes, qt, zero), jnp.where(a_lanes, zero, qt)], axis=0)
        sw = _dot_nt(qq, kwin) + mask_ref[variant]
        sc = _dot_nt(qq, kc)
        sk = jnp.where(top, sink_ref[t:t + 1, 0:1], sink_ref[t + 4:t + 5, 0:1])
        m = jnp.maximum(jnp.maximum(jnp.max(sw, axis=-1, keepdims=True), jnp.max(sc, axis=-1, keepdims=True)), sk)
        pw = jnp.exp(sw - m)
        pc = jnp.exp(sc - m)
        den = jnp.sum(pw, axis=-1, keepdims=True) + jnp.sum(pc, axis=-1, keepdims=True) + jnp.exp(sk - m)
        o = (_dot(pw.astype(BF16), vwin) + _dot(pc.astype(BF16), vc)) / den
        o_ref[0, :, 128 * t:128 * (t + 1)] = jnp.where(lo_lanes, o[:blk], o[blk:]).astype(BF16)


def _swa(qr, kr, proj_l, kc_rot, proj_c, sink128):
    b, length, _ = qr.shape
    lc = proj_c.shape[1]
    assert length >= 3 * SW_BLOCK
    r = np.arange(2 * SW_BLOCK)[:, None] % SW_BLOCK
    c = np.arange(3 * SW_BLOCK)[None, :]
    mask_tab = jnp.asarray(np.stack([np.where(np.abs(v * SW_BLOCK + r - c) <= SW_WINDOW, 0.0, NEG_INF)
                                     for v in range(3)]), F32)
    return pl.pallas_call(
        functools.partial(_swa_kernel, length=length),
        grid=(b, length // SW_BLOCK),
        in_specs=[pl.BlockSpec((1, SW_BLOCK, 512), lambda bi, i: (bi, i, 0)),
                  pl.BlockSpec((1, length, 128), lambda bi, i: (bi, 0, 0)),
                  pl.BlockSpec((1, length, 128), lambda bi, i: (bi, 0, P_BV // 128)),
                  pl.BlockSpec((1, lc, 128), lambda bi, i: (bi, 0, P_BK // 128)),
                  pl.BlockSpec((1, lc, 128), lambda bi, i: (bi, 0, P_BV // 128)),
                  pl.BlockSpec((8, 128), lambda bi, i: (0, 0)),
                  pl.BlockSpec((3, 2 * SW_BLOCK, 3 * SW_BLOCK), lambda bi, i: (0, 0, 0))],
        out_specs=pl.BlockSpec((1, SW_BLOCK, 512), lambda bi, i: (bi, i, 0)),
        out_shape=jax.ShapeDtypeStruct((b, length, 512), BF16),
        compiler_params=_cparams(("parallel", "arbitrary")),
        name="swa",
    )(qr, kr, proj_l, kc_rot, proj_c, sink128, mask_tab)


def _ctx_attn_kernel(q_ref, k_ref, v_ref, sink_ref, o_ref, *, swa):
    t_len = q_ref.shape[1]
    lane = lax.broadcasted_iota(jnp.int32, (t_len, LANES), 1)
    lo_lanes = lane < 64
    a_lanes = ((lane // 32) % 2 == 0) if swa else lo_lanes
    top = lax.broadcasted_iota(jnp.int32, (2 * t_len, 1), 0) < t_len
    for t in range(4):
        qt = (q_ref[0, :, 128 * t:128 * (t + 1)].astype(F32) * HEAD_DIM ** -0.5).astype(BF16)
        zero = jnp.zeros_like(qt)
        qq = jnp.concatenate([jnp.where(a_lanes, qt, zero), jnp.where(a_lanes, zero, qt)], axis=0)
        if swa:
            kt, vt = k_ref[0], v_ref[0]
        else:
            kt, vt = k_ref[0, :, 128 * t:128 * (t + 1)], v_ref[0, :, 128 * t:128 * (t + 1)]
        s = _dot_nt(qq, kt)
        m = jnp.max(s, axis=-1, keepdims=True)
        if swa:
            sk = jnp.where(top, sink_ref[t:t + 1, 0:1], sink_ref[t + 4:t + 5, 0:1])
            m = jnp.maximum(m, sk)
        p = jnp.exp(s - m)
        den = jnp.sum(p, axis=-1, keepdims=True)
        if swa:
            den = den + jnp.exp(sk - m)
        o = _dot(p.astype(BF16), vt) / den
        o_ref[0, :, 128 * t:128 * (t + 1)] = jnp.where(lo_lanes, o[:t_len], o[t_len:]).astype(BF16)


def _ctx_attn(proj_c, sink128, swa):
    b, lc, _ = proj_c.shape
    if swa:
        qs = pl.BlockSpec((1, lc, 512), lambda bi: (bi, 0, P_BQ // 512))
        ks = pl.BlockSpec((1, lc, 128), lambda bi: (bi, 0, P_BK // 128))
        vs = pl.BlockSpec((1, lc, 128), lambda bi: (bi, 0, P_BV // 128))
    else:
        qs = pl.BlockSpec((1, lc, 512), lambda bi: (bi, 0, P_AQ // 512))
        ks = pl.BlockSpec((1, lc, 512), lambda bi: (bi, 0, P_AK // 512))
        vs = pl.BlockSpec((1, lc, 512), lambda bi: (bi, 0, P_AV // 512))
    return pl.pallas_call(
        functools.partial(_ctx_attn_kernel, swa=swa),
        grid=(b,),
        in_specs=[qs, ks, vs, pl.BlockSpec((8, 128), lambda bi: (0, 0))],
        out_specs=pl.BlockSpec((1, lc, 512), lambda bi: (bi, 0, 0)),
        out_shape=jax.ShapeDtypeStruct((b, lc, 512), BF16),
        compiler_params=_cparams(("parallel",)),
        name="ctx_attn_swa" if swa else "ctx_attn_na",
    )(proj_c, proj_c, proj_c, sink128)


def _natten_kernel(q_ref, k_ref, v_ref, kc_ref, vc_ref, bias_ref, o_ref, *, rows, rows_per_step):
    blk = pl.program_id(2)
    kc = kc_ref[0]
    vc = vc_ref[0]
    w = GRID_W
    nkeys = NA_KR * w
    lo_all = lax.broadcasted_iota(jnp.int32, (rows_per_step * w, LANES), 1) < 64
    lo_row = lax.broadcasted_iota(jnp.int32, (w, LANES), 1) < 64
    q_all = (q_ref[0].astype(F32) * HEAD_DIM ** -0.5).astype(BF16)
    zero = jnp.zeros_like(q_all)
    q_lo = jnp.where(lo_all, q_all, zero)
    q_hi = jnp.where(lo_all, zero, q_all)
    sc_lo = _dot_nt(q_lo, kc)
    sc_hi = _dot_nt(q_hi, kc)
    o_rows, pc_lo, pc_hi = [], [], []

    def scores(rr):
        r = blk * rows_per_step + rr
        rs = jnp.clip(r - NA_KR // 2, 0, rows - NA_KR)
        off = rs - r + (NA_KR - 1)
        k0 = pl.multiple_of(rs * w, w)
        sl = slice(rr * w, (rr + 1) * w)
        qq = jnp.concatenate([q_lo[sl], q_hi[sl]], axis=0)
        s = _dot_nt(qq, k_ref[0, pl.ds(k0, nkeys), :]) + bias_ref[off, 0]
        sc = jnp.concatenate([sc_lo[sl], sc_hi[sl]], axis=0)
        return s, sc, k0

    ahead = 2
    queue = [scores(rr) for rr in range(ahead)]
    for rr in range(rows_per_step):
        s, sc, k0 = queue.pop(0)
        if rr + ahead < rows_per_step:
            queue.append(scores(rr + ahead))
        vrows = v_ref[0, pl.ds(k0, nkeys), :]
        m = jnp.maximum(jnp.max(s, axis=-1, keepdims=True), jnp.max(sc, axis=-1, keepdims=True))
        p = jnp.exp(s - m)
        pc = jnp.exp(sc - m)
        rden = 1.0 / (jnp.sum(p, axis=-1, keepdims=True) + jnp.sum(pc, axis=-1, keepdims=True))
        o = _dot((p * rden).astype(BF16), vrows)
        pcn = (pc * rden).astype(BF16)
        o_rows.append(jnp.where(lo_row, o[:w], o[w:]))
        pc_lo.append(pcn[:w])
        pc_hi.append(pcn[w:])
    oc = jnp.where(lo_all, _dot(jnp.concatenate(pc_lo, axis=0), vc), _dot(jnp.concatenate(pc_hi, axis=0), vc))
    o_ref[0] = (jnp.concatenate(o_rows, axis=0) + oc).astype(BF16)


def _natten(proj_l, proj_c, bias_tab):
    b, length, _ = proj_l.shape
    lc = proj_c.shape[1]
    rows = length // GRID_W
    assert rows >= NA_KR
    rps = 8
    tq = rps * GRID_W
    nk = NA_KR * GRID_W
    return pl.pallas_call(
        functools.partial(_natten_kernel, rows=rows, rows_per_step=rps),
        grid=(b, 4, rows // rps),
        in_specs=[pl.BlockSpec((1, tq, 128), lambda bi, hp, i: (bi, i, P_AQ // 128 + hp)),
                  pl.BlockSpec((1, length, 128), lambda bi, hp, i: (bi, 0, P_AK // 128 + hp)),
                  pl.BlockSpec((1, length, 128), lambda bi, hp, i: (bi, 0, P_AV // 128 + hp)),
                  pl.BlockSpec((1, lc, 128), lambda bi, hp, i: (bi, 0, P_AK // 128 + hp)),
                  pl.BlockSpec((1, lc, 128), lambda bi, hp, i: (bi, 0, P_AV // 128 + hp)),
                  pl.BlockSpec((NA_KR, 1, 2 * GRID_W, nk), lambda bi, hp, i: (0, hp, 0, 0))],
        out_specs=pl.BlockSpec((1, tq, 128), lambda bi, hp, i: (bi, i, hp)),
        out_shape=jax.ShapeDtypeStruct((b, length, 512), BF16),
        compiler_params=_cparams(("parallel", "parallel", "arbitrary")),
        name="natten",
    )(proj_l, proj_l, proj_l, proj_c, proj_c, bias_tab)


def _natten_bias_table(rpb):
    qc = np.arange(GRID_W)[:, None]
    kcol = np.arange(GRID_W)[None, :]
    dc = np.clip(kcol - qc + (NA_KC - 1), 0, 2 * NA_KC - 2)
    cs = np.clip(qc - NA_KC // 2, 0, GRID_W - NA_KC)
    cmask = (kcol >= cs) & (kcol < cs + NA_KC)
    dr = np.arange(NA_KR)[:, None] + np.arange(NA_KR)[None, :]
    g = rpb[:, dr[:, :, None, None], dc[None, None, :, :]]
    g = jnp.where(cmask[None, None, None], g.astype(F32), NEG_INF)
    return g.transpose(1, 0, 3, 2, 4).reshape(NA_KR, NA_HEADS // 2, 2 * GRID_W, NA_KR * GRID_W)


def _gla_kernel(*refs, rev, ncb, final):
    if final:
        (q_ref, k_ref, v_ref, lr_ref, whi_ref, wlo_ref, b_ref, s0_ref, of_ref, gnw_ref,
         o_ref, sfin_ref, st_ref) = refs
    else:
        (q_ref, k_ref, v_ref, lr_ref, whi_ref, wlo_ref, b_ref, s0_ref,
         o_ref, sfin_ref, st_ref) = refs
    ch = GLA_CHUNK

    @pl.when(pl.program_id(2) == 0)
    def _():
        st_ref[...] = s0_ref[0, 0]

    tb = ncb * ch
    rix = lax.broadcasted_iota(jnp.int32, (ch, ch), 0)
    cix = lax.broadcasted_iota(jnp.int32, (ch, ch), 1)
    tri = (rix <= cix) if rev else (rix >= cix)
    lo_lanes = lax.broadcasted_iota(jnp.int32, (tb, LANES), 1) < 64
    end = 0 if rev else ch - 1

    lr = lr_ref[0]
    arg = _dot(lr, whi_ref[0]) + _dot(lr, wlo_ref[0]) + b_ref[0]
    cum = (jnp.minimum(arg, 0.0) - jnp.log(1.0 + jnp.exp(-jnp.abs(arg)))) * (1.0 / GLA_TAU)
    pos = lax.broadcasted_iota(jnp.int32, (tb, 1), 0) % ch
    step = 1
    while step < ch:
        if rev:
            cum = cum + jnp.where(pos < ch - step, pltpu.roll(cum, tb - step, 0), 0.0)
        else:
            cum = cum + jnp.where(pos >= step, pltpu.roll(cum, step, 0), 0.0)
        step *= 2
    q = q_ref[0].astype(F32)
    k = k_ref[0].astype(F32)
    qd = (q * (HEAD_DIM ** -0.5) * jnp.exp(cum)).astype(BF16)
    kd = (k * jnp.exp(-cum)).astype(BF16)
    zero = jnp.zeros_like(qd)
    qd_h = (jnp.where(lo_lanes, qd, zero), jnp.where(lo_lanes, zero, qd))

    order = [(ncb - 1 - cc) if rev else cc for cc in range(ncb)]
    intra = {}
    for c in order:
        sl = slice(c * ch, (c + 1) * ch)
        cum_c = cum[sl]
        cum_end = cum_c[end:end + 1, :]
        kdec = (k[sl] * jnp.exp(cum_end - cum_c)).astype(BF16)
        kd_c = kd[sl]
        for h in range(2):
            vh = v_ref[0, sl, 128 * h:128 * (h + 1)]
            a = jnp.where(tri, _dot_nt(qd_h[h][sl], kd_c), 0.0)
            intra[c, h] = (_dot(a.astype(BF16), vh), _dot_tn(vh, kdec), jnp.exp(cum_end))
    st = [st_ref[0], st_ref[1]]
    for c in order:
        sl = slice(c * ch, (c + 1) * ch)
        for h in range(2):
            o_intra, kv_t, gdec = intra[c, h]
            o = o_intra + _dot_nt(qd_h[h][sl], st[h].astype(BF16))
            st[h] = st[h] * gdec + kv_t
            if final:
                o = o + of_ref[0, sl, 128 * h:128 * (h + 1)]
                ms = jnp.mean(o * o, axis=-1, keepdims=True)
                o = o * lax.rsqrt(ms + EPS) * gnw_ref[...]
                o_ref[0, sl, 128 * h:128 * (h + 1)] = o.astype(BF16)
            else:
                o_ref[0, sl, 128 * h:128 * (h + 1)] = o
    st_ref[0] = st[0]
    st_ref[1] = st[1]
    sfin_ref[0, 0] = st_ref[...]


def _gla_pass(proj, w_pad_hi, w_pad_lo, b_pad, s0, rev, o_fwd=None, gnw=None):
    b, t, _ = proj.shape
    final = o_fwd is not None
    tb = min(t, 512)
    nblk = t // tb
    ncb = tb // GLA_CHUNK
    d = 1 if rev else 0

    def bi_map(i):
        return (nblk - 1 - i) if rev else i

    in_specs = [pl.BlockSpec((1, tb, 128), lambda bb, p, i: (bb, bi_map(i), P_CQ // 128 + p)),
                pl.BlockSpec((1, tb, 128), lambda bb, p, i: (bb, bi_map(i), P_CK // 128 + p)),
                pl.BlockSpec((1, tb, 256), lambda bb, p, i: (bb, bi_map(i), P_CV // 256 + p)),
                pl.BlockSpec((1, tb, 128), lambda bb, p, i: (bb, bi_map(i), P_LR // 128)),
                pl.BlockSpec((1, 128, 128), lambda bb, p, i: (2 * d + p, 0, 0)),
                pl.BlockSpec((1, 128, 128), lambda bb, p, i: (2 * d + p, 0, 0)),
                pl.BlockSpec((1, 1, 128), lambda bb, p, i: (2 * d + p, 0, 0)),
                pl.BlockSpec((1, 1, 2, 128, 128), lambda bb, p, i: (bb, p, 0, 0, 0))]
    args = [proj, proj, proj, proj, w_pad_hi, w_pad_lo, b_pad, s0]
    if final:
        in_specs += [pl.BlockSpec((1, tb, 256), lambda bb, p, i: (bb, bi_map(i), p)),
                     pl.BlockSpec((1, 128), lambda bb, p, i: (0, 0))]
        args += [o_fwd, gnw.reshape(1, 128)]
    out, s_fin = pl.pallas_call(
        functools.partial(_gla_kernel, rev=rev, ncb=ncb, final=final),
        grid=(b, 2, nblk),
        in_specs=in_specs,
        out_specs=[pl.BlockSpec((1, tb, 256), lambda bb, p, i: (bb, bi_map(i), p)),
                   pl.BlockSpec((1, 1, 2, 128, 128), lambda bb, p, i: (bb, p, 0, 0, 0))],
        out_shape=[jax.ShapeDtypeStruct((b, t, 512), BF16 if final else F32),
                   jax.ShapeDtypeStruct((b, 2, 2, 128, 128), F32)],
        scratch_shapes=[pltpu.VMEM((2, 128, 128), F32)],
        compiler_params=_cparams(("parallel", "parallel", "arbitrary")),
        name="gla_rev" if rev else "gla_fwd",
    )(*args)
    return out, s_fin


def _gla_decay_weights(w_alpha_up, b_alpha):
    w = jnp.zeros((2, 2, 128, 128), F32)
    for d in range(2):
        for p in range(2):
            w = w.at[d, p, 16 * d:16 * d + 16, :].set(w_alpha_up[d][:, 128 * p:128 * (p + 1)].astype(F32))
    w = w.reshape(4, 128, 128)
    hi = w.astype(BF16)
    lo = (w - hi.astype(F32)).astype(BF16)
    return hi, lo, b_alpha.astype(F32).reshape(4, 1, 128)


def _hy_pre_kernel(u0, u1, u2, p0, p1, p2, n0, n1, n2, w_ref, b_ref, x0_ref, z_ref, *, nblk):
    i = pl.program_id(1)
    tm = u0.shape[1]
    row = lax.broadcasted_iota(jnp.int32, (tm, 1), 0)
    has_prev = (i > 0).astype(F32)
    has_next = (i < nblk - 1).astype(F32)

    def conv(u_ref, p_ref, n_ref, j):
        u = u_ref[0].astype(F32)
        prev = p_ref[0, 7:8, :].astype(F32) * has_prev
        nxt = n_ref[0, 0:1, :].astype(F32) * has_next
        up = jnp.where(row == 0, prev, pltpu.roll(u, 1, 0))
        dn = jnp.where(row == tm - 1, nxt, pltpu.roll(u, tm - 1, 0))
        w = w_ref[:, 512 * j:512 * (j + 1)]
        return up * w[0:1] + u * w[1:2] + dn * w[2:3] + b_ref[:, 512 * j:512 * (j + 1)]

    x0_ref[0] = conv(u0, p0, n0, 0).astype(BF16)
    z_ref[0] = (conv(u1, p1, n1, 1) * conv(u2, p2, n2, 2)).astype(BF16)


def _hy_pre(proj, conv_w, conv_b):
    b, t, _ = proj.shape
    tm = min(t, 512)
    nblk = t // tm
    hb = tm // 8
    nrb = t // 8
    c0 = P_DU // 512
    main = [pl.BlockSpec((1, tm, 512), functools.partial(lambda bi, i, j: (bi, i, c0 + j), j=j)) for j in range(3)]
    prev = [pl.BlockSpec((1, 8, 512), functools.partial(lambda bi, i, j: (bi, jnp.maximum(i * hb - 1, 0), c0 + j), j=j))
            for j in range(3)]
    nxt = [pl.BlockSpec((1, 8, 512),
                        functools.partial(lambda bi, i, j: (bi, jnp.minimum((i + 1) * hb, nrb - 1), c0 + j), j=j))
           for j in range(3)]
    return pl.pallas_call(
        functools.partial(_hy_pre_kernel, nblk=nblk),
        grid=(b, nblk),
        in_specs=main + prev + nxt + [pl.BlockSpec((3, 1536), lambda bi, i: (0, 0)),
                                      pl.BlockSpec((1, 1536), lambda bi, i: (0, 0))],
        out_specs=[pl.BlockSpec((1, tm, 512), lambda bi, i: (bi, i, 0))] * 2,
        out_shape=[jax.ShapeDtypeStruct((b, t, 512), BF16)] * 2,
        compiler_params=_cparams(("parallel", "parallel")),
        name="hyena_pre",
    )(*([proj] * 9), conv_w.astype(F32), conv_b.astype(F32).reshape(1, 1536))


def _hy_filter_kernel(z_ref, w1, b1, w2, b2, w3, b3, fr, wo, dl, h_ref, l1_ref, *, tm, t_len):
    i = pl.program_id(0)
    f = fr[...]
    hh = jnp.sin(f * (_dot_x3(z_ref[...], w1[...]) + b1[...]))
    hh = jnp.sin(f * (_dot_x3(hh, w2[...]) + b2[...]))
    hh = jnp.sin(f * (_dot_x3(hh, w3[...]) + b3[...]))
    hh = _dot_x3(hh, wo[...])
    row = i * tm + lax.broadcasted_iota(jnp.int32, (tm, 1), 0)
    t = row.astype(F32) / (t_len - 1)
    decay = jnp.exp(-t * dl[...])
    h_f = hh[:, :HY_WIDTH] * decay
    h_b = jnp.where(row == 0, 0.0, hh[:, HY_WIDTH:] * decay)
    h_ref[0] = h_f.astype(BF16)
    h_ref[1] = h_b.astype(BF16)
    part = jnp.sum(jnp.abs(h_f) + jnp.abs(h_b), axis=0, keepdims=True)

    @pl.when(i == 0)
    def _():
        l1_ref[...] = part

    @pl.when(i > 0)
    def _():
        l1_ref[...] = l1_ref[...] + part


def _hy_filter(t_len, w1, b1, w2, b2, w3, b3, freq, wout):
    tm = min(t_len, 512)
    t = jnp.linspace(0.0, 1.0, t_len, dtype=F32)[:, None]
    bands = (HY_EMB - 1) // 2
    w_ang = 2.0 * math.pi * jnp.arange(t_len, dtype=F32)[:, None] / t_len
    f = jnp.linspace(1e-4, bands - 1, bands, dtype=F32)[None, :]
    z = jnp.concatenate([t, jnp.cos(f * w_ang), -jnp.sin(f * w_ang),
                         jnp.zeros((t_len, LANES - HY_EMB), F32)], axis=-1)
    w1p = jnp.concatenate([w1.astype(F32), jnp.zeros((LANES - HY_EMB, HY_FFN), F32)], axis=0)
    deltas = jnp.abs(jnp.linspace(math.log(HY_TARGET) / HY_SLOW_PCT, math.log(HY_TARGET) / HY_FAST_PCT,
                                  HY_WIDTH, dtype=F32))[None, :]
    full = lambda shape: pl.BlockSpec(shape, lambda i: tuple(0 for _ in shape))
    r = lambda v: v.astype(F32).reshape(1, -1)
    return pl.pallas_call(
        functools.partial(_hy_filter_kernel, tm=tm, t_len=t_len),
        grid=(t_len // tm,),
        in_specs=[pl.BlockSpec((tm, LANES), lambda i: (i, 0)),
                  full((LANES, HY_FFN)), full((1, HY_FFN)), full((HY_FFN, HY_FFN)), full((1, HY_FFN)),
                  full((HY_FFN, HY_FFN)), full((1, HY_FFN)), full((1, HY_FFN)), full((HY_FFN, 2 * HY_WIDTH)),
                  full((1, HY_WIDTH))],
        out_specs=[pl.BlockSpec((2, tm, HY_WIDTH), lambda i: (0, i, 0)),
                   pl.BlockSpec((1, HY_WIDTH), lambda i: (0, 0))],
        out_shape=[jax.ShapeDtypeStruct((2, t_len, HY_WIDTH), BF16), jax.ShapeDtypeStruct((1, HY_WIDTH), F32)],
        compiler_params=_cparams(("arbitrary",)),
        name="hyena_filter",
    )(z, w1p, r(b1), w2.astype(F32), r(b2), w3.astype(F32), r(b3), r(freq), wout.astype(F32), deltas)


def _dft_tables(t_len):
    n = 2 * t_len
    n2 = FFT_N2
    n1 = n // n2
    k1 = np.arange(n1)[:, None]
    j1 = np.arange(n1 // 2)[None, :]
    ang1 = 2.0 * np.pi * ((k1 * j1) % n1) / n1
    f1 = np.concatenate([np.cos(ang1), -np.sin(ang1)], axis=0)
    i1 = np.concatenate([np.cos(ang1).T, -np.sin(ang1).T], axis=1)
    a = np.arange(n2)
    ang2 = 2.0 * np.pi * ((a[:, None] * a[None, :]) % n2) / n2
    fr, fi = np.cos(ang2), -np.sin(ang2)
    fwd = np.block([[fr, -fi], [fi, fr]])
    inv = np.block([[fr, fi], [-fi, fr]])
    angt = 2.0 * np.pi * ((np.arange(n1)[:, None] * a[None, :]) % n) / n
    twc = np.cos(angt)[:, :, None]
    tws = np.sin(angt)[:, :, None]
    return (jnp.asarray(f1, F32).astype(BF16), jnp.asarray(i1, F32).astype(BF16), jnp.asarray(fwd, F32).astype(BF16),
            jnp.asarray(inv, F32).astype(BF16),
            jnp.broadcast_to(jnp.asarray(twc, F32), (n1, n2, LANES)),
            jnp.broadcast_to(jnp.asarray(tws, F32), (n1, n2, LANES)))


def _hy_s1_kernel(z_ref, f_ref, xr_ref, xi_ref):
    n1 = xr_ref.shape[1]
    y = _dot(f_ref[...], z_ref[0])
    xr_ref[0] = y[:n1].astype(BF16)
    xi_ref[0] = y[n1:].astype(BF16)


def _hy_s1(zv, f1):
    b, half, cols = zv.shape
    n1 = 2 * half
    tn = min(cols, 4096)
    return pl.pallas_call(
        _hy_s1_kernel,
        grid=(b, cols // tn),
        in_specs=[pl.BlockSpec((1, half, tn), lambda bi, j: (bi, 0, j)),
                  pl.BlockSpec((2 * n1, half), lambda bi, j: (0, 0))],
        out_specs=[pl.BlockSpec((1, n1, tn), lambda bi, j: (bi, 0, j))] * 2,
        out_shape=[jax.ShapeDtypeStruct((b, n1, cols), BF16)] * 2,
        compiler_params=_cparams(("parallel", "parallel")),
        name="hyena_dft_outer",
    )(zv, f1)


def _twiddle_fwd(xr, xi, c, s):
    return xr * c + xi * s, xi * c - xr * s


def _hy_filt_spec_kernel(xr_ref, xi_ref, twc_ref, tws_ref, fwd_ref, l1_ref, kr_ref, ki_ref, *, n):
    reps = xr_ref.shape[-1] // LANES
    c = jnp.concatenate([twc_ref[0]] * reps, axis=1)
    s = jnp.concatenate([tws_ref[0]] * reps, axis=1)
    n2 = FFT_N2
    parts = []
    for j in range(2):
        ar, ai = _twiddle_fwd(xr_ref[j, 0].astype(F32), xi_ref[j, 0].astype(F32), c, s)
        y = _dot(fwd_ref[...], jnp.concatenate([ar, ai], axis=0).astype(BF16))
        parts.append((y[:n2], y[n2:]))
    scale = 1.0 / (l1_ref[...] * n)
    kr_ref[0] = (parts[0][0] + parts[1][0]) * scale
    ki_ref[0] = (parts[0][1] - parts[1][1]) * scale


def _hy_filt_spec(xr, xi, twc, tws, fwd, l1, n):
    _, n1, n2, w = xr.shape
    return pl.pallas_call(
        functools.partial(_hy_filt_spec_kernel, n=n),
        grid=(n1,),
        in_specs=[pl.BlockSpec((2, 1, n2, w), lambda k: (0, k, 0, 0)),
                  pl.BlockSpec((2, 1, n2, w), lambda k: (0, k, 0, 0)),
                  pl.BlockSpec((1, n2, LANES), lambda k: (k, 0, 0)),
                  pl.BlockSpec((1, n2, LANES), lambda k: (k, 0, 0)),
                  pl.BlockSpec((2 * n2, 2 * n2), lambda k: (0, 0)),
                  pl.BlockSpec((1, w), lambda k: (0, 0))],
        out_specs=[pl.BlockSpec((1, n2, w), lambda k: (k, 0, 0))] * 2,
        out_shape=[jax.ShapeDtypeStruct((n1, n2, w), F32)] * 2,
        compiler_params=_cparams(("parallel",)),
        name="hyena_filter_spectrum",
    )(xr, xi, twc, tws, fwd, l1)


def _hy_mid_kernel(xr_ref, xi_ref, twc_ref, tws_ref, fwd_ref, inv_ref, kr_ref, ki_ref, tr_ref, ti_ref):
    reps = xr_ref.shape[-1] // LANES
    c = jnp.concatenate([twc_ref[0]] * reps, axis=1)
    s = jnp.concatenate([tws_ref[0]] * reps, axis=1)
    n2 = FFT_N2
    ar, ai = _twiddle_fwd(xr_ref[0, 0].astype(F32), xi_ref[0, 0].astype(F32), c, s)
    y = _dot(fwd_ref[...], jnp.concatenate([ar, ai], axis=0).astype(BF16))
    yr, yi = y[:n2], y[n2:]
    kr = kr_ref[0]
    ki = ki_ref[0]
    zr = yr * kr - yi * ki
    zi = yr * ki + yi * kr
    u = _dot(inv_ref[...], jnp.concatenate([zr, zi], axis=0).astype(BF16))
    ur, ui = u[:n2], u[n2:]
    tr_ref[0, 0] = (ur * c - ui * s).astype(BF16)
    ti_ref[0, 0] = (ur * s + ui * c).astype(BF16)


def _hy_mid(xr, xi, twc, tws, fwd, inv, kr, ki):
    b, n1, n2, w = xr.shape
    xspec = pl.BlockSpec((1, 1, n2, w), lambda bi, k: (bi, k, 0, 0))
    return pl.pallas_call(
        _hy_mid_kernel,
        grid=(b, n1),
        in_specs=[xspec, xspec,
                  pl.BlockSpec((1, n2, LANES), lambda bi, k: (k, 0, 0)),
                  pl.BlockSpec((1, n2, LANES), lambda bi, k: (k, 0, 0)),
                  pl.BlockSpec((2 * n2, 2 * n2), lambda bi, k: (0, 0)),
                  pl.BlockSpec((2 * n2, 2 * n2), lambda bi, k: (0, 0)),
                  pl.BlockSpec((1, n2, w), lambda bi, k: (k, 0, 0)),
                  pl.BlockSpec((1, n2, w), lambda bi, k: (k, 0, 0))],
        out_specs=[xspec, xspec],
        out_shape=[jax.ShapeDtypeStruct((b, n1, n2, w), BF16)] * 2,
        compiler_params=_cparams(("parallel", "parallel")),
        name="hyena_dft_inner",
    )(xr, xi, twc, tws, fwd, inv, kr, ki)


def _hy_post_kernel(tr_ref, ti_ref, i1_ref, x0_ref, z_ref, fb_ref, o_ref):
    t = jnp.concatenate([tr_ref[0], ti_ref[0]], axis=0)
    y = _dot(i1_ref[...], t)
    z = z_ref[0].astype(F32)
    o_ref[0] = (x0_ref[0].astype(F32) * (y + z * fb_ref[...])).astype(BF16)


def _hy_post(tr, ti, i1, x0v, zv, fb_tiled):
    b, n1, cols = tr.shape
    half = n1 // 2
    tn = fb_tiled.shape[1]
    hspec = pl.BlockSpec((1, half, tn), lambda bi, j: (bi, 0, j))
    tspec = pl.BlockSpec((1, n1, tn), lambda bi, j: (bi, 0, j))
    return pl.pallas_call(
        _hy_post_kernel,
        grid=(b, cols // tn),
        in_specs=[tspec, tspec, pl.BlockSpec((half, 2 * n1), lambda bi, j: (0, 0)), hspec, hspec,
                  pl.BlockSpec((1, tn), lambda bi, j: (0, 0))],
        out_specs=hspec,
        out_shape=jax.ShapeDtypeStruct((b, half, cols), BF16),
        compiler_params=_cparams(("parallel", "parallel")),
        name="hyena_dft_outer_inv",
    )(tr, ti, i1, x0v, zv, fb_tiled)


def _hyena_long(proj, conv_w, conv_b, filt, filt_bias):
    b, t, _ = proj.shape
    w = HY_WIDTH
    n = 2 * t
    n2 = FFT_N2
    n1 = n // n2
    f1, i1, fwd, inv, twc, tws = _dft_tables(t)
    hfb, l1 = _hy_filter(t, *filt)
    hr, hi = _hy_s1(hfb.reshape(2, n1 // 2, n2 * w), f1)
    kr, ki = _hy_filt_spec(hr.reshape(2, n1, n2, w), hi.reshape(2, n1, n2, w), twc, tws, fwd, l1, n)
    x0, z = _hy_pre(proj, conv_w, conv_b)
    zv = z.reshape(b, n1 // 2, n2 * w)
    xr, xi = _hy_s1(zv, f1)
    tr, ti = _hy_mid(xr.reshape(b, n1, n2, w), xi.reshape(b, n1, n2, w), twc, tws, fwd, inv, kr, ki)
    tn = min(n2 * w, 4096)
    fb_tiled = jnp.tile(filt_bias.astype(F32).reshape(1, w), (1, tn // w))
    d = _hy_post(tr.reshape(b, n1, n2 * w), ti.reshape(b, n1, n2 * w), i1, x0.reshape(b, n1 // 2, n2 * w), zv, fb_tiled)
    return d.reshape(b, t, w)


def _hy_ctx_kernel(x0_ref, z_ref, h_ref, l1_ref, cm_ref, sm_ref, cmt_ref, smt_ref, fb_ref, o_ref, *, n):
    cm = cm_ref[...]
    sm = sm_ref[...]

    def spectrum(v):
        return _dot(cm, v), -_dot(sm, v)

    z = z_ref[0]
    zr, zi = spectrum(z)
    ar, ai = spectrum(h_ref[0])
    br, bi = spectrum(h_ref[1])
    scale = 1.0 / (l1_ref[...] * n)
    kr = (ar + br) * scale
    ki = (ai - bi) * scale
    yr = (zr * kr - zi * ki).astype(BF16)
    yi = (zr * ki + zi * kr).astype(BF16)
    y = _dot(cmt_ref[...], yr) - _dot(smt_ref[...], yi)
    o_ref[0] = (x0_ref[0].astype(F32) * (y + z.astype(F32) * fb_ref[...])).astype(BF16)


def _hyena_short(proj, conv_w, conv_b, filt, filt_bias):
    b, t, _ = proj.shape
    w = HY_WIDTH
    n = 2 * t
    hfb, l1 = _hy_filter(t, *filt)
    x0, z = _hy_pre(proj, conv_w, conv_b)
    k = np.arange(n)[:, None]
    j = np.arange(t)[None, :]
    ang = 2.0 * np.pi * ((k * j) % n) / n
    cm, sm = np.cos(ang), np.sin(ang)
    tabs = [jnp.asarray(a, F32).astype(BF16) for a in (cm, sm, cm.T, sm.T)]
    full2 = lambda shape: pl.BlockSpec(shape, lambda bi: (0, 0))
    bspec = pl.BlockSpec((1, t, w), lambda bi: (bi, 0, 0))
    return pl.pallas_call(
        functools.partial(_hy_ctx_kernel, n=n),
        grid=(b,),
        in_specs=[bspec, bspec, pl.BlockSpec((2, t, w), lambda bi: (0, 0, 0)), full2((1, w)),
                  full2((n, t)), full2((n, t)), full2((t, n)), full2((t, n)), full2((1, w))],
        out_specs=bspec,
        out_shape=jax.ShapeDtypeStruct((b, t, w), BF16),
        compiler_params=_cparams(("parallel",)),
        name="hyena_ctx",
    )(x0, z, hfb, l1, *tabs, filt_bias.astype(F32).reshape(1, w))


def _layer(xc, xl, mod, cos128, sin128, norm_w, w_in, rpb, sink, w_alpha_up, b_alpha, gla_norm_w, conv_w, conv_b,
           filt, filt_bias, w_branch, w_out, final_w, with_ctx_out):
    b, length, d = xl.shape
    mod_l = mod[:b].reshape(b, 1, 3 * d)
    mod_c = jnp.broadcast_to(mod[b].reshape(1, 1, 3 * d), (b, 1, 3 * d))
    sh_l, sc_l, g_l = mod_l[..., :d], mod_l[..., d:2 * d], mod_l[..., 2 * d:]
    sh_c, sc_c, g_c = mod_c[..., :d], mod_c[..., d:2 * d], mod_c[..., 2 * d:]

    w_perm = _permute(w_in.astype(BF16), _IN_PERM, axis=1) * jnp.asarray(_HALF_COLS, BF16)[None, :]
    wbf = w_branch.astype(BF16) * 0.5
    wb = jnp.stack([wbf[0], _permute(wbf[1], _SWA_OUT_PERM, axis=0), wbf[2], wbf[3]])
    wo = w_out.astype(BF16)

    proj_l = _inproj(xl, norm_w, sc_l, sh_l, w_perm)
    proj_c = _inproj(xc, norm_w, sc_c, sh_c, w_perm)

    y_a = _natten(proj_l, proj_c, _natten_bias_table(rpb))
    sink128 = jnp.broadcast_to(sink.astype(F32)[:, None], (SW_HEADS, LANES))
    qr, kr = _rope(proj_l, cos128, sin128)
    y_b = _swa(qr, kr, proj_l, proj_c, proj_c, sink128)
    whi, wlo, bpad = _gla_decay_weights(w_alpha_up, b_alpha)
    s_zero = jnp.zeros((b, 2, 2, 128, 128), F32)
    of_c, s_cf = _gla_pass(proj_c, whi, wlo, bpad, s_zero, rev=False)
    y_cc, s_cb = _gla_pass(proj_c, whi, wlo, bpad, s_zero, rev=True, o_fwd=of_c, gnw=gla_norm_w.astype(F32))
    of_l, _ = _gla_pass(proj_l, whi, wlo, bpad, s_cf, rev=False)
    y_c, _ = _gla_pass(proj_l, whi, wlo, bpad, s_cb, rev=True, o_fwd=of_l, gnw=gla_norm_w.astype(F32))
    y_d = _hyena_long(proj_l, conv_w, conv_b, filt, filt_bias)

    xl_new = _merge([y_a, y_b, y_c, y_d], proj_l, wb, wo, g_l, xl, final_w, final=not with_ctx_out)
    if with_ctx_out:
        ya_c = _ctx_attn(proj_c, sink128, swa=False)
        yb_c = _ctx_attn(proj_c, sink128, swa=True)
        yd_c = _hyena_short(proj_c, conv_w, conv_b, filt, filt_bias)
        xc = _merge([ya_c, yb_c, y_cc, yd_c], proj_c, wb, wo, g_c, xc, final_w, final=False)
    return xc, xl_new


def kernel(x, c, ctx, c_ctx, norm_w, w_mod, b_mod, w_in, rpb, sink, w_alpha_up, b_alpha, gla_norm_w, conv_w, conv_b,
           filt_w1, filt_b1, filt_w2, filt_b2, filt_w3, filt_b3, filt_freq, filt_wout, filt_bias, w_branch, w_out,
           final_norm_w):
    b, length, d = x.shape
    depth = norm_w.shape[0]
    cvec = jnp.zeros((8, d), F32).at[:b].set(c.astype(F32)).at[b].set(c_ctx.astype(F32))
    mod = _modulation(cvec, w_mod.astype(F32), b_mod.astype(F32))
    cos128, sin128 = _rope_tables(length)
    xc, xl = ctx, x
    for i in range(depth):
        filt = (filt_w1[i], filt_b1[i], filt_w2[i], filt_b2[i], filt_w3[i], filt_b3[i], filt_freq[i], filt_wout[i])
        xc, xl = _layer(xc, xl, mod[i], cos128, sin128, norm_w[i], w_in[i], rpb[i], sink[i], w_alpha_up[i],
                        b_alpha[i], gla_norm_w[i], conv_w[i], conv_b[i], filt, filt_bias[i], w_branch[i], w_out[i],
                        final_norm_w, with_ctx_out=(i < depth - 1))
    return xl
```

```python
import functools
import math

import numpy as np
import jax
import jax.numpy as jnp
from jax import lax
from jax.experimental import pallas as pl
from jax.experimental.pallas import tpu as pltpu

F32 = jnp.float32
BF16 = jnp.bfloat16

D_MODEL = 1024
GRID_W = 64
HEAD_DIM = 64
BRANCH_WIDTH = D_MODEL // 2
N_BRANCH = 4
NA_HEADS = 8
NA_KR = 8
NA_KC = 16
SW_HEADS = 8
SW_KV_HEADS = 2
SW_WINDOW = 128
SW_BLOCK = 128
GLA_HEADS = 4
GLA_DK = BRANCH_WIDTH // 2
GLA_DV = BRANCH_WIDTH
GLA_RANK = 16
GLA_TAU = 16.0
GLA_CHUNK = 64
HY_WIDTH = BRANCH_WIDTH
HY_EMB = 33
HY_FFN = 64
HY_TARGET = 1e-2
HY_FAST_PCT = 0.3
HY_SLOW_PCT = 1.5
ROPE_BASE = 10000.0
EPS = 1e-6
NEG_INF = -1e30
LOG2E = math.log2(math.e)
LANES = 128
FFT_N2 = 128
VMEM_LIMIT = 56 * 1024 * 1024

_IN_WIDTHS = (512, 512, 512, 512, 512, 128, 128, 512, 256, 256, 512, 16, 16, 512, 1536, 512, 4096)
_IN_OFF = np.concatenate([[0], np.cumsum(_IN_WIDTHS)])
(_O_AQ, _O_AK, _O_AV, _O_AG, _O_BQ, _O_BK, _O_BV, _O_BG, _O_CQ, _O_CK, _O_CV, _O_LRF, _O_LRB, _O_CG,
 _O_DU, _O_DG, _O_GM) = [int(v) for v in _IN_OFF[:-1]]
IN_TOTAL = int(_IN_OFF[-1])

P_AQ, P_AK, P_AV, P_AG = 0, 512, 1024, 1536
P_BQ, P_BG = 2048, 2560
P_CV, P_CG = 3072, 3584
P_DU, P_DG = 4096, 5632
P_GM = 6144
P_CQ, P_CK = 10240, 10496
P_BK, P_BV = 10752, 10880
P_LR = 11008
NP_COLS = 11264


def _swa_q_perm():
    idx = np.zeros(512, np.int64)
    for t in range(4):
        a, b = t, t + 4
        base = 128 * t
        idx[base + 0:base + 32] = 64 * a + np.arange(32)
        idx[base + 32:base + 64] = 64 * b + np.arange(32)
        idx[base + 64:base + 96] = 64 * a + 32 + np.arange(32)
        idx[base + 96:base + 128] = 64 * b + 32 + np.arange(32)
    return idx


def _swa_k_perm():
    idx = np.zeros(128, np.int64)
    idx[0:32] = np.arange(32)
    idx[32:64] = 64 + np.arange(32)
    idx[64:96] = 32 + np.arange(32)
    idx[96:128] = 96 + np.arange(32)
    return idx


def _swa_out_perm():
    idx = np.zeros(512, np.int64)
    for t in range(4):
        idx[128 * t:128 * t + 64] = 64 * t + np.arange(64)
        idx[128 * t + 64:128 * t + 128] = 64 * (t + 4) + np.arange(64)
    return idx


def _build_in_perm():
    perm = np.full(NP_COLS, -1, np.int64)

    def put(p, o, w):
        perm[p:p + w] = o + np.arange(w)

    put(P_AQ, _O_AQ, 512); put(P_AK, _O_AK, 512); put(P_AV, _O_AV, 512); put(P_AG, _O_AG, 512)
    perm[P_BQ:P_BQ + 512] = _O_BQ + _swa_q_perm()
    perm[P_BG:P_BG + 512] = _O_BG + _swa_out_perm()
    put(P_CV, _O_CV, 512); put(P_CG, _O_CG, 512)
    put(P_DU, _O_DU, 1536); put(P_DG, _O_DG, 512)
    put(P_GM, _O_GM, 4096)
    put(P_CQ, _O_CQ, 256); put(P_CK, _O_CK, 256)
    perm[P_BK:P_BK + 128] = _O_BK + _swa_k_perm()
    put(P_BV, _O_BV, 128)
    put(P_LR, _O_LRF, 16); put(P_LR + 16, _O_LRB, 16)
    return perm


def _build_half_cols():
    s = np.ones(NP_COLS, np.float32)
    for p, w in ((P_AG, 512), (P_BG, 512), (P_CG, 512), (P_DG, 512), (P_GM, 4096)):
        s[p:p + w] = 0.5
    return s


_HALF_COLS = _build_half_cols()
_IN_PERM = _build_in_perm()
_SWA_OUT_PERM = _swa_out_perm()


def _permute(w, perm, axis):
    pieces = []
    i = 0
    n = len(perm)
    while i < n:
        j = i + 1
        if perm[i] < 0:
            while j < n and perm[j] < 0:
                j += 1
            shape = list(w.shape)
            shape[axis] = j - i
            pieces.append(jnp.zeros(shape, w.dtype))
        else:
            while j < n and perm[j] == perm[j - 1] + 1:
                j += 1
            pieces.append(lax.slice_in_dim(w, int(perm[i]), int(perm[j - 1]) + 1, axis=axis))
        i = j
    return jnp.concatenate(pieces, axis=axis)


def _cparams(sem):
    return pltpu.CompilerParams(dimension_semantics=sem, vmem_limit_bytes=VMEM_LIMIT)


def _sigmoid(x):
    return 1.0 / (1.0 + jnp.exp(-x))


def _sigmoid_tanh(x):
    return 0.5 * jnp.tanh(0.5 * x) + 0.5


def _split3(a):
    hi = a.astype(BF16)
    r1 = a - hi.astype(F32)
    mid = r1.astype(BF16)
    lo = (r1 - mid.astype(F32)).astype(BF16)
    return hi, mid, lo


def _dot(a, b):
    return jnp.dot(a, b, preferred_element_type=F32)


def _dot_nt(a, b):
    return lax.dot_general(a, b, (((1,), (1,)), ((), ())), preferred_element_type=F32)


def _dot_tn(a, b):
    return lax.dot_general(a, b, (((0,), (0,)), ((), ())), preferred_element_type=F32)


def _dot_x3(a, b):
    ah, am, _ = _split3(a)
    bh, bm, _ = _split3(b)
    return _dot(ah, bh) + (_dot(ah, bm) + _dot(am, bh))


def _mod_kernel(c_ref, w_ref, b_ref, o_ref):
    c = c_ref[...]
    s = c * _sigmoid(c)
    o_ref[0] = _dot_x3(s, w_ref[0]) + b_ref[0]


def _modulation(cvec, w_mod, b_mod):
    depth, d, n = w_mod.shape
    tn = 512
    return pl.pallas_call(
        _mod_kernel,
        grid=(depth, n // tn),
        in_specs=[pl.BlockSpec((8, d), lambda l, j: (0, 0)),
                  pl.BlockSpec((1, d, tn), lambda l, j: (l, 0, j)),
                  pl.BlockSpec((1, 1, tn), lambda l, j: (l, 0, j))],
        out_specs=pl.BlockSpec((1, 8, tn), lambda l, j: (l, 0, j)),
        out_shape=jax.ShapeDtypeStruct((depth, 8, n), F32),
        compiler_params=_cparams(("parallel", "parallel")),
        name="modulation",
    )(cvec, w_mod, b_mod.reshape(depth, 1, n))


def _inproj_kernel(x_ref, nw_ref, sc_ref, sh_ref, w_ref, o_ref, h_ref):
    @pl.when(pl.program_id(2) == 0)
    def _():
        x = x_ref[0]
        ms = jnp.mean(x * x, axis=-1, keepdims=True)
        y = x * lax.rsqrt(ms + EPS) * nw_ref[...]
        h_ref[...] = (y * (1.0 + sc_ref[0]) + sh_ref[0]).astype(BF16)

    o_ref[0] = _dot(h_ref[...], w_ref[...]).astype(BF16)


def _inproj(x, norm_w, scale, shift, w_perm):
    b, t, d = x.shape
    n = w_perm.shape[1]
    tm = min(t, 2048)
    tn = 1024
    return pl.pallas_call(
        _inproj_kernel,
        grid=(b, t // tm, n // tn),
        in_specs=[pl.BlockSpec((1, tm, d), lambda bi, i, j: (bi, i, 0)),
                  pl.BlockSpec((1, d), lambda bi, i, j: (0, 0)),
                  pl.BlockSpec((1, 1, d), lambda bi, i, j: (bi, 0, 0)),
                  pl.BlockSpec((1, 1, d), lambda bi, i, j: (bi, 0, 0)),
                  pl.BlockSpec((d, tn), lambda bi, i, j: (0, j))],
        out_specs=pl.BlockSpec((1, tm, tn), lambda bi, i, j: (bi, i, j)),
        out_shape=jax.ShapeDtypeStruct((b, t, n), BF16),
        scratch_shapes=[pltpu.VMEM((tm, d), BF16)],
        compiler_params=_cparams(("parallel", "parallel", "arbitrary")),
        name="inproj",
    )(x, norm_w.reshape(1, d), scale, shift, w_perm)


def _merge_kernel(ya, yb, yc, yd, ga, gb, gc, gd, m0, m1, m2, m3, wb_ref, wo_ref, g_ref, x_ref, fw_ref,
                  o_ref, *, final):
    acc = None
    for i, (y, g, gm) in enumerate(((ya, ga, m0), (yb, gb, m1), (yc, gc, m2), (yd, gd, m3))):
        hg = g[0].astype(F32)
        yg = y[0].astype(F32) * hg
        ys = (yg + yg * jnp.tanh(hg)).astype(BF16)
        hp = _dot(ys, wb_ref[i])
        t = hp + hp * jnp.tanh(gm[0].astype(F32))
        acc = t if acc is None else acc + t
    out = x_ref[0] + g_ref[0] * _dot(acc.astype(BF16), wo_ref[...])
    if final:
        ms = jnp.mean(out * out, axis=-1, keepdims=True)
        out = out * lax.rsqrt(ms + EPS) * fw_ref[...]
    o_ref[0] = out


def _merge(ys, proj, w_branch, w_out, gate, x, final_w, final):
    b, t, d = x.shape
    bw = BRANCH_WIDTH
    tm = min(t, 512)
    yspec = pl.BlockSpec((1, tm, bw), lambda bi, i: (bi, i, 0))

    def pspec(col, width):
        blk = col // width
        return pl.BlockSpec((1, tm, width), lambda bi, i: (bi, i, blk))

    in_specs = ([yspec] * 4
                + [pspec(P_AG, bw), pspec(P_BG, bw), pspec(P_CG, bw), pspec(P_DG, bw)]
                + [pspec(P_GM + k * d, d) for k in range(N_BRANCH)]
                + [pl.BlockSpec((N_BRANCH, bw, d), lambda bi, i: (0, 0, 0)),
                   pl.BlockSpec((d, d), lambda bi, i: (0, 0)),
                   pl.BlockSpec((1, 1, d), lambda bi, i: (bi, 0, 0)),
                   pl.BlockSpec((1, tm, d), lambda bi, i: (bi, i, 0)),
                   pl.BlockSpec((1, d), lambda bi, i: (0, 0))])
    return pl.pallas_call(
        functools.partial(_merge_kernel, final=final),
        grid=(b, t // tm),
        in_specs=in_specs,
        out_specs=pl.BlockSpec((1, tm, d), lambda bi, i: (bi, i, 0)),
        out_shape=jax.ShapeDtypeStruct((b, t, d), F32),
        compiler_params=_cparams(("parallel", "parallel")),
        name="merge",
    )(*ys, *([proj] * 8), w_branch, w_out, gate, x, final_w.reshape(1, d))


def _rope_kernel(q_ref, k_ref, cos_ref, sin_ref, qo_ref, ko_ref):
    cos = cos_ref[...]
    sin = sin_ref[...]

    def rot(x):
        return x * cos + pltpu.roll(x, 64, 1) * sin

    for t in range(4):
        q = q_ref[0, :, 128 * t:128 * (t + 1)].astype(F32)
        qo_ref[0, :, 128 * t:128 * (t + 1)] = (rot(q) * (HEAD_DIM ** -0.5 * LOG2E)).astype(BF16)
    ko_ref[0] = rot(k_ref[0].astype(F32)).astype(BF16)


def _rope(proj, cos128, sin128):
    b, t, _ = proj.shape
    tm = min(t, 1024)
    return pl.pallas_call(
        _rope_kernel,
        grid=(b, t // tm),
        in_specs=[pl.BlockSpec((1, tm, 512), lambda bi, i: (bi, i, P_BQ // 512)),
                  pl.BlockSpec((1, tm, 128), lambda bi, i: (bi, i, P_BK // 128)),
                  pl.BlockSpec((tm, 128), lambda bi, i: (i, 0)),
                  pl.BlockSpec((tm, 128), lambda bi, i: (i, 0))],
        out_specs=[pl.BlockSpec((1, tm, 512), lambda bi, i: (bi, i, 0)),
                   pl.BlockSpec((1, tm, 128), lambda bi, i: (bi, i, 0))],
        out_shape=[jax.ShapeDtypeStruct((b, t, 512), BF16), jax.ShapeDtypeStruct((b, t, 128), BF16)],
        compiler_params=_cparams(("parallel", "parallel")),
        name="rope",
    )(proj, proj, cos128, sin128)


def _rope_tables(length):
    t = jnp.arange(length, dtype=jnp.int32)
    row = (t // GRID_W).astype(F32)
    col = (t % GRID_W).astype(F32)
    n_freq = HEAD_DIM // 4
    inv = ROPE_BASE ** (-jnp.arange(n_freq, dtype=F32) / n_freq)
    ang = jnp.concatenate([row[:, None] * inv, col[:, None] * inv], axis=-1)
    cos, sin = jnp.cos(ang), jnp.sin(ang)
    return jnp.tile(cos, (1, 4)), jnp.concatenate([-sin, -sin, sin, sin], axis=-1)


def _swa_kernel(q_ref, k_ref, v_ref, kc_ref, vc_ref, sink_ref, mask_ref, o_ref, *, length):
    i = pl.program_id(1)
    blk = SW_BLOCK
    kw = 3 * blk
    nb = length // blk
    start = pl.multiple_of(jnp.clip((i - 1) * blk, 0, length - kw), blk)
    kwin = k_ref[0, pl.ds(start, kw), :]
    vwin = v_ref[0, pl.ds(start, kw), :]
    kc = kc_ref[0]
    vc = vc_ref[0]
    variant = jnp.where(i == 0, 0, jnp.where(i == nb - 1, 2, 1))
    lane = lax.broadcasted_iota(jnp.int32, (blk, LANES), 1)
    a_lanes = (lane // 32) % 2 == 0
    lo_lanes = lane < 64
    top = lax.broadcasted_iota(jnp.int32, (2 * blk, 1), 0) < blk
    for t in range(4):
        qt = q_ref[0, :, 128 * t:128 * (t + 1)]
        zero = jnp.zeros_like(qt)
        qq = jnp.concatenate([jnp.where(a_lanes, qt, zero), jnp.where(a_lanes, zero, qt)], axis=0)
        sw = _dot_nt(qq, kwin) + mask_ref[variant]
        sc = _dot_nt(qq, kc)
        sk = jnp.where(top, sink_ref[t:t + 1, 0:1], sink_ref[t + 4:t + 5, 0:1])
        m = jnp.maximum(jnp.maximum(jnp.max(sw, axis=-1, keepdims=True), jnp.max(sc, axis=-1, keepdims=True)), sk)
        pw = jnp.exp2(sw - m)
        pc = jnp.exp2(sc - m)
        den = jnp.sum(pw, axis=-1, keepdims=True) + jnp.sum(pc, axis=-1, keepdims=True) + jnp.exp2(sk - m)
        o = (_dot(pw.astype(BF16), vwin) + _dot(pc.astype(BF16), vc)) / den
        o_ref[0, :, 128 * t:128 * (t + 1)] = jnp.where(lo_lanes, o[:blk], o[blk:]).astype(BF16)


def _swa(qr, kr, proj_l, kc_rot, proj_c, sink128):
    b, length, _ = qr.shape
    lc = proj_c.shape[1]
    assert length >= 3 * SW_BLOCK
    r = np.arange(2 * SW_BLOCK)[:, None] % SW_BLOCK
    c = np.arange(3 * SW_BLOCK)[None, :]
    mask_tab = jnp.asarray(np.stack([np.where(np.abs(v * SW_BLOCK + r - c) <= SW_WINDOW, 0.0, NEG_INF)
                                     for v in range(3)]), F32)
    return pl.pallas_call(
        functools.partial(_swa_kernel, length=length),
        grid=(b, length // SW_BLOCK),
        in_specs=[pl.BlockSpec((1, SW_BLOCK, 512), lambda bi, i: (bi, i, 0)),
                  pl.BlockSpec((1, length, 128), lambda bi, i: (bi, 0, 0)),
                  pl.BlockSpec((1, length, 128), lambda bi, i: (bi, 0, P_BV // 128)),
                  pl.BlockSpec((1, lc, 128), lambda bi, i: (bi, 0, P_BK // 128)),
                  pl.BlockSpec((1, lc, 128), lambda bi, i: (bi, 0, P_BV // 128)),
                  pl.BlockSpec((8, 128), lambda bi, i: (0, 0)),
                  pl.BlockSpec((3, 2 * SW_BLOCK, 3 * SW_BLOCK), lambda bi, i: (0, 0, 0))],
        out_specs=pl.BlockSpec((1, SW_BLOCK, 512), lambda bi, i: (bi, i, 0)),
        out_shape=jax.ShapeDtypeStruct((b, length, 512), BF16),
        compiler_params=_cparams(("parallel", "arbitrary")),
        name="swa",
    )(qr, kr, proj_l, kc_rot, proj_c, sink128, mask_tab)


def _ctx_attn_kernel(q_ref, k_ref, v_ref, sink_ref, o_ref, *, swa):
    t_len = q_ref.shape[1]
    lane = lax.broadcasted_iota(jnp.int32, (t_len, LANES), 1)
    lo_lanes = lane < 64
    a_lanes = ((lane // 32) % 2 == 0) if swa else lo_lanes
    top = lax.broadcasted_iota(jnp.int32, (2 * t_len, 1), 0) < t_len
    for t in range(4):
        qt = (q_ref[0, :, 128 * t:128 * (t + 1)].astype(F32) * HEAD_DIM ** -0.5).astype(BF16)
        zero = jnp.zeros_like(qt)
        qq = jnp.concatenate([jnp.where(a_lanes, qt, zero), jnp.where(a_lanes, zero, qt)], axis=0)
        if swa:
            kt, vt = k_ref[0], v_ref[0]
        else:
            kt, vt = k_ref[0, :, 128 * t:128 * (t + 1)], v_ref[0, :, 128 * t:128 * (t + 1)]
        s = _dot_nt(qq, kt)
        m = jnp.max(s, axis=-1, keepdims=True)
        if swa:
            sk = jnp.where(top, sink_ref[t:t + 1, 0:1], sink_ref[t + 4:t + 5, 0:1])
            m = jnp.maximum(m, sk)
        p = jnp.exp(s - m)
        den = jnp.sum(p, axis=-1, keepdims=True)
        if swa:
            den = den + jnp.exp(sk - m)
        o = _dot(p.astype(BF16), vt) / den
        o_ref[0, :, 128 * t:128 * (t + 1)] = jnp.where(lo_lanes, o[:t_len], o[t_len:]).astype(BF16)


def _ctx_attn(proj_c, sink128, swa):
    b, lc, _ = proj_c.shape
    if swa:
        qs = pl.BlockSpec((1, lc, 512), lambda bi: (bi, 0, P_BQ // 512))
        ks = pl.BlockSpec((1, lc, 128), lambda bi: (bi, 0, P_BK // 128))
        vs = pl.BlockSpec((1, lc, 128), lambda bi: (bi, 0, P_BV // 128))
    else:
        qs = pl.BlockSpec((1, lc, 512), lambda bi: (bi, 0, P_AQ // 512))
        ks = pl.BlockSpec((1, lc, 512), lambda bi: (bi, 0, P_AK // 512))
        vs = pl.BlockSpec((1, lc, 512), lambda bi: (bi, 0, P_AV // 512))
    return pl.pallas_call(
        functools.partial(_ctx_attn_kernel, swa=swa),
        grid=(b,),
        in_specs=[qs, ks, vs, pl.BlockSpec((8, 128), lambda bi: (0, 0))],
        out_specs=pl.BlockSpec((1, lc, 512), lambda bi: (bi, 0, 0)),
        out_shape=jax.ShapeDtypeStruct((b, lc, 512), BF16),
        compiler_params=_cparams(("parallel",)),
        name="ctx_attn_swa" if swa else "ctx_attn_na",
    )(proj_c, proj_c, proj_c, sink128)


def _natten_kernel(q_ref, k_ref, v_ref, kc_ref, vc_ref, bias_ref, o_ref, *, rows, rows_per_step):
    blk = pl.program_id(2)
    kc = kc_ref[0]
    vc = vc_ref[0]
    w = GRID_W
    nkeys = NA_KR * w
    lo_all = lax.broadcasted_iota(jnp.int32, (rows_per_step * w, LANES), 1) < 64
    lo_row = lax.broadcasted_iota(jnp.int32, (w, LANES), 1) < 64
    q_all = (q_ref[0].astype(F32) * (HEAD_DIM ** -0.5 * LOG2E)).astype(BF16)
    zero = jnp.zeros_like(q_all)
    q_lo = jnp.where(lo_all, q_all, zero)
    q_hi = jnp.where(lo_all, zero, q_all)
    sc_lo = _dot_nt(q_lo, kc)
    sc_hi = _dot_nt(q_hi, kc)
    o_rows, pc_lo, pc_hi, rd_lo, rd_hi = [], [], [], [], []

    def scores(rr):
        r = blk * rows_per_step + rr
        rs = jnp.clip(r - NA_KR // 2, 0, rows - NA_KR)
        off = rs - r + (NA_KR - 1)
        k0 = pl.multiple_of(rs * w, w)
        sl = slice(rr * w, (rr + 1) * w)
        qq = jnp.concatenate([q_lo[sl], q_hi[sl]], axis=0)
        s = _dot_nt(qq, k_ref[0, pl.ds(k0, nkeys), :]) + bias_ref[off, 0]
        sc = jnp.concatenate([sc_lo[sl], sc_hi[sl]], axis=0)
        return s, sc, k0

    ahead = 2
    queue = [scores(rr) for rr in range(ahead)]
    for rr in range(rows_per_step):
        s, sc, k0 = queue.pop(0)
        if rr + ahead < rows_per_step:
            queue.append(scores(rr + ahead))
        vrows = v_ref[0, pl.ds(k0, nkeys), :]
        m = jnp.maximum(jnp.max(s, axis=-1, keepdims=True), jnp.max(sc, axis=-1, keepdims=True))
        p = jnp.exp2(s - m)
        pc = jnp.exp2(sc - m)
        rden = 1.0 / (jnp.sum(p, axis=-1, keepdims=True) + jnp.sum(pc, axis=-1, keepdims=True))
        o = _dot(p.astype(BF16), vrows)
        pcb = pc.astype(BF16)
        o_rows.append(jnp.where(lo_row, o[:w], o[w:]))
        rd_lo.append(rden[:w])
        rd_hi.append(rden[w:])
        pc_lo.append(pcb[:w])
        pc_hi.append(pcb[w:])
    oc = jnp.where(lo_all, _dot(jnp.concatenate(pc_lo, axis=0), vc), _dot(jnp.concatenate(pc_hi, axis=0), vc))
    rd = jnp.where(lo_all, jnp.concatenate(rd_lo, axis=0), jnp.concatenate(rd_hi, axis=0))
    o_ref[0] = ((jnp.concatenate(o_rows, axis=0) + oc) * rd).astype(BF16)


def _natten(proj_l, proj_c, bias_tab):
    b, length, _ = proj_l.shape
    lc = proj_c.shape[1]
    rows = length // GRID_W
    assert rows >= NA_KR
    rps = 8
    tq = rps * GRID_W
    nk = NA_KR * GRID_W
    return pl.pallas_call(
        functools.partial(_natten_kernel, rows=rows, rows_per_step=rps),
        grid=(b, 4, rows // rps),
        in_specs=[pl.BlockSpec((1, tq, 128), lambda bi, hp, i: (bi, i, P_AQ // 128 + hp)),
                  pl.BlockSpec((1, length, 128), lambda bi, hp, i: (bi, 0, P_AK // 128 + hp)),
                  pl.BlockSpec((1, length, 128), lambda bi, hp, i: (bi, 0, P_AV // 128 + hp)),
                  pl.BlockSpec((1, lc, 128), lambda bi, hp, i: (bi, 0, P_AK // 128 + hp)),
                  pl.BlockSpec((1, lc, 128), lambda bi, hp, i: (bi, 0, P_AV // 128 + hp)),
                  pl.BlockSpec((NA_KR, 1, 2 * GRID_W, nk), lambda bi, hp, i: (0, hp, 0, 0))],
        out_specs=pl.BlockSpec((1, tq, 128), lambda bi, hp, i: (bi, i, hp)),
        out_shape=jax.ShapeDtypeStruct((b, length, 512), BF16),
        compiler_params=_cparams(("parallel", "parallel", "arbitrary")),
        name="natten",
    )(proj_l, proj_l, proj_l, proj_c, proj_c, bias_tab)


def _natten_bias_table(rpb):
    w = GRID_W
    nd = 2 * NA_KC - 1
    qc = np.arange(w)[:, None]
    kcol = np.arange(w)[None, :]
    cs = np.clip(qc - NA_KC // 2, 0, w - NA_KC)
    cmask = (kcol >= cs) & (kcol < cs + NA_KC)
    period = nd + w
    padded = jnp.pad(rpb.astype(F32), ((0, 0), (0, 0), (0, w)))
    flat = jnp.tile(padded, (1, 1, w))[:, :, :w * (period - 1)]
    t = flat.reshape(NA_HEADS, 2 * NA_KR - 1, w, period - 1)[..., NA_KC - 1:NA_KC - 1 + w]
    g = jnp.stack([t[:, off:off + NA_KR] for off in range(NA_KR)])
    g = jnp.where(cmask[None, None, None], g * LOG2E, NEG_INF)
    return g.transpose(0, 1, 3, 2, 4).reshape(NA_KR, NA_HEADS // 2, 2 * w, NA_KR * w)


def _gla_kernel(*refs, rev, ncb, final):
    if final:
        (q_ref, k_ref, v_ref, lr_ref, whi_ref, wlo_ref, b_ref, s0_ref, of_ref, gnw_ref,
         o_ref, sfin_ref, st_ref) = refs
    else:
        (q_ref, k_ref, v_ref, lr_ref, whi_ref, wlo_ref, b_ref, s0_ref,
         o_ref, sfin_ref, st_ref) = refs
    ch = GLA_CHUNK

    @pl.when(pl.program_id(2) == 0)
    def _():
        st_ref[...] = s0_ref[0, 0]

    tb = ncb * ch
    rix = lax.broadcasted_iota(jnp.int32, (ch, ch), 0)
    cix = lax.broadcasted_iota(jnp.int32, (ch, ch), 1)
    tri = (rix <= cix) if rev else (rix >= cix)
    lo_lanes = lax.broadcasted_iota(jnp.int32, (tb, LANES), 1) < 64
    end = 0 if rev else ch - 1

    lr = lr_ref[0]
    arg = _dot(lr, whi_ref[0]) + _dot(lr, wlo_ref[0]) + b_ref[0]
    cum = (jnp.minimum(arg, 0.0) - jnp.log(1.0 + jnp.exp(-jnp.abs(arg)))) * (1.0 / GLA_TAU)
    pos = lax.broadcasted_iota(jnp.int32, (tb, 1), 0) % ch
    step = 1
    while step < ch:
        if rev:
            cum = cum + jnp.where(pos < ch - step, pltpu.roll(cum, tb - step, 0), 0.0)
        else:
            cum = cum + jnp.where(pos >= step, pltpu.roll(cum, step, 0), 0.0)
        step *= 2
    q = q_ref[0].astype(F32)
    k = k_ref[0].astype(F32)
    qd = (q * (HEAD_DIM ** -0.5) * jnp.exp(cum)).astype(BF16)
    kd = (k * jnp.exp(-cum)).astype(BF16)
    zero = jnp.zeros_like(qd)
    qd_h = (jnp.where(lo_lanes, qd, zero), jnp.where(lo_lanes, zero, qd))

    order = [(ncb - 1 - cc) if rev else cc for cc in range(ncb)]
    intra = {}
    for c in order:
        sl = slice(c * ch, (c + 1) * ch)
        cum_c = cum[sl]
        cum_end = cum_c[end:end + 1, :]
        kdec = (k[sl] * jnp.exp(cum_end - cum_c)).astype(BF16)
        kd_c = kd[sl]
        for h in range(2):
            vh = v_ref[0, sl, 128 * h:128 * (h + 1)]
            a = jnp.where(tri, _dot_nt(qd_h[h][sl], kd_c), 0.0)
            intra[c, h] = (_dot(a.astype(BF16), vh), _dot_tn(vh, kdec), jnp.exp(cum_end))
    st = [st_ref[0], st_ref[1]]
    for c in order:
        sl = slice(c * ch, (c + 1) * ch)
        for h in range(2):
            o_intra, kv_t, gdec = intra[c, h]
            o = o_intra + _dot_nt(qd_h[h][sl], st[h].astype(BF16))
            st[h] = st[h] * gdec + kv_t
            if final:
                o = o + of_ref[0, sl, 128 * h:128 * (h + 1)]
                ms = jnp.mean(o * o, axis=-1, keepdims=True)
                o = o * lax.rsqrt(ms + EPS) * gnw_ref[...]
                o_ref[0, sl, 128 * h:128 * (h + 1)] = o.astype(BF16)
            else:
                o_ref[0, sl, 128 * h:128 * (h + 1)] = o
    st_ref[0] = st[0]
    st_ref[1] = st[1]
    sfin_ref[0, 0] = st_ref[...]


def _gla_pass(proj, w_pad_hi, w_pad_lo, b_pad, s0, rev, o_fwd=None, gnw=None):
    b, t, _ = proj.shape
    final = o_fwd is not None
    tb = min(t, 512)
    nblk = t // tb
    ncb = tb // GLA_CHUNK
    d = 1 if rev else 0

    def bi_map(i):
        return (nblk - 1 - i) if rev else i

    in_specs = [pl.BlockSpec((1, tb, 128), lambda bb, p, i: (bb, bi_map(i), P_CQ // 128 + p)),
                pl.BlockSpec((1, tb, 128), lambda bb, p, i: (bb, bi_map(i), P_CK // 128 + p)),
                pl.BlockSpec((1, tb, 256), lambda bb, p, i: (bb, bi_map(i), P_CV // 256 + p)),
                pl.BlockSpec((1, tb, 128), lambda bb, p, i: (bb, bi_map(i), P_LR // 128)),
                pl.BlockSpec((1, 128, 128), lambda bb, p, i: (2 * d + p, 0, 0)),
                pl.BlockSpec((1, 128, 128), lambda bb, p, i: (2 * d + p, 0, 0)),
                pl.BlockSpec((1, 1, 128), lambda bb, p, i: (2 * d + p, 0, 0)),
                pl.BlockSpec((1, 1, 2, 128, 128), lambda bb, p, i: (bb, p, 0, 0, 0))]
    args = [proj, proj, proj, proj, w_pad_hi, w_pad_lo, b_pad, s0]
    if final:
        in_specs += [pl.BlockSpec((1, tb, 256), lambda bb, p, i: (bb, bi_map(i), p)),
                     pl.BlockSpec((1, 128), lambda bb, p, i: (0, 0))]
        args += [o_fwd, gnw.reshape(1, 128)]
    out, s_fin = pl.pallas_call(
        functools.partial(_gla_kernel, rev=rev, ncb=ncb, final=final),
        grid=(b, 2, nblk),
        in_specs=in_specs,
        out_specs=[pl.BlockSpec((1, tb, 256), lambda bb, p, i: (bb, bi_map(i), p)),
                   pl.BlockSpec((1, 1, 2, 128, 128), lambda bb, p, i: (bb, p, 0, 0, 0))],
        out_shape=[jax.ShapeDtypeStruct((b, t, 512), BF16 if final else F32),
                   jax.ShapeDtypeStruct((b, 2, 2, 128, 128), F32)],
        scratch_shapes=[pltpu.VMEM((2, 128, 128), F32)],
        compiler_params=_cparams(("parallel", "parallel", "arbitrary")),
        name="gla_rev" if rev else "gla_fwd",
    )(*args)
    return out, s_fin


def _gla_decay_weights(w_alpha_up, b_alpha):
    w = jnp.zeros((2, 2, 128, 128), F32)
    for d in range(2):
        for p in range(2):
            w = w.at[d, p, 16 * d:16 * d + 16, :].set(w_alpha_up[d][:, 128 * p:128 * (p + 1)].astype(F32))
    w = w.reshape(4, 128, 128)
    hi = w.astype(BF16)
    lo = (w - hi.astype(F32)).astype(BF16)
    return hi, lo, b_alpha.astype(F32).reshape(4, 1, 128)


def _hy_pre_kernel(u0, u1, u2, p0, p1, p2, n0, n1, n2, w_ref, b_ref, x0_ref, z_ref, *, nblk):
    i = pl.program_id(1)
    tm = u0.shape[1]
    row = lax.broadcasted_iota(jnp.int32, (tm, 1), 0)
    has_prev = (i > 0).astype(F32)
    has_next = (i < nblk - 1).astype(F32)

    def conv(u_ref, p_ref, n_ref, j):
        u = u_ref[0].astype(F32)
        prev = p_ref[0, 7:8, :].astype(F32) * has_prev
        nxt = n_ref[0, 0:1, :].astype(F32) * has_next
        up = jnp.where(row == 0, prev, pltpu.roll(u, 1, 0))
        dn = jnp.where(row == tm - 1, nxt, pltpu.roll(u, tm - 1, 0))
        w = w_ref[:, 512 * j:512 * (j + 1)]
        return up * w[0:1] + u * w[1:2] + dn * w[2:3] + b_ref[:, 512 * j:512 * (j + 1)]

    x0_ref[0] = conv(u0, p0, n0, 0).astype(BF16)
    z_ref[0] = (conv(u1, p1, n1, 1) * conv(u2, p2, n2, 2)).astype(BF16)


def _hy_pre(proj, conv_w, conv_b):
    b, t, _ = proj.shape
    tm = min(t, 512)
    nblk = t // tm
    hb = tm // 8
    nrb = t // 8
    c0 = P_DU // 512
    main = [pl.BlockSpec((1, tm, 512), functools.partial(lambda bi, i, j: (bi, i, c0 + j), j=j)) for j in range(3)]
    prev = [pl.BlockSpec((1, 8, 512), functools.partial(lambda bi, i, j: (bi, jnp.maximum(i * hb - 1, 0), c0 + j), j=j))
            for j in range(3)]
    nxt = [pl.BlockSpec((1, 8, 512),
                        functools.partial(lambda bi, i, j: (bi, jnp.minimum((i + 1) * hb, nrb - 1), c0 + j), j=j))
           for j in range(3)]
    return pl.pallas_call(
        functools.partial(_hy_pre_kernel, nblk=nblk),
        grid=(b, nblk),
        in_specs=main + prev + nxt + [pl.BlockSpec((3, 1536), lambda bi, i: (0, 0)),
                                      pl.BlockSpec((1, 1536), lambda bi, i: (0, 0))],
        out_specs=[pl.BlockSpec((1, tm, 512), lambda bi, i: (bi, i, 0))] * 2,
        out_shape=[jax.ShapeDtypeStruct((b, t, 512), BF16)] * 2,
        compiler_params=_cparams(("parallel", "parallel")),
        name="hyena_pre",
    )(*([proj] * 9), conv_w.astype(F32), conv_b.astype(F32).reshape(1, 1536))


def _hy_filter_kernel(z_ref, w1, b1, w2, b2, w3, b3, fr, wo, dl, h_ref, l1_ref, *, tm, t_len):
    i = pl.program_id(0)
    f = fr[...]
    hh = jnp.sin(f * (_dot_x3(z_ref[...], w1[...]) + b1[...]))
    hh = jnp.sin(f * (_dot_x3(hh, w2[...]) + b2[...]))
    hh = jnp.sin(f * (_dot_x3(hh, w3[...]) + b3[...]))
    hh = _dot_x3(hh, wo[...])
    row = i * tm + lax.broadcasted_iota(jnp.int32, (tm, 1), 0)
    t = row.astype(F32) / (t_len - 1)
    decay = jnp.exp(-t * dl[...])
    h_f = hh[:, :HY_WIDTH] * decay
    h_b = jnp.where(row == 0, 0.0, hh[:, HY_WIDTH:] * decay)
    h_ref[0] = h_f.astype(BF16)
    h_ref[1] = h_b.astype(BF16)
    part = jnp.sum(jnp.abs(h_f) + jnp.abs(h_b), axis=0, keepdims=True)

    @pl.when(i == 0)
    def _():
        l1_ref[...] = part

    @pl.when(i > 0)
    def _():
        l1_ref[...] = l1_ref[...] + part


def _hy_filter(t_len, w1, b1, w2, b2, w3, b3, freq, wout):
    tm = min(t_len, 512)
    t = jnp.linspace(0.0, 1.0, t_len, dtype=F32)[:, None]
    bands = (HY_EMB - 1) // 2
    w_ang = 2.0 * math.pi * jnp.arange(t_len, dtype=F32)[:, None] / t_len
    f = jnp.linspace(1e-4, bands - 1, bands, dtype=F32)[None, :]
    z = jnp.concatenate([t, jnp.cos(f * w_ang), -jnp.sin(f * w_ang),
                         jnp.zeros((t_len, LANES - HY_EMB), F32)], axis=-1)
    w1p = jnp.concatenate([w1.astype(F32), jnp.zeros((LANES - HY_EMB, HY_FFN), F32)], axis=0)
    deltas = jnp.abs(jnp.linspace(math.log(HY_TARGET) / HY_SLOW_PCT, math.log(HY_TARGET) / HY_FAST_PCT,
                                  HY_WIDTH, dtype=F32))[None, :]
    full = lambda shape: pl.BlockSpec(shape, lambda i: tuple(0 for _ in shape))
    r = lambda v: v.astype(F32).reshape(1, -1)
    return pl.pallas_call(
        functools.partial(_hy_filter_kernel, tm=tm, t_len=t_len),
        grid=(t_len // tm,),
        in_specs=[pl.BlockSpec((tm, LANES), lambda i: (i, 0)),
                  full((LANES, HY_FFN)), full((1, HY_FFN)), full((HY_FFN, HY_FFN)), full((1, HY_FFN)),
                  full((HY_FFN, HY_FFN)), full((1, HY_FFN)), full((1, HY_FFN)), full((HY_FFN, 2 * HY_WIDTH)),
                  full((1, HY_WIDTH))],
        out_specs=[pl.BlockSpec((2, tm, HY_WIDTH), lambda i: (0, i, 0)),
                   pl.BlockSpec((1, HY_WIDTH), lambda i: (0, 0))],
        out_shape=[jax.ShapeDtypeStruct((2, t_len, HY_WIDTH), BF16), jax.ShapeDtypeStruct((1, HY_WIDTH), F32)],
        compiler_params=_cparams(("arbitrary",)),
        name="hyena_filter",
    )(z, w1p, r(b1), w2.astype(F32), r(b2), w3.astype(F32), r(b3), r(freq), wout.astype(F32), deltas)


BF16_TILE_ROWS = 16
FFT_K1_PER_STEP = 4


def _used_k1(n1):
    need = n1 // 2 + 1
    return min(n1, -(-need // BF16_TILE_ROWS) * BF16_TILE_ROWS)


def _dft_tables(t_len):
    n = 2 * t_len
    n2 = FFT_N2
    n1 = n // n2
    half = n1 // 2
    k1u = _used_k1(n1)
    k1 = np.arange(k1u)[:, None]
    j1 = np.arange(half)[None, :]
    ang1 = 2.0 * np.pi * ((k1 * j1) % n1) / n1
    f1 = np.concatenate([np.cos(ang1), -np.sin(ang1)], axis=0)
    wgt = np.where((k1 == 0) | (k1 == half), 1.0, np.where(k1 < half, 2.0, 0.0))
    i1 = np.concatenate([(wgt * np.cos(ang1)).T, (-wgt * np.sin(ang1)).T], axis=1)
    n1 = k1u
    a = np.arange(n2)
    ang2 = 2.0 * np.pi * ((a[:, None] * a[None, :]) % n2) / n2
    fr, fi = np.cos(ang2), -np.sin(ang2)
    fwd = np.block([[fr, -fi], [fi, fr]])
    inv = np.block([[fr, fi], [-fi, fr]])
    angt = 2.0 * np.pi * ((np.arange(n1)[:, None] * a[None, :]) % n) / n
    twc = np.cos(angt)[:, :, None]
    tws = np.sin(angt)[:, :, None]
    return (jnp.asarray(f1, F32).astype(BF16), jnp.asarray(i1, F32).astype(BF16), jnp.asarray(fwd, F32).astype(BF16),
            jnp.asarray(inv, F32).astype(BF16),
            jnp.broadcast_to(jnp.asarray(twc, F32), (n1, n2, LANES)),
            jnp.broadcast_to(jnp.asarray(tws, F32), (n1, n2, LANES)))


def _hy_s1_kernel(z_ref, f_ref, xr_ref, xi_ref):
    n1 = xr_ref.shape[1]
    y = _dot(f_ref[...], z_ref[0])
    xr_ref[0] = y[:n1].astype(BF16)
    xi_ref[0] = y[n1:].astype(BF16)


def _hy_s1(zv, f1):
    b, half, cols = zv.shape
    n1 = f1.shape[0] // 2
    tn = min(cols, 4096)
    return pl.pallas_call(
        _hy_s1_kernel,
        grid=(b, cols // tn),
        in_specs=[pl.BlockSpec((1, half, tn), lambda bi, j: (bi, 0, j)),
                  pl.BlockSpec((2 * n1, half), lambda bi, j: (0, 0))],
        out_specs=[pl.BlockSpec((1, n1, tn), lambda bi, j: (bi, 0, j))] * 2,
        out_shape=[jax.ShapeDtypeStruct((b, n1, cols), BF16)] * 2,
        compiler_params=_cparams(("parallel", "parallel")),
        name="hyena_dft_outer",
    )(zv, f1)


def _twiddle_fwd(xr, xi, c, s):
    return xr * c + xi * s, xi * c - xr * s


def _lane_tile(ref, j, reps):
    v = ref[j]
    return jnp.concatenate([v] * reps, axis=1)


def _hy_filt_spec_kernel(xr_ref, xi_ref, twc_ref, tws_ref, fwd_ref, l1_ref, kr_ref, ki_ref, *, n):
    reps = xr_ref.shape[-1] // LANES
    n2 = FFT_N2
    scale = 1.0 / (l1_ref[...] * n)
    for j in range(FFT_K1_PER_STEP):
        c = _lane_tile(twc_ref, j, reps)
        s = _lane_tile(tws_ref, j, reps)
        parts = []
        for f in range(2):
            ar, ai = _twiddle_fwd(xr_ref[f, j].astype(F32), xi_ref[f, j].astype(F32), c, s)
            y = _dot(fwd_ref[...], jnp.concatenate([ar, ai], axis=0).astype(BF16))
            parts.append((y[:n2], y[n2:]))
        kr_ref[j] = ((parts[0][0] + parts[1][0]) * scale).astype(BF16)
        ki_ref[j] = ((parts[0][1] - parts[1][1]) * scale).astype(BF16)


def _hy_filt_spec(xr, xi, twc, tws, fwd, l1, n):
    _, n1, n2, w = xr.shape
    kb = FFT_K1_PER_STEP
    return pl.pallas_call(
        functools.partial(_hy_filt_spec_kernel, n=n),
        grid=(n1 // kb,),
        in_specs=[pl.BlockSpec((2, kb, n2, w), lambda k: (0, k, 0, 0)),
                  pl.BlockSpec((2, kb, n2, w), lambda k: (0, k, 0, 0)),
                  pl.BlockSpec((kb, n2, LANES), lambda k: (k, 0, 0)),
                  pl.BlockSpec((kb, n2, LANES), lambda k: (k, 0, 0)),
                  pl.BlockSpec((2 * n2, 2 * n2), lambda k: (0, 0)),
                  pl.BlockSpec((1, w), lambda k: (0, 0))],
        out_specs=[pl.BlockSpec((kb, n2, w), lambda k: (k, 0, 0))] * 2,
        out_shape=[jax.ShapeDtypeStruct((n1, n2, w), BF16)] * 2,
        compiler_params=_cparams(("parallel",)),
        name="hyena_filter_spectrum",
    )(xr, xi, twc, tws, fwd, l1)


def _hy_mid_kernel(xr_ref, xi_ref, twc_ref, tws_ref, fwd_ref, inv_ref, kr_ref, ki_ref, tr_ref, ti_ref):
    reps = xr_ref.shape[-1] // LANES
    n2 = FFT_N2
    for j in range(FFT_K1_PER_STEP):
        c = _lane_tile(twc_ref, j, reps)
        s = _lane_tile(tws_ref, j, reps)
        ar, ai = _twiddle_fwd(xr_ref[0, j].astype(F32), xi_ref[0, j].astype(F32), c, s)
        y = _dot(fwd_ref[...], jnp.concatenate([ar, ai], axis=0).astype(BF16))
        yr, yi = y[:n2], y[n2:]
        kr = kr_ref[j].astype(F32)
        ki = ki_ref[j].astype(F32)
        zr = yr * kr - yi * ki
        zi = yr * ki + yi * kr
        u = _dot(inv_ref[...], jnp.concatenate([zr, zi], axis=0).astype(BF16))
        ur, ui = u[:n2], u[n2:]
        tr_ref[0, j] = (ur * c - ui * s).astype(BF16)
        ti_ref[0, j] = (ur * s + ui * c).astype(BF16)


def _hy_mid(xr, xi, twc, tws, fwd, inv, kr, ki):
    b, n1, n2, w = xr.shape
    kb = FFT_K1_PER_STEP
    xspec = pl.BlockSpec((1, kb, n2, w), lambda bi, k: (bi, k, 0, 0))
    return pl.pallas_call(
        _hy_mid_kernel,
        grid=(b, n1 // kb),
        in_specs=[xspec, xspec,
                  pl.BlockSpec((kb, n2, LANES), lambda bi, k: (k, 0, 0)),
                  pl.BlockSpec((kb, n2, LANES), lambda bi, k: (k, 0, 0)),
                  pl.BlockSpec((2 * n2, 2 * n2), lambda bi, k: (0, 0)),
                  pl.BlockSpec((2 * n2, 2 * n2), lambda bi, k: (0, 0)),
                  pl.BlockSpec((kb, n2, w), lambda bi, k: (k, 0, 0)),
                  pl.BlockSpec((kb, n2, w), lambda bi, k: (k, 0, 0))],
        out_specs=[xspec, xspec],
        out_shape=[jax.ShapeDtypeStruct((b, n1, n2, w), BF16)] * 2,
        compiler_params=_cparams(("parallel", "parallel")),
        name="hyena_dft_inner",
    )(xr, xi, twc, tws, fwd, inv, kr, ki)


def _hy_post_kernel(tr_ref, ti_ref, i1_ref, x0_ref, z_ref, fb_ref, o_ref):
    t = jnp.concatenate([tr_ref[0], ti_ref[0]], axis=0)
    y = _dot(i1_ref[...], t)
    z = z_ref[0].astype(F32)
    o_ref[0] = (x0_ref[0].astype(F32) * (y + z * fb_ref[...])).astype(BF16)


def _hy_post(tr, ti, i1, x0v, zv, fb_tiled):
    b, n1, cols = tr.shape
    half = i1.shape[0]
    tn = fb_tiled.shape[1]
    hspec = pl.BlockSpec((1, half, tn), lambda bi, j: (bi, 0, j))
    tspec = pl.BlockSpec((1, n1, tn), lambda bi, j: (bi, 0, j))
    return pl.pallas_call(
        _hy_post_kernel,
        grid=(b, cols // tn),
        in_specs=[tspec, tspec, pl.BlockSpec((half, 2 * n1), lambda bi, j: (0, 0)), hspec, hspec,
                  pl.BlockSpec((1, tn), lambda bi, j: (0, 0))],
        out_specs=hspec,
        out_shape=jax.ShapeDtypeStruct((b, half, cols), BF16),
        compiler_params=_cparams(("parallel", "parallel")),
        name="hyena_dft_outer_inv",
    )(tr, ti, i1, x0v, zv, fb_tiled)


def _hyena_long(proj, conv_w, conv_b, filt, filt_bias):
    b, t, _ = proj.shape
    w = HY_WIDTH
    n = 2 * t
    n2 = FFT_N2
    n1 = n // n2
    k1u = _used_k1(n1)
    f1, i1, fwd, inv, twc, tws = _dft_tables(t)
    hfb, l1 = _hy_filter(t, *filt)
    hr, hi = _hy_s1(hfb.reshape(2, n1 // 2, n2 * w), f1)
    kr, ki = _hy_filt_spec(hr.reshape(2, k1u, n2, w), hi.reshape(2, k1u, n2, w), twc, tws, fwd, l1, n)
    x0, z = _hy_pre(proj, conv_w, conv_b)
    zv = z.reshape(b, n1 // 2, n2 * w)
    xr, xi = _hy_s1(zv, f1)
    tr, ti = _hy_mid(xr.reshape(b, k1u, n2, w), xi.reshape(b, k1u, n2, w), twc, tws, fwd, inv, kr, ki)
    tn = min(n2 * w, 4096)
    fb_tiled = jnp.tile(filt_bias.astype(F32).reshape(1, w), (1, tn // w))
    d = _hy_post(tr.reshape(b, k1u, n2 * w), ti.reshape(b, k1u, n2 * w), i1, x0.reshape(b, n1 // 2, n2 * w), zv,
                 fb_tiled)
    return d.reshape(b, t, w)


def _hy_ctx_kernel(x0_ref, z_ref, h_ref, l1_ref, cm_ref, sm_ref, cmt_ref, smt_ref, fb_ref, o_ref, *, n):
    cm = cm_ref[...]
    sm = sm_ref[...]

    def spectrum(v):
        return _dot(cm, v), -_dot(sm, v)

    z = z_ref[0]
    zr, zi = spectrum(z)
    ar, ai = spectrum(h_ref[0])
    br, bi = spectrum(h_ref[1])
    scale = 1.0 / (l1_ref[...] * n)
    kr = (ar + br) * scale
    ki = (ai - bi) * scale
    yr = (zr * kr - zi * ki).astype(BF16)
    yi = (zr * ki + zi * kr).astype(BF16)
    y = _dot(cmt_ref[...], yr) - _dot(smt_ref[...], yi)
    o_ref[0] = (x0_ref[0].astype(F32) * (y + z.astype(F32) * fb_ref[...])).astype(BF16)


def _hyena_short(proj, conv_w, conv_b, filt, filt_bias):
    b, t, _ = proj.shape
    w = HY_WIDTH
    n = 2 * t
    hfb, l1 = _hy_filter(t, *filt)
    x0, z = _hy_pre(proj, conv_w, conv_b)
    k = np.arange(n)[:, None]
    j = np.arange(t)[None, :]
    ang = 2.0 * np.pi * ((k * j) % n) / n
    cm, sm = np.cos(ang), np.sin(ang)
    tabs = [jnp.asarray(a, F32).astype(BF16) for a in (cm, sm, cm.T, sm.T)]
    full2 = lambda shape: pl.BlockSpec(shape, lambda bi: (0, 0))
    bspec = pl.BlockSpec((1, t, w), lambda bi: (bi, 0, 0))
    return pl.pallas_call(
        functools.partial(_hy_ctx_kernel, n=n),
        grid=(b,),
        in_specs=[bspec, bspec, pl.BlockSpec((2, t, w), lambda bi: (0, 0, 0)), full2((1, w)),
                  full2((n, t)), full2((n, t)), full2((t, n)), full2((t, n)), full2((1, w))],
        out_specs=bspec,
        out_shape=jax.ShapeDtypeStruct((b, t, w), BF16),
        compiler_params=_cparams(("parallel",)),
        name="hyena_ctx",
    )(x0, z, hfb, l1, *tabs, filt_bias.astype(F32).reshape(1, w))


def _layer(xc, xl, mod, cos128, sin128, norm_w, w_in, rpb, sink, w_alpha_up, b_alpha, gla_norm_w, conv_w, conv_b,
           filt, filt_bias, w_branch, w_out, final_w, with_ctx_out):
    b, length, d = xl.shape
    mod_l = mod[:b].reshape(b, 1, 3 * d)
    mod_c = jnp.broadcast_to(mod[b].reshape(1, 1, 3 * d), (b, 1, 3 * d))
    sh_l, sc_l, g_l = mod_l[..., :d], mod_l[..., d:2 * d], mod_l[..., 2 * d:]
    sh_c, sc_c, g_c = mod_c[..., :d], mod_c[..., d:2 * d], mod_c[..., 2 * d:]

    w_perm = _permute(w_in.astype(BF16), _IN_PERM, axis=1) * jnp.asarray(_HALF_COLS, BF16)[None, :]
    wbf = w_branch.astype(BF16) * 0.5
    wb = jnp.stack([wbf[0], _permute(wbf[1], _SWA_OUT_PERM, axis=0), wbf[2], wbf[3]])
    wo = w_out.astype(BF16)

    proj_l = _inproj(xl, norm_w, sc_l, sh_l, w_perm)
    proj_c = _inproj(xc, norm_w, sc_c, sh_c, w_perm)

    y_a = _natten(proj_l, proj_c, _natten_bias_table(rpb))
    sink128 = jnp.broadcast_to(sink.astype(F32)[:, None], (SW_HEADS, LANES))
    qr, kr = _rope(proj_l, cos128, sin128)
    y_b = _swa(qr, kr, proj_l, proj_c, proj_c, sink128 * LOG2E)
    whi, wlo, bpad = _gla_decay_weights(w_alpha_up, b_alpha)
    s_zero = jnp.zeros((b, 2, 2, 128, 128), F32)
    of_c, s_cf = _gla_pass(proj_c, whi, wlo, bpad, s_zero, rev=False)
    y_cc, s_cb = _gla_pass(proj_c, whi, wlo, bpad, s_zero, rev=True, o_fwd=of_c, gnw=gla_norm_w.astype(F32))
    of_l, _ = _gla_pass(proj_l, whi, wlo, bpad, s_cf, rev=False)
    y_c, _ = _gla_pass(proj_l, whi, wlo, bpad, s_cb, rev=True, o_fwd=of_l, gnw=gla_norm_w.astype(F32))
    y_d = _hyena_long(proj_l, conv_w, conv_b, filt, filt_bias)

    xl_new = _merge([y_a, y_b, y_c, y_d], proj_l, wb, wo, g_l, xl, final_w, final=not with_ctx_out)
    if with_ctx_out:
        ya_c = _ctx_attn(proj_c, sink128, swa=False)
        yb_c = _ctx_attn(proj_c, sink128, swa=True)
        yd_c = _hyena_short(proj_c, conv_w, conv_b, filt, filt_bias)
        xc = _merge([ya_c, yb_c, y_cc, yd_c], proj_c, wb, wo, g_c, xc, final_w, final=False)
    return xc, xl_new


def kernel(x, c, ctx, c_ctx, norm_w, w_mod, b_mod, w_in, rpb, sink, w_alpha_up, b_alpha, gla_norm_w, conv_w, conv_b,
           filt_w1, filt_b1, filt_w2, filt_b2, filt_w3, filt_b3, filt_freq, filt_wout, filt_bias, w_branch, w_out,
           final_norm_w):
    b, length, d = x.shape
    depth = norm_w.shape[0]
    cvec = jnp.zeros((8, d), F32).at[:b].set(c.astype(F32)).at[b].set(c_ctx.astype(F32))
    mod = _modulation(cvec, w_mod.astype(F32), b_mod.astype(F32))
    cos128, sin128 = _rope_tables(length)
    xc, xl = ctx, x
    for i in range(depth):
        filt = (filt_w1[i], filt_b1[i], filt_w2[i], filt_b2[i], filt_w3[i], filt_b3[i], filt_freq[i], filt_wout[i])
        xc, xl = _layer(xc, xl, mod[i], cos128, sin128, norm_w[i], w_in[i], rpb[i], sink[i], w_alpha_up[i],
                        b_alpha[i], gla_norm_w[i], conv_w[i], conv_b[i], filt, filt_bias[i], w_branch[i], w_out[i],
                        final_norm_w, with_ctx_out=(i < depth - 1))
    return xl
```

```python
import functools
import math

import numpy as np
import jax
import jax.numpy as jnp
from jax import lax
from jax.experimental import pallas as pl
from jax.experimental.pallas import tpu as pltpu

F32 = jnp.float32
BF16 = jnp.bfloat16

D_MODEL = 1024
GRID_W = 64
HEAD_DIM = 64
BRANCH_WIDTH = D_MODEL // 2
N_BRANCH = 4
NA_HEADS = 8
NA_KR = 8
NA_KC = 16
SW_HEADS = 8
SW_KV_HEADS = 2
SW_WINDOW = 128
SW_BLOCK = 128
GLA_HEADS = 4
GLA_DK = BRANCH_WIDTH // 2
GLA_DV = BRANCH_WIDTH
GLA_RANK = 16
GLA_TAU = 16.0
GLA_CHUNK = 64
HY_WIDTH = BRANCH_WIDTH
HY_EMB = 33
HY_FFN = 64
HY_TARGET = 1e-2
HY_FAST_PCT = 0.3
HY_SLOW_PCT = 1.5
ROPE_BASE = 10000.0
EPS = 1e-6
NEG_INF = -1e30
LOG2E = math.log2(math.e)
LANES = 128
FFT_N2 = 128
VMEM_LIMIT = 56 * 1024 * 1024

_IN_WIDTHS = (512, 512, 512, 512, 512, 128, 128, 512, 256, 256, 512, 16, 16, 512, 1536, 512, 4096)
_IN_OFF = np.concatenate([[0], np.cumsum(_IN_WIDTHS)])
(_O_AQ, _O_AK, _O_AV, _O_AG, _O_BQ, _O_BK, _O_BV, _O_BG, _O_CQ, _O_CK, _O_CV, _O_LRF, _O_LRB, _O_CG,
 _O_DU, _O_DG, _O_GM) = [int(v) for v in _IN_OFF[:-1]]
IN_TOTAL = int(_IN_OFF[-1])

P_AQ, P_AK, P_AV, P_AG = 0, 512, 1024, 1536
P_BQ, P_BG = 2048, 2560
P_CV, P_CG = 3072, 3584
P_DU, P_DG = 4096, 5632
P_GM = 6144
P_CQ, P_CK = 10240, 10496
P_BK, P_BV = 10752, 10880
P_LR = 11008
NP_COLS = 11264


def _swa_q_perm():
    idx = np.zeros(512, np.int64)
    for t in range(4):
        a, b = t, t + 4
        base = 128 * t
        idx[base + 0:base + 32] = 64 * a + np.arange(32)
        idx[base + 32:base + 64] = 64 * b + np.arange(32)
        idx[base + 64:base + 96] = 64 * a + 32 + np.arange(32)
        idx[base + 96:base + 128] = 64 * b + 32 + np.arange(32)
    return idx


def _swa_k_perm():
    idx = np.zeros(128, np.int64)
    idx[0:32] = np.arange(32)
    idx[32:64] = 64 + np.arange(32)
    idx[64:96] = 32 + np.arange(32)
    idx[96:128] = 96 + np.arange(32)
    return idx


def _swa_out_perm():
    idx = np.zeros(512, np.int64)
    for t in range(4):
        idx[128 * t:128 * t + 64] = 64 * t + np.arange(64)
        idx[128 * t + 64:128 * t + 128] = 64 * (t + 4) + np.arange(64)
    return idx


def _build_in_perm():
    perm = np.full(NP_COLS, -1, np.int64)

    def put(p, o, w):
        perm[p:p + w] = o + np.arange(w)

    put(P_AQ, _O_AQ, 512); put(P_AK, _O_AK, 512); put(P_AV, _O_AV, 512); put(P_AG, _O_AG, 512)
    perm[P_BQ:P_BQ + 512] = _O_BQ + _swa_q_perm()
    perm[P_BG:P_BG + 512] = _O_BG + _swa_out_perm()
    put(P_CV, _O_CV, 512); put(P_CG, _O_CG, 512)
    put(P_DU, _O_DU, 1536); put(P_DG, _O_DG, 512)
    put(P_GM, _O_GM, 4096)
    put(P_CQ, _O_CQ, 256); put(P_CK, _O_CK, 256)
    perm[P_BK:P_BK + 128] = _O_BK + _swa_k_perm()
    put(P_BV, _O_BV, 128)
    put(P_LR, _O_LRF, 16); put(P_LR + 16, _O_LRB, 16)
    return perm


def _build_half_cols():
    s = np.ones(NP_COLS, np.float32)
    for p, w in ((P_AG, 512), (P_BG, 512), (P_CG, 512), (P_DG, 512), (P_GM, 4096)):
        s[p:p + w] = 0.5
    return s


_HALF_COLS = _build_half_cols()
_IN_PERM = _build_in_perm()
_SWA_OUT_PERM = _swa_out_perm()


def _permute(w, perm, axis):
    pieces = []
    i = 0
    n = len(perm)
    while i < n:
        j = i + 1
        if perm[i] < 0:
            while j < n and perm[j] < 0:
                j += 1
            shape = list(w.shape)
            shape[axis] = j - i
            pieces.append(jnp.zeros(shape, w.dtype))
        else:
            while j < n and perm[j] == perm[j - 1] + 1:
                j += 1
            pieces.append(lax.slice_in_dim(w, int(perm[i]), int(perm[j - 1]) + 1, axis=axis))
        i = j
    return jnp.concatenate(pieces, axis=axis)


def _cparams(sem):
    return pltpu.CompilerParams(dimension_semantics=sem, vmem_limit_bytes=VMEM_LIMIT)


def _sigmoid(x):
    return 1.0 / (1.0 + jnp.exp(-x))


def _sigmoid_tanh(x):
    return 0.5 * jnp.tanh(0.5 * x) + 0.5


def _split3(a):
    hi = a.astype(BF16)
    r1 = a - hi.astype(F32)
    mid = r1.astype(BF16)
    lo = (r1 - mid.astype(F32)).astype(BF16)
    return hi, mid, lo


def _dot(a, b):
    return jnp.dot(a, b, preferred_element_type=F32)


def _dot_nt(a, b):
    return lax.dot_general(a, b, (((1,), (1,)), ((), ())), preferred_element_type=F32)


def _dot_tn(a, b):
    return lax.dot_general(a, b, (((0,), (0,)), ((), ())), preferred_element_type=F32)


def _dot_x3(a, b):
    ah, am, _ = _split3(a)
    bh, bm, _ = _split3(b)
    return _dot(ah, bh) + (_dot(ah, bm) + _dot(am, bh))


def _mod_kernel(c_ref, w_ref, b_ref, o_ref):
    c = c_ref[...]
    s = c * _sigmoid(c)
    o_ref[0] = _dot_x3(s, w_ref[0]) + b_ref[0]


def _modulation(cvec, w_mod, b_mod):
    depth, d, n = w_mod.shape
    tn = 512
    return pl.pallas_call(
        _mod_kernel,
        grid=(depth, n // tn),
        in_specs=[pl.BlockSpec((8, d), lambda l, j: (0, 0)),
                  pl.BlockSpec((1, d, tn), lambda l, j: (l, 0, j)),
                  pl.BlockSpec((1, 1, tn), lambda l, j: (l, 0, j))],
        out_specs=pl.BlockSpec((1, 8, tn), lambda l, j: (l, 0, j)),
        out_shape=jax.ShapeDtypeStruct((depth, 8, n), F32),
        compiler_params=_cparams(("parallel", "parallel")),
        name="modulation",
    )(cvec, w_mod, b_mod.reshape(depth, 1, n))


def _inproj_kernel(x_ref, nw_ref, sc_ref, sh_ref, w_ref, o_ref, h_ref):
    @pl.when(pl.program_id(2) == 0)
    def _():
        x = x_ref[0]
        ms = jnp.mean(x * x, axis=-1, keepdims=True)
        y = x * lax.rsqrt(ms + EPS) * nw_ref[...]
        h_ref[...] = (y * (1.0 + sc_ref[0]) + sh_ref[0]).astype(BF16)

    o_ref[0] = _dot(h_ref[...], w_ref[...]).astype(BF16)


def _inproj(x, norm_w, scale, shift, w_perm):
    b, t, d = x.shape
    n = w_perm.shape[1]
    tm = min(t, 2048)
    tn = 1024
    return pl.pallas_call(
        _inproj_kernel,
        grid=(b, t // tm, n // tn),
        in_specs=[pl.BlockSpec((1, tm, d), lambda bi, i, j: (bi, i, 0)),
                  pl.BlockSpec((1, d), lambda bi, i, j: (0, 0)),
                  pl.BlockSpec((1, 1, d), lambda bi, i, j: (bi, 0, 0)),
                  pl.BlockSpec((1, 1, d), lambda bi, i, j: (bi, 0, 0)),
                  pl.BlockSpec((d, tn), lambda bi, i, j: (0, j))],
        out_specs=pl.BlockSpec((1, tm, tn), lambda bi, i, j: (bi, i, j)),
        out_shape=jax.ShapeDtypeStruct((b, t, n), BF16),
        scratch_shapes=[pltpu.VMEM((tm, d), BF16)],
        compiler_params=_cparams(("parallel", "parallel", "arbitrary")),
        name="inproj",
    )(x, norm_w.reshape(1, d), scale, shift, w_perm)


def _merge_kernel(ya, yb, yc, yd, ga, gb, gc, gd, m0, m1, m2, m3, wb_ref, wo_ref, g_ref, x_ref, fw_ref,
                  o_ref, *, final):
    acc = None
    for i, (y, g, gm) in enumerate(((ya, ga, m0), (yb, gb, m1), (yc, gc, m2), (yd, gd, m3))):
        hg = g[0].astype(F32)
        yg = y[0].astype(F32) * hg
        ys = (yg + yg * jnp.tanh(hg)).astype(BF16)
        hp = _dot(ys, wb_ref[i])
        t = hp + hp * jnp.tanh(gm[0].astype(F32))
        acc = t if acc is None else acc + t
    out = x_ref[0] + g_ref[0] * _dot(acc.astype(BF16), wo_ref[...])
    if final:
        ms = jnp.mean(out * out, axis=-1, keepdims=True)
        out = out * lax.rsqrt(ms + EPS) * fw_ref[...]
    o_ref[0] = out


def _merge(ys, proj, w_branch, w_out, gate, x, final_w, final):
    b, t, d = x.shape
    bw = BRANCH_WIDTH
    tm = min(t, 512)
    yspec = pl.BlockSpec((1, tm, bw), lambda bi, i: (bi, i, 0))

    def pspec(col, width):
        blk = col // width
        return pl.BlockSpec((1, tm, width), lambda bi, i: (bi, i, blk))

    in_specs = ([yspec] * 4
                + [pspec(P_AG, bw), pspec(P_BG, bw), pspec(P_CG, bw), pspec(P_DG, bw)]
                + [pspec(P_GM + k * d, d) for k in range(N_BRANCH)]
                + [pl.BlockSpec((N_BRANCH, bw, d), lambda bi, i: (0, 0, 0)),
                   pl.BlockSpec((d, d), lambda bi, i: (0, 0)),
                   pl.BlockSpec((1, 1, d), lambda bi, i: (bi, 0, 0)),
                   pl.BlockSpec((1, tm, d), lambda bi, i: (bi, i, 0)),
                   pl.BlockSpec((1, d), lambda bi, i: (0, 0))])
    return pl.pallas_call(
        functools.partial(_merge_kernel, final=final),
        grid=(b, t // tm),
        in_specs=in_specs,
        out_specs=pl.BlockSpec((1, tm, d), lambda bi, i: (bi, i, 0)),
        out_shape=jax.ShapeDtypeStruct((b, t, d), F32),
        compiler_params=_cparams(("parallel", "parallel")),
        name="merge",
    )(*ys, *([proj] * 8), w_branch, w_out, gate, x, final_w.reshape(1, d))


def _rope_kernel(q_ref, k_ref, cos_ref, sin_ref, qo_ref, ko_ref):
    cos = cos_ref[...]
    sin = sin_ref[...]

    def rot(x):
        return x * cos + pltpu.roll(x, 64, 1) * sin

    for t in range(4):
        q = q_ref[0, :, 128 * t:128 * (t + 1)].astype(F32)
        qo_ref[0, :, 128 * t:128 * (t + 1)] = (rot(q) * (HEAD_DIM ** -0.5 * LOG2E)).astype(BF16)
    ko_ref[0] = rot(k_ref[0].astype(F32)).astype(BF16)


def _rope(proj, cos128, sin128):
    b, t, _ = proj.shape
    tm = min(t, 1024)
    return pl.pallas_call(
        _rope_kernel,
        grid=(b, t // tm),
        in_specs=[pl.BlockSpec((1, tm, 512), lambda bi, i: (bi, i, P_BQ // 512)),
                  pl.BlockSpec((1, tm, 128), lambda bi, i: (bi, i, P_BK // 128)),
                  pl.BlockSpec((tm, 128), lambda bi, i: (i, 0)),
                  pl.BlockSpec((tm, 128), lambda bi, i: (i, 0))],
        out_specs=[pl.BlockSpec((1, tm, 512), lambda bi, i: (bi, i, 0)),
                   pl.BlockSpec((1, tm, 128), lambda bi, i: (bi, i, 0))],
        out_shape=[jax.ShapeDtypeStruct((b, t, 512), BF16), jax.ShapeDtypeStruct((b, t, 128), BF16)],
        compiler_params=_cparams(("parallel", "parallel")),
        name="rope",
    )(proj, proj, cos128, sin128)


def _rope_tables(length):
    t = jnp.arange(length, dtype=jnp.int32)
    row = (t // GRID_W).astype(F32)
    col = (t % GRID_W).astype(F32)
    n_freq = HEAD_DIM // 4
    inv = ROPE_BASE ** (-jnp.arange(n_freq, dtype=F32) / n_freq)
    ang = jnp.concatenate([row[:, None] * inv, col[:, None] * inv], axis=-1)
    cos, sin = jnp.cos(ang), jnp.sin(ang)
    return jnp.tile(cos, (1, 4)), jnp.concatenate([-sin, -sin, sin, sin], axis=-1)


def _swa_kernel(q_ref, k_ref, v_ref, kc_ref, vc_ref, sink_ref, mask_ref, o_ref, *, length):
    blk = SW_BLOCK
    kw = 3 * blk
    nb = length // blk
    kc = kc_ref[0]
    vc = vc_ref[0]
    lane = lax.broadcasted_iota(jnp.int32, (blk, LANES), 1)
    a_lanes = (lane // 32) % 2 == 0
    lo_lanes = lane < 64
    top = lax.broadcasted_iota(jnp.int32, (2 * blk, 1), 0) < blk
    for sb in range(q_ref.shape[1] // blk):
        i = pl.program_id(1) * (q_ref.shape[1] // blk) + sb
        rows = slice(sb * blk, (sb + 1) * blk)
        start = pl.multiple_of(jnp.clip((i - 1) * blk, 0, length - kw), blk)
        kwin = k_ref[0, pl.ds(start, kw), :]
        vwin = v_ref[0, pl.ds(start, kw), :]
        variant = jnp.where(i == 0, 0, jnp.where(i == nb - 1, 2, 1))
        for t in range(4):
            qt = q_ref[0, rows, 128 * t:128 * (t + 1)]
            zero = jnp.zeros_like(qt)
            qq = jnp.concatenate([jnp.where(a_lanes, qt, zero), jnp.where(a_lanes, zero, qt)], axis=0)
            sw = _dot_nt(qq, kwin) + mask_ref[variant]
            sc = _dot_nt(qq, kc)
            sk = jnp.where(top, sink_ref[t:t + 1, 0:1], sink_ref[t + 4:t + 5, 0:1])
            m = jnp.maximum(jnp.maximum(jnp.max(sw, axis=-1, keepdims=True), jnp.max(sc, axis=-1, keepdims=True)), sk)
            pw = jnp.exp2(sw - m)
            pc = jnp.exp2(sc - m)
            den = jnp.sum(pw, axis=-1, keepdims=True) + jnp.sum(pc, axis=-1, keepdims=True) + jnp.exp2(sk - m)
            o = (_dot(pw.astype(BF16), vwin) + _dot(pc.astype(BF16), vc)) / den
            o_ref[0, rows, 128 * t:128 * (t + 1)] = jnp.where(lo_lanes, o[:blk], o[blk:]).astype(BF16)


def _swa(qr, kr, proj_l, kc_rot, proj_c, sink128):
    b, length, _ = qr.shape
    lc = proj_c.shape[1]
    assert length >= 3 * SW_BLOCK
    tq = 2 * SW_BLOCK if length % (2 * SW_BLOCK) == 0 else SW_BLOCK
    r = np.arange(2 * SW_BLOCK)[:, None] % SW_BLOCK
    c = np.arange(3 * SW_BLOCK)[None, :]
    mask_tab = jnp.asarray(np.stack([np.where(np.abs(v * SW_BLOCK + r - c) <= SW_WINDOW, 0.0, NEG_INF)
                                     for v in range(3)]), F32)
    return pl.pallas_call(
        functools.partial(_swa_kernel, length=length),
        grid=(b, length // tq),
        in_specs=[pl.BlockSpec((1, tq, 512), lambda bi, i: (bi, i, 0)),
                  pl.BlockSpec((1, length, 128), lambda bi, i: (bi, 0, 0)),
                  pl.BlockSpec((1, length, 128), lambda bi, i: (bi, 0, P_BV // 128)),
                  pl.BlockSpec((1, lc, 128), lambda bi, i: (bi, 0, P_BK // 128)),
                  pl.BlockSpec((1, lc, 128), lambda bi, i: (bi, 0, P_BV // 128)),
                  pl.BlockSpec((8, 128), lambda bi, i: (0, 0)),
                  pl.BlockSpec((3, 2 * SW_BLOCK, 3 * SW_BLOCK), lambda bi, i: (0, 0, 0))],
        out_specs=pl.BlockSpec((1, tq, 512), lambda bi, i: (bi, i, 0)),
        out_shape=jax.ShapeDtypeStruct((b, length, 512), BF16),
        compiler_params=_cparams(("parallel", "arbitrary")),
        name="swa",
    )(qr, kr, proj_l, kc_rot, proj_c, sink128, mask_tab)


def _ctx_attn_kernel(q_ref, k_ref, v_ref, sink_ref, o_ref, *, swa):
    t_len = q_ref.shape[1]
    lane = lax.broadcasted_iota(jnp.int32, (t_len, LANES), 1)
    lo_lanes = lane < 64
    a_lanes = ((lane // 32) % 2 == 0) if swa else lo_lanes
    top = lax.broadcasted_iota(jnp.int32, (2 * t_len, 1), 0) < t_len
    for t in range(4):
        qt = (q_ref[0, :, 128 * t:128 * (t + 1)].astype(F32) * HEAD_DIM ** -0.5).astype(BF16)
        zero = jnp.zeros_like(qt)
        qq = jnp.concatenate([jnp.where(a_lanes, qt, zero), jnp.where(a_lanes, zero, qt)], axis=0)
        if swa:
            kt, vt = k_ref[0], v_ref[0]
        else:
            kt, vt = k_ref[0, :, 128 * t:128 * (t + 1)], v_ref[0, :, 128 * t:128 * (t + 1)]
        s = _dot_nt(qq, kt)
        m = jnp.max(s, axis=-1, keepdims=True)
        if swa:
            sk = jnp.where(top, sink_ref[t:t + 1, 0:1], sink_ref[t + 4:t + 5, 0:1])
            m = jnp.maximum(m, sk)
        p = jnp.exp(s - m)
        den = jnp.sum(p, axis=-1, keepdims=True)
        if swa:
            den = den + jnp.exp(sk - m)
        o = _dot(p.astype(BF16), vt) / den
        o_ref[0, :, 128 * t:128 * (t + 1)] = jnp.where(lo_lanes, o[:t_len], o[t_len:]).astype(BF16)


def _ctx_attn(proj_c, sink128, swa):
    b, lc, _ = proj_c.shape
    if swa:
        qs = pl.BlockSpec((1, lc, 512), lambda bi: (bi, 0, P_BQ // 512))
        ks = pl.BlockSpec((1, lc, 128), lambda bi: (bi, 0, P_BK // 128))
        vs = pl.BlockSpec((1, lc, 128), lambda bi: (bi, 0, P_BV // 128))
    else:
        qs = pl.BlockSpec((1, lc, 512), lambda bi: (bi, 0, P_AQ // 512))
        ks = pl.BlockSpec((1, lc, 512), lambda bi: (bi, 0, P_AK // 512))
        vs = pl.BlockSpec((1, lc, 512), lambda bi: (bi, 0, P_AV // 512))
    return pl.pallas_call(
        functools.partial(_ctx_attn_kernel, swa=swa),
        grid=(b,),
        in_specs=[qs, ks, vs, pl.BlockSpec((8, 128), lambda bi: (0, 0))],
        out_specs=pl.BlockSpec((1, lc, 512), lambda bi: (bi, 0, 0)),
        out_shape=jax.ShapeDtypeStruct((b, lc, 512), BF16),
        compiler_params=_cparams(("parallel",)),
        name="ctx_attn_swa" if swa else "ctx_attn_na",
    )(proj_c, proj_c, proj_c, sink128)


def _natten_kernel(q_ref, k_ref, v_ref, kc_ref, vc_ref, bias_ref, o_ref, *, rows, rows_per_step):
    blk = pl.program_id(2)
    kc = kc_ref[0]
    vc = vc_ref[0]
    w = GRID_W
    nkeys = NA_KR * w
    lo_all = lax.broadcasted_iota(jnp.int32, (rows_per_step * w, LANES), 1) < 64
    lo_row = lax.broadcasted_iota(jnp.int32, (w, LANES), 1) < 64
    q_all = (q_ref[0].astype(F32) * (HEAD_DIM ** -0.5 * LOG2E)).astype(BF16)
    zero = jnp.zeros_like(q_all)
    q_lo = jnp.where(lo_all, q_all, zero)
    q_hi = jnp.where(lo_all, zero, q_all)
    sc_lo = _dot_nt(q_lo, kc)
    sc_hi = _dot_nt(q_hi, kc)
    o_rows, pc_lo, pc_hi, rd_lo, rd_hi = [], [], [], [], []

    def scores(rr):
        r = blk * rows_per_step + rr
        rs = jnp.clip(r - NA_KR // 2, 0, rows - NA_KR)
        off = rs - r + (NA_KR - 1)
        k0 = pl.multiple_of(rs * w, w)
        sl = slice(rr * w, (rr + 1) * w)
        qq = jnp.concatenate([q_lo[sl], q_hi[sl]], axis=0)
        s = _dot_nt(qq, k_ref[0, pl.ds(k0, nkeys), :]) + bias_ref[off, 0]
        sc = jnp.concatenate([sc_lo[sl], sc_hi[sl]], axis=0)
        return s, sc, k0

    ahead = 2
    queue = [scores(rr) for rr in range(ahead)]
    for rr in range(rows_per_step):
        s, sc, k0 = queue.pop(0)
        if rr + ahead < rows_per_step:
            queue.append(scores(rr + ahead))
        vrows = v_ref[0, pl.ds(k0, nkeys), :]
        m = jnp.maximum(jnp.max(s, axis=-1, keepdims=True), jnp.max(sc, axis=-1, keepdims=True))
        p = jnp.exp2(s - m)
        pc = jnp.exp2(sc - m)
        rden = 1.0 / (jnp.sum(p, axis=-1, keepdims=True) + jnp.sum(pc, axis=-1, keepdims=True))
        o = _dot(p.astype(BF16), vrows)
        pcb = pc.astype(BF16)
        o_rows.append(jnp.where(lo_row, o[:w], o[w:]))
        rd_lo.append(rden[:w])
        rd_hi.append(rden[w:])
        pc_lo.append(pcb[:w])
        pc_hi.append(pcb[w:])
    oc = jnp.where(lo_all, _dot(jnp.concatenate(pc_lo, axis=0), vc), _dot(jnp.concatenate(pc_hi, axis=0), vc))
    rd = jnp.where(lo_all, jnp.concatenate(rd_lo, axis=0), jnp.concatenate(rd_hi, axis=0))
    o_ref[0] = ((jnp.concatenate(o_rows, axis=0) + oc) * rd).astype(BF16)


def _natten(proj_l, proj_c, bias_tab):
    b, length, _ = proj_l.shape
    lc = proj_c.shape[1]
    rows = length // GRID_W
    assert rows >= NA_KR
    rps = 16 if rows % 16 == 0 else 8
    tq = rps * GRID_W
    nk = NA_KR * GRID_W
    return pl.pallas_call(
        functools.partial(_natten_kernel, rows=rows, rows_per_step=rps),
        grid=(b, 4, rows // rps),
        in_specs=[pl.BlockSpec((1, tq, 128), lambda bi, hp, i: (bi, i, P_AQ // 128 + hp)),
                  pl.BlockSpec((1, length, 128), lambda bi, hp, i: (bi, 0, P_AK // 128 + hp)),
                  pl.BlockSpec((1, length, 128), lambda bi, hp, i: (bi, 0, P_AV // 128 + hp)),
                  pl.BlockSpec((1, lc, 128), lambda bi, hp, i: (bi, 0, P_AK // 128 + hp)),
                  pl.BlockSpec((1, lc, 128), lambda bi, hp, i: (bi, 0, P_AV // 128 + hp)),
                  pl.BlockSpec((NA_KR, 1, 2 * GRID_W, nk), lambda bi, hp, i: (0, hp, 0, 0))],
        out_specs=pl.BlockSpec((1, tq, 128), lambda bi, hp, i: (bi, i, hp)),
        out_shape=jax.ShapeDtypeStruct((b, length, 512), BF16),
        compiler_params=_cparams(("parallel", "parallel", "arbitrary")),
        name="natten",
    )(proj_l, proj_l, proj_l, proj_c, proj_c, bias_tab)


def _natten_bias_table(rpb):
    w = GRID_W
    nd = 2 * NA_KC - 1
    qc = np.arange(w)[:, None]
    kcol = np.arange(w)[None, :]
    cs = np.clip(qc - NA_KC // 2, 0, w - NA_KC)
    cmask = (kcol >= cs) & (kcol < cs + NA_KC)
    period = nd + w
    padded = jnp.pad(rpb.astype(F32), ((0, 0), (0, 0), (0, w)))
    flat = jnp.tile(padded, (1, 1, w))[:, :, :w * (period - 1)]
    t = flat.reshape(NA_HEADS, 2 * NA_KR - 1, w, period - 1)[..., NA_KC - 1:NA_KC - 1 + w]
    g = jnp.stack([t[:, off:off + NA_KR] for off in range(NA_KR)])
    g = jnp.where(cmask[None, None, None], g * LOG2E, NEG_INF)
    return g.transpose(0, 1, 3, 2, 4).reshape(NA_KR, NA_HEADS // 2, 2 * w, NA_KR * w)


def _gla_kernel(*refs, rev, ncb, final):
    if final:
        (q_ref, k_ref, v_ref, lr_ref, whi_ref, wlo_ref, b_ref, s0_ref, of_ref, gnw_ref,
         o_ref, sfin_ref, st_ref) = refs
    else:
        (q_ref, k_ref, v_ref, lr_ref, whi_ref, wlo_ref, b_ref, s0_ref,
         o_ref, sfin_ref, st_ref) = refs
    ch = GLA_CHUNK
    nstream = q_ref.shape[0]

    @pl.when(pl.program_id(1) == 0)
    def _():
        st_ref[...] = s0_ref[:, 0]

    tb = ncb * ch
    rix = lax.broadcasted_iota(jnp.int32, (ch, ch), 0)
    cix = lax.broadcasted_iota(jnp.int32, (ch, ch), 1)
    tri = (rix <= cix) if rev else (rix >= cix)
    lo_lanes = lax.broadcasted_iota(jnp.int32, (tb, LANES), 1) < 64
    end = 0 if rev else ch - 1
    pos = lax.broadcasted_iota(jnp.int32, (tb, 1), 0) % ch
    order = [(ncb - 1 - cc) if rev else cc for cc in range(ncb)]

    def prefix(s):
        lr = lr_ref[s]
        arg = _dot(lr, whi_ref[0]) + _dot(lr, wlo_ref[0]) + b_ref[0]
        cum = (jnp.minimum(arg, 0.0) - jnp.log(1.0 + jnp.exp(-jnp.abs(arg)))) * (1.0 / GLA_TAU)
        step = 1
        while step < ch:
            if rev:
                cum = cum + jnp.where(pos < ch - step, pltpu.roll(cum, tb - step, 0), 0.0)
            else:
                cum = cum + jnp.where(pos >= step, pltpu.roll(cum, step, 0), 0.0)
            step *= 2
        q = q_ref[s].astype(F32)
        k = k_ref[s].astype(F32)
        qd = (q * (HEAD_DIM ** -0.5) * jnp.exp(cum)).astype(BF16)
        kd = (k * jnp.exp(-cum)).astype(BF16)
        zero = jnp.zeros_like(qd)
        return cum, k, kd, (jnp.where(lo_lanes, qd, zero), jnp.where(lo_lanes, zero, qd))

    def intra_chunks(s, pre):
        cum, k, kd, qd_h = pre
        res = {}
        for c in order:
            sl = slice(c * ch, (c + 1) * ch)
            cum_c = cum[sl]
            cum_end = cum_c[end:end + 1, :]
            kdec = (k[sl] * jnp.exp(cum_end - cum_c)).astype(BF16)
            for h in range(2):
                vh = v_ref[s, sl, 128 * h:128 * (h + 1)]
                a = jnp.where(tri, _dot_nt(qd_h[h][sl], kd[sl]), 0.0)
                res[c, h] = (_dot(a.astype(BF16), vh), _dot_tn(vh, kdec), jnp.exp(cum_end))
        return res

    def recurrence(s, pre, res):
        qd_h = pre[3]
        st = [st_ref[s, 0], st_ref[s, 1]]
        for c in order:
            sl = slice(c * ch, (c + 1) * ch)
            for h in range(2):
                o_intra, kv_t, gdec = res[c, h]
                o = o_intra + _dot_nt(qd_h[h][sl], st[h].astype(BF16))
                st[h] = st[h] * gdec + kv_t
                if final:
                    o = o + of_ref[s, sl, 128 * h:128 * (h + 1)]
                    ms = jnp.mean(o * o, axis=-1, keepdims=True)
                    o = o * lax.rsqrt(ms + EPS) * gnw_ref[...]
                    o_ref[s, sl, 128 * h:128 * (h + 1)] = o.astype(BF16)
                else:
                    o_ref[s, sl, 128 * h:128 * (h + 1)] = o
        st_ref[s, 0] = st[0]
        st_ref[s, 1] = st[1]

    pres = [prefix(s) for s in range(nstream)]
    ress = [intra_chunks(s, pres[s]) for s in range(nstream)]
    for s in range(nstream):
        recurrence(s, pres[s], ress[s])
    sfin_ref[:, 0] = st_ref[...]


def _gla_pass(proj, w_pad_hi, w_pad_lo, b_pad, s0, rev, o_fwd=None, gnw=None):
    b, t, _ = proj.shape
    final = o_fwd is not None
    tb = min(t, 512)
    nblk = t // tb
    ncb = tb // GLA_CHUNK
    d = 1 if rev else 0

    def bi_map(i):
        return (nblk - 1 - i) if rev else i

    in_specs = [pl.BlockSpec((b, tb, 128), lambda p, i: (0, bi_map(i), P_CQ // 128 + p)),
                pl.BlockSpec((b, tb, 128), lambda p, i: (0, bi_map(i), P_CK // 128 + p)),
                pl.BlockSpec((b, tb, 256), lambda p, i: (0, bi_map(i), P_CV // 256 + p)),
                pl.BlockSpec((b, tb, 128), lambda p, i: (0, bi_map(i), P_LR // 128)),
                pl.BlockSpec((1, 128, 128), lambda p, i: (2 * d + p, 0, 0)),
                pl.BlockSpec((1, 128, 128), lambda p, i: (2 * d + p, 0, 0)),
                pl.BlockSpec((1, 1, 128), lambda p, i: (2 * d + p, 0, 0)),
                pl.BlockSpec((b, 1, 2, 128, 128), lambda p, i: (0, p, 0, 0, 0))]
    args = [proj, proj, proj, proj, w_pad_hi, w_pad_lo, b_pad, s0]
    if final:
        in_specs += [pl.BlockSpec((b, tb, 256), lambda p, i: (0, bi_map(i), p)),
                     pl.BlockSpec((1, 128), lambda p, i: (0, 0))]
        args += [o_fwd, gnw.reshape(1, 128)]
    out, s_fin = pl.pallas_call(
        functools.partial(_gla_kernel, rev=rev, ncb=ncb, final=final),
        grid=(2, nblk),
        in_specs=in_specs,
        out_specs=[pl.BlockSpec((b, tb, 256), lambda p, i: (0, bi_map(i), p)),
                   pl.BlockSpec((b, 1, 2, 128, 128), lambda p, i: (0, p, 0, 0, 0))],
        out_shape=[jax.ShapeDtypeStruct((b, t, 512), BF16 if final else F32),
                   jax.ShapeDtypeStruct((b, 2, 2, 128, 128), F32)],
        scratch_shapes=[pltpu.VMEM((b, 2, 128, 128), F32)],
        compiler_params=_cparams(("parallel", "arbitrary")),
        name="gla_rev" if rev else "gla_fwd",
    )(*args)
    return out, s_fin


def _gla_decay_weights(w_alpha_up, b_alpha):
    w = jnp.zeros((2, 2, 128, 128), F32)
    for d in range(2):
        for p in range(2):
            w = w.at[d, p, 16 * d:16 * d + 16, :].set(w_alpha_up[d][:, 128 * p:128 * (p + 1)].astype(F32))
    w = w.reshape(4, 128, 128)
    hi = w.astype(BF16)
    lo = (w - hi.astype(F32)).astype(BF16)
    return hi, lo, b_alpha.astype(F32).reshape(4, 1, 128)


def _hy_pre_kernel(u0, u1, u2, p0, p1, p2, n0, n1, n2, w_ref, b_ref, x0_ref, z_ref, *, nblk):
    i = pl.program_id(1)
    tm = u0.shape[1]
    row = lax.broadcasted_iota(jnp.int32, (tm, 1), 0)
    has_prev = (i > 0).astype(F32)
    has_next = (i < nblk - 1).astype(F32)

    def conv(u_ref, p_ref, n_ref, j):
        u = u_ref[0].astype(F32)
        prev = p_ref[0, 7:8, :].astype(F32) * has_prev
        nxt = n_ref[0, 0:1, :].astype(F32) * has_next
        up = jnp.where(row == 0, prev, pltpu.roll(u, 1, 0))
        dn = jnp.where(row == tm - 1, nxt, pltpu.roll(u, tm - 1, 0))
        w = w_ref[:, 512 * j:512 * (j + 1)]
        return up * w[0:1] + u * w[1:2] + dn * w[2:3] + b_ref[:, 512 * j:512 * (j + 1)]

    x0_ref[0] = conv(u0, p0, n0, 0).astype(BF16)
    z_ref[0] = (conv(u1, p1, n1, 1) * conv(u2, p2, n2, 2)).astype(BF16)


def _hy_pre(proj, conv_w, conv_b):
    b, t, _ = proj.shape
    tm = min(t, 512)
    nblk = t // tm
    hb = tm // 8
    nrb = t // 8
    c0 = P_DU // 512
    main = [pl.BlockSpec((1, tm, 512), functools.partial(lambda bi, i, j: (bi, i, c0 + j), j=j)) for j in range(3)]
    prev = [pl.BlockSpec((1, 8, 512), functools.partial(lambda bi, i, j: (bi, jnp.maximum(i * hb - 1, 0), c0 + j), j=j))
            for j in range(3)]
    nxt = [pl.BlockSpec((1, 8, 512),
                        functools.partial(lambda bi, i, j: (bi, jnp.minimum((i + 1) * hb, nrb - 1), c0 + j), j=j))
           for j in range(3)]
    return pl.pallas_call(
        functools.partial(_hy_pre_kernel, nblk=nblk),
        grid=(b, nblk),
        in_specs=main + prev + nxt + [pl.BlockSpec((3, 1536), lambda bi, i: (0, 0)),
                                      pl.BlockSpec((1, 1536), lambda bi, i: (0, 0))],
        out_specs=[pl.BlockSpec((1, tm, 512), lambda bi, i: (bi, i, 0))] * 2,
        out_shape=[jax.ShapeDtypeStruct((b, t, 512), BF16)] * 2,
        compiler_params=_cparams(("parallel", "parallel")),
        name="hyena_pre",
    )(*([proj] * 9), conv_w.astype(F32), conv_b.astype(F32).reshape(1, 1536))


def _hy_filter_kernel(z_ref, w1, b1, w2, b2, w3, b3, fr, wo, dl, h_ref, l1_ref, *, tm, t_len):
    i = pl.program_id(0)
    f = fr[...]
    half = tm // 2
    z = z_ref[...]
    hh = jnp.concatenate([z[:half], z[half:]], axis=1)
    hh = jnp.sin(f * (_dot_x3(hh, w1[...]) + b1[...]))
    hh = jnp.sin(f * (_dot_x3(hh, w2[...]) + b2[...]))
    hh = jnp.sin(f * (_dot_x3(hh, w3[...]) + b3[...]))
    hh = _dot_x3(hh, wo[...])
    hh = jnp.concatenate([hh[:, :2 * HY_WIDTH], hh[:, 2 * HY_WIDTH:]], axis=0)
    row = i * tm + lax.broadcasted_iota(jnp.int32, (tm, 1), 0)
    t = row.astype(F32) / (t_len - 1)
    decay = jnp.exp(-t * dl[...])
    h_f = hh[:, :HY_WIDTH] * decay
    h_b = jnp.where(row == 0, 0.0, hh[:, HY_WIDTH:] * decay)
    h_ref[0] = h_f.astype(BF16)
    h_ref[1] = h_b.astype(BF16)
    part = jnp.sum(jnp.abs(h_f) + jnp.abs(h_b), axis=0, keepdims=True)

    @pl.when(i == 0)
    def _():
        l1_ref[...] = part

    @pl.when(i > 0)
    def _():
        l1_ref[...] = l1_ref[...] + part


def _hy_filter(t_len, w1, b1, w2, b2, w3, b3, freq, wout):
    tm = min(t_len, 512)
    t = jnp.linspace(0.0, 1.0, t_len, dtype=F32)[:, None]
    bands = (HY_EMB - 1) // 2
    w_ang = 2.0 * math.pi * jnp.arange(t_len, dtype=F32)[:, None] / t_len
    f = jnp.linspace(1e-4, bands - 1, bands, dtype=F32)[None, :]
    z = jnp.concatenate([t, jnp.cos(f * w_ang), -jnp.sin(f * w_ang),
                         jnp.zeros((t_len, LANES - HY_EMB), F32)], axis=-1)
    w1p = jnp.concatenate([w1.astype(F32), jnp.zeros((LANES - HY_EMB, HY_FFN), F32)], axis=0)
    deltas = jnp.abs(jnp.linspace(math.log(HY_TARGET) / HY_SLOW_PCT, math.log(HY_TARGET) / HY_FAST_PCT,
                                  HY_WIDTH, dtype=F32))[None, :]
    full = lambda shape: pl.BlockSpec(shape, lambda i: tuple(0 for _ in shape))
    r = lambda v: jnp.tile(v.astype(F32).reshape(1, -1), (1, 2))

    def bdiag(w):
        w = w.astype(F32)
        zero = jnp.zeros_like(w)
        return jnp.concatenate([jnp.concatenate([w, zero], axis=1), jnp.concatenate([zero, w], axis=1)], axis=0)

    hid = 2 * HY_FFN
    return pl.pallas_call(
        functools.partial(_hy_filter_kernel, tm=tm, t_len=t_len),
        grid=(t_len // tm,),
        in_specs=[pl.BlockSpec((tm, LANES), lambda i: (i, 0)),
                  full((2 * LANES, hid)), full((1, hid)), full((hid, hid)), full((1, hid)),
                  full((hid, hid)), full((1, hid)), full((1, hid)), full((hid, 4 * HY_WIDTH)),
                  full((1, HY_WIDTH))],
        out_specs=[pl.BlockSpec((2, tm, HY_WIDTH), lambda i: (0, i, 0)),
                   pl.BlockSpec((1, HY_WIDTH), lambda i: (0, 0))],
        out_shape=[jax.ShapeDtypeStruct((2, t_len, HY_WIDTH), BF16), jax.ShapeDtypeStruct((1, HY_WIDTH), F32)],
        compiler_params=_cparams(("arbitrary",)),
        name="hyena_filter",
    )(z, bdiag(w1p), r(b1), bdiag(w2), r(b2), bdiag(w3), r(b3), r(freq), bdiag(wout), deltas)


BF16_TILE_ROWS = 16
FFT_K1_PER_STEP = 4


def _used_k1(n1):
    need = n1 // 2 + 1
    return min(n1, -(-need // BF16_TILE_ROWS) * BF16_TILE_ROWS)


def _dft_tables(t_len):
    n = 2 * t_len
    n2 = FFT_N2
    n1 = n // n2
    half = n1 // 2
    k1u = _used_k1(n1)
    k1 = np.arange(k1u)[:, None]
    j1 = np.arange(half)[None, :]
    ang1 = 2.0 * np.pi * ((k1 * j1) % n1) / n1
    f1 = np.concatenate([np.cos(ang1), -np.sin(ang1)], axis=0)
    wgt = np.where((k1 == 0) | (k1 == half), 1.0, np.where(k1 < half, 2.0, 0.0))
    i1 = np.concatenate([(wgt * np.cos(ang1)).T, (-wgt * np.sin(ang1)).T], axis=1)
    n1 = k1u
    a = np.arange(n2)
    ang2 = 2.0 * np.pi * ((a[:, None] * a[None, :]) % n2) / n2
    fr, fi = np.cos(ang2), -np.sin(ang2)
    fwd = np.block([[fr, -fi], [fi, fr]])
    inv = np.block([[fr, fi], [-fi, fr]])
    angt = 2.0 * np.pi * ((np.arange(n1)[:, None] * a[None, :]) % n) / n
    twc = np.cos(angt)[:, :, None]
    tws = np.sin(angt)[:, :, None]
    return (jnp.asarray(f1, F32).astype(BF16), jnp.asarray(i1, F32).astype(BF16), jnp.asarray(fwd, F32).astype(BF16),
            jnp.asarray(inv, F32).astype(BF16),
            jnp.broadcast_to(jnp.asarray(twc, F32), (n1, n2, LANES)),
            jnp.broadcast_to(jnp.asarray(tws, F32), (n1, n2, LANES)))


def _hy_s1_kernel(z_ref, f_ref, xr_ref, xi_ref):
    n1 = xr_ref.shape[1]
    y = _dot(f_ref[...], z_ref[0])
    xr_ref[0] = y[:n1].astype(BF16)
    xi_ref[0] = y[n1:].astype(BF16)


def _hy_s1(zv, f1):
    b, half, cols = zv.shape
    n1 = f1.shape[0] // 2
    tn = min(cols, 4096)
    return pl.pallas_call(
        _hy_s1_kernel,
        grid=(b, cols // tn),
        in_specs=[pl.BlockSpec((1, half, tn), lambda bi, j: (bi, 0, j)),
                  pl.BlockSpec((2 * n1, half), lambda bi, j: (0, 0))],
        out_specs=[pl.BlockSpec((1, n1, tn), lambda bi, j: (bi, 0, j))] * 2,
        out_shape=[jax.ShapeDtypeStruct((b, n1, cols), BF16)] * 2,
        compiler_params=_cparams(("parallel", "parallel")),
        name="hyena_dft_outer",
    )(zv, f1)


def _twiddle_fwd(xr, xi, c, s):
    return xr * c + xi * s, xi * c - xr * s


def _lane_tile(ref, j, reps):
    v = ref[j]
    return jnp.concatenate([v] * reps, axis=1)


def _hy_filt_spec_kernel(xr_ref, xi_ref, twc_ref, tws_ref, fwd_ref, l1_ref, kr_ref, ki_ref, *, n):
    reps = xr_ref.shape[-1] // LANES
    n2 = FFT_N2
    scale = 1.0 / (l1_ref[...] * n)
    for j in range(FFT_K1_PER_STEP):
        c = _lane_tile(twc_ref, j, reps)
        s = _lane_tile(tws_ref, j, reps)
        parts = []
        for f in range(2):
            ar, ai = _twiddle_fwd(xr_ref[f, j].astype(F32), xi_ref[f, j].astype(F32), c, s)
            y = _dot(fwd_ref[...], jnp.concatenate([ar, ai], axis=0).astype(BF16))
            parts.append((y[:n2], y[n2:]))
        kr_ref[j] = ((parts[0][0] + parts[1][0]) * scale).astype(BF16)
        ki_ref[j] = ((parts[0][1] - parts[1][1]) * scale).astype(BF16)


def _hy_filt_spec(xr, xi, twc, tws, fwd, l1, n):
    _, n1, n2, w = xr.shape
    kb = FFT_K1_PER_STEP
    return pl.pallas_call(
        functools.partial(_hy_filt_spec_kernel, n=n),
        grid=(n1 // kb,),
        in_specs=[pl.BlockSpec((2, kb, n2, w), lambda k: (0, k, 0, 0)),
                  pl.BlockSpec((2, kb, n2, w), lambda k: (0, k, 0, 0)),
                  pl.BlockSpec((kb, n2, LANES), lambda k: (k, 0, 0)),
                  pl.BlockSpec((kb, n2, LANES), lambda k: (k, 0, 0)),
                  pl.BlockSpec((2 * n2, 2 * n2), lambda k: (0, 0)),
                  pl.BlockSpec((1, w), lambda k: (0, 0))],
        out_specs=[pl.BlockSpec((kb, n2, w), lambda k: (k, 0, 0))] * 2,
        out_shape=[jax.ShapeDtypeStruct((n1, n2, w), BF16)] * 2,
        compiler_params=_cparams(("parallel",)),
        name="hyena_filter_spectrum",
    )(xr, xi, twc, tws, fwd, l1)


def _hy_mid_kernel(xr_ref, xi_ref, twc_ref, tws_ref, fwd_ref, inv_ref, kr_ref, ki_ref, tr_ref, ti_ref):
    reps = xr_ref.shape[-1] // LANES
    n2 = FFT_N2
    for j in range(FFT_K1_PER_STEP):
        c = _lane_tile(twc_ref, j, reps)
        s = _lane_tile(tws_ref, j, reps)
        ar, ai = _twiddle_fwd(xr_ref[0, j].astype(F32), xi_ref[0, j].astype(F32), c, s)
        y = _dot(fwd_ref[...], jnp.concatenate([ar, ai], axis=0).astype(BF16))
        yr, yi = y[:n2], y[n2:]
        kr = kr_ref[j].astype(F32)
        ki = ki_ref[j].astype(F32)
        zr = yr * kr - yi * ki
        zi = yr * ki + yi * kr
        u = _dot(inv_ref[...], jnp.concatenate([zr, zi], axis=0).astype(BF16))
        ur, ui = u[:n2], u[n2:]
        tr_ref[0, j] = (ur * c - ui * s).astype(BF16)
        ti_ref[0, j] = (ur * s + ui * c).astype(BF16)


def _hy_mid(xr, xi, twc, tws, fwd, inv, kr, ki):
    b, n1, n2, w = xr.shape
    kb = FFT_K1_PER_STEP
    xspec = pl.BlockSpec((1, kb, n2, w), lambda bi, k: (bi, k, 0, 0))
    return pl.pallas_call(
        _hy_mid_kernel,
        grid=(b, n1 // kb),
        in_specs=[xspec, xspec,
                  pl.BlockSpec((kb, n2, LANES), lambda bi, k: (k, 0, 0)),
                  pl.BlockSpec((kb, n2, LANES), lambda bi, k: (k, 0, 0)),
                  pl.BlockSpec((2 * n2, 2 * n2), lambda bi, k: (0, 0)),
                  pl.BlockSpec((2 * n2, 2 * n2), lambda bi, k: (0, 0)),
                  pl.BlockSpec((kb, n2, w), lambda bi, k: (k, 0, 0)),
                  pl.BlockSpec((kb, n2, w), lambda bi, k: (k, 0, 0))],
        out_specs=[xspec, xspec],
        out_shape=[jax.ShapeDtypeStruct((b, n1, n2, w), BF16)] * 2,
        compiler_params=_cparams(("parallel", "parallel")),
        name="hyena_dft_inner",
    )(xr, xi, twc, tws, fwd, inv, kr, ki)


def _hy_post_kernel(tr_ref, ti_ref, i1_ref, x0_ref, z_ref, fb_ref, o_ref):
    t = jnp.concatenate([tr_ref[0], ti_ref[0]], axis=0)
    y = _dot(i1_ref[...], t)
    z = z_ref[0].astype(F32)
    o_ref[0] = (x0_ref[0].astype(F32) * (y + z * fb_ref[...])).astype(BF16)


def _hy_post(tr, ti, i1, x0v, zv, fb_tiled):
    b, n1, cols = tr.shape
    half = i1.shape[0]
    tn = fb_tiled.shape[1]
    hspec = pl.BlockSpec((1, half, tn), lambda bi, j: (bi, 0, j))
    tspec = pl.BlockSpec((1, n1, tn), lambda bi, j: (bi, 0, j))
    return pl.pallas_call(
        _hy_post_kernel,
        grid=(b, cols // tn),
        in_specs=[tspec, tspec, pl.BlockSpec((half, 2 * n1), lambda bi, j: (0, 0)), hspec, hspec,
                  pl.BlockSpec((1, tn), lambda bi, j: (0, 0))],
        out_specs=hspec,
        out_shape=jax.ShapeDtypeStruct((b, half, cols), BF16),
        compiler_params=_cparams(("parallel", "parallel")),
        name="hyena_dft_outer_inv",
    )(tr, ti, i1, x0v, zv, fb_tiled)


def _hyena_long(proj, conv_w, conv_b, filt, filt_bias):
    b, t, _ = proj.shape
    w = HY_WIDTH
    n = 2 * t
    n2 = FFT_N2
    n1 = n // n2
    k1u = _used_k1(n1)
    f1, i1, fwd, inv, twc, tws = _dft_tables(t)
    hfb, l1 = _hy_filter(t, *filt)
    hr, hi = _hy_s1(hfb.reshape(2, n1 // 2, n2 * w), f1)
    kr, ki = _hy_filt_spec(hr.reshape(2, k1u, n2, w), hi.reshape(2, k1u, n2, w), twc, tws, fwd, l1, n)
    x0, z = _hy_pre(proj, conv_w, conv_b)
    zv = z.reshape(b, n1 // 2, n2 * w)
    xr, xi = _hy_s1(zv, f1)
    tr, ti = _hy_mid(xr.reshape(b, k1u, n2, w), xi.reshape(b, k1u, n2, w), twc, tws, fwd, inv, kr, ki)
    tn = min(n2 * w, 4096)
    fb_tiled = jnp.tile(filt_bias.astype(F32).reshape(1, w), (1, tn // w))
    d = _hy_post(tr.reshape(b, k1u, n2 * w), ti.reshape(b, k1u, n2 * w), i1, x0.reshape(b, n1 // 2, n2 * w), zv,
                 fb_tiled)
    return d.reshape(b, t, w)


def _hy_ctx_kernel(x0_ref, z_ref, h_ref, l1_ref, cm_ref, sm_ref, cmt_ref, smt_ref, fb_ref, o_ref, *, n):
    cm = cm_ref[...]
    sm = sm_ref[...]

    def spectrum(v):
        return _dot(cm, v), -_dot(sm, v)

    z = z_ref[0]
    zr, zi = spectrum(z)
    ar, ai = spectrum(h_ref[0])
    br, bi = spectrum(h_ref[1])
    scale = 1.0 / (l1_ref[...] * n)
    kr = (ar + br) * scale
    ki = (ai - bi) * scale
    yr = (zr * kr - zi * ki).astype(BF16)
    yi = (zr * ki + zi * kr).astype(BF16)
    y = _dot(cmt_ref[...], yr) - _dot(smt_ref[...], yi)
    o_ref[0] = (x0_ref[0].astype(F32) * (y + z.astype(F32) * fb_ref[...])).astype(BF16)


def _hyena_short(proj, conv_w, conv_b, filt, filt_bias):
    b, t, _ = proj.shape
    w = HY_WIDTH
    n = 2 * t
    hfb, l1 = _hy_filter(t, *filt)
    x0, z = _hy_pre(proj, conv_w, conv_b)
    k = np.arange(n)[:, None]
    j = np.arange(t)[None, :]
    ang = 2.0 * np.pi * ((k * j) % n) / n
    cm, sm = np.cos(ang), np.sin(ang)
    tabs = [jnp.asarray(a, F32).astype(BF16) for a in (cm, sm, cm.T, sm.T)]
    full2 = lambda shape: pl.BlockSpec(shape, lambda bi: (0, 0))
    bspec = pl.BlockSpec((1, t, w), lambda bi: (bi, 0, 0))
    return pl.pallas_call(
        functools.partial(_hy_ctx_kernel, n=n),
        grid=(b,),
        in_specs=[bspec, bspec, pl.BlockSpec((2, t, w), lambda bi: (0, 0, 0)), full2((1, w)),
                  full2((n, t)), full2((n, t)), full2((t, n)), full2((t, n)), full2((1, w))],
        out_specs=bspec,
        out_shape=jax.ShapeDtypeStruct((b, t, w), BF16),
        compiler_params=_cparams(("parallel",)),
        name="hyena_ctx",
    )(x0, z, hfb, l1, *tabs, filt_bias.astype(F32).reshape(1, w))


def _layer(xc, xl, mod, cos128, sin128, norm_w, w_in, rpb, sink, w_alpha_up, b_alpha, gla_norm_w, conv_w, conv_b,
           filt, filt_bias, w_branch, w_out, final_w, with_ctx_out):
    b, length, d = xl.shape
    mod_l = mod[:b].reshape(b, 1, 3 * d)
    mod_c = jnp.broadcast_to(mod[b].reshape(1, 1, 3 * d), (b, 1, 3 * d))
    sh_l, sc_l, g_l = mod_l[..., :d], mod_l[..., d:2 * d], mod_l[..., 2 * d:]
    sh_c, sc_c, g_c = mod_c[..., :d], mod_c[..., d:2 * d], mod_c[..., 2 * d:]

    w_perm = _permute(w_in.astype(BF16), _IN_PERM, axis=1) * jnp.asarray(_HALF_COLS, BF16)[None, :]
    wbf = w_branch.astype(BF16) * 0.5
    wb = jnp.stack([wbf[0], _permute(wbf[1], _SWA_OUT_PERM, axis=0), wbf[2], wbf[3]])
    wo = w_out.astype(BF16)

    proj_l = _inproj(xl, norm_w, sc_l, sh_l, w_perm)
    proj_c = _inproj(xc, norm_w, sc_c, sh_c, w_perm)

    y_a = _natten(proj_l, proj_c, _natten_bias_table(rpb))
    sink128 = jnp.broadcast_to(sink.astype(F32)[:, None], (SW_HEADS, LANES))
    qr, kr = _rope(proj_l, cos128, sin128)
    y_b = _swa(qr, kr, proj_l, proj_c, proj_c, sink128 * LOG2E)
    whi, wlo, bpad = _gla_decay_weights(w_alpha_up, b_alpha)
    s_zero = jnp.zeros((b, 2, 2, 128, 128), F32)
    of_c, s_cf = _gla_pass(proj_c, whi, wlo, bpad, s_zero, rev=False)
    y_cc, s_cb = _gla_pass(proj_c, whi, wlo, bpad, s_zero, rev=True, o_fwd=of_c, gnw=gla_norm_w.astype(F32))
    of_l, _ = _gla_pass(proj_l, whi, wlo, bpad, s_cf, rev=False)
    y_c, _ = _gla_pass(proj_l, whi, wlo, bpad, s_cb, rev=True, o_fwd=of_l, gnw=gla_norm_w.astype(F32))
    y_d = _hyena_long(proj_l, conv_w, conv_b, filt, filt_bias)

    xl_new = _merge([y_a, y_b, y_c, y_d], proj_l, wb, wo, g_l, xl, final_w, final=not with_ctx_out)
    if with_ctx_out:
        ya_c = _ctx_attn(proj_c, sink128, swa=False)
        yb_c = _ctx_attn(proj_c, sink128, swa=True)
        yd_c = _hyena_short(proj_c, conv_w, conv_b, filt, filt_bias)
        xc = _merge([ya_c, yb_c, y_cc, yd_c], proj_c, wb, wo, g_c, xc, final_w, final=False)
    return xc, xl_new


def kernel(x, c, ctx, c_ctx, norm_w, w_mod, b_mod, w_in, rpb, sink, w_alpha_up, b_alpha, gla_norm_w, conv_w, conv_b,
           filt_w1, filt_b1, filt_w2, filt_b2, filt_w3, filt_b3, filt_freq, filt_wout, filt_bias, w_branch, w_out,
           final_norm_w):
    b, length, d = x.shape
    depth = norm_w.shape[0]
    cvec = jnp.zeros((8, d), F32).at[:b].set(c.astype(F32)).at[b].set(c_ctx.astype(F32))
    mod = _modulation(cvec, w_mod.astype(F32), b_mod.astype(F32))
    cos128, sin128 = _rope_tables(length)
    xc, xl = ctx, x
    for i in range(depth):
        filt = (filt_w1[i], filt_b1[i], filt_w2[i], filt_b2[i], filt_w3[i], filt_b3[i], filt_freq[i], filt_wout[i])
        xc, xl = _layer(xc, xl, mod[i], cos128, sin128, norm_w[i], w_in[i], rpb[i], sink[i], w_alpha_up[i],
                        b_alpha[i], gla_norm_w[i], conv_w[i], conv_b[i], filt, filt_bias[i], w_branch[i], w_out[i],
                        final_norm_w, with_ctx_out=(i < depth - 1))
    return xl
```

```python
import functools
import math

import numpy as np
import jax
import jax.numpy as jnp
from jax import lax
from jax.experimental import pallas as pl
from jax.experimental.pallas import tpu as pltpu

F32 = jnp.float32
BF16 = jnp.bfloat16

D_MODEL = 1024
GRID_W = 64
HEAD_DIM = 64
BRANCH_WIDTH = D_MODEL // 2
N_BRANCH = 4
NA_HEADS = 8
NA_KR = 8
NA_KC = 16
SW_HEADS = 8
SW_KV_HEADS = 2
SW_WINDOW = 128
SW_BLOCK = 128
GLA_HEADS = 4
GLA_DK = BRANCH_WIDTH // 2
GLA_DV = BRANCH_WIDTH
GLA_RANK = 16
GLA_TAU = 16.0
GLA_CHUNK = 64
HY_WIDTH = BRANCH_WIDTH
HY_EMB = 33
HY_FFN = 64
HY_TARGET = 1e-2
HY_FAST_PCT = 0.3
HY_SLOW_PCT = 1.5
ROPE_BASE = 10000.0
EPS = 1e-6
NEG_INF = -1e30
LOG2E = math.log2(math.e)
LANES = 128
FFT_N2 = 128
VMEM_LIMIT = 56 * 1024 * 1024

_IN_WIDTHS = (512, 512, 512, 512, 512, 128, 128, 512, 256, 256, 512, 16, 16, 512, 1536, 512, 4096)
_IN_OFF = np.concatenate([[0], np.cumsum(_IN_WIDTHS)])
(_O_AQ, _O_AK, _O_AV, _O_AG, _O_BQ, _O_BK, _O_BV, _O_BG, _O_CQ, _O_CK, _O_CV, _O_LRF, _O_LRB, _O_CG,
 _O_DU, _O_DG, _O_GM) = [int(v) for v in _IN_OFF[:-1]]
IN_TOTAL = int(_IN_OFF[-1])

P_AQ, P_AK, P_AV, P_AG = 0, 512, 1024, 1536
P_BQ, P_BG = 2048, 2560
P_CV, P_CG = 3072, 3584
P_DU, P_DG = 4096, 5632
P_GM = 6144
P_CQ, P_CK = 10240, 10496
P_BK, P_BV = 10752, 10880
P_LR = 11008
NP_COLS = 11264


def _swa_q_perm():
    idx = np.zeros(512, np.int64)
    for t in range(4):
        a, b = t, t + 4
        base = 128 * t
        idx[base + 0:base + 32] = 64 * a + np.arange(32)
        idx[base + 32:base + 64] = 64 * b + np.arange(32)
        idx[base + 64:base + 96] = 64 * a + 32 + np.arange(32)
        idx[base + 96:base + 128] = 64 * b + 32 + np.arange(32)
    return idx


def _swa_k_perm():
    idx = np.zeros(128, np.int64)
    idx[0:32] = np.arange(32)
    idx[32:64] = 64 + np.arange(32)
    idx[64:96] = 32 + np.arange(32)
    idx[96:128] = 96 + np.arange(32)
    return idx


def _swa_out_perm():
    idx = np.zeros(512, np.int64)
    for t in range(4):
        idx[128 * t:128 * t + 64] = 64 * t + np.arange(64)
        idx[128 * t + 64:128 * t + 128] = 64 * (t + 4) + np.arange(64)
    return idx


def _build_in_perm():
    perm = np.full(NP_COLS, -1, np.int64)

    def put(p, o, w):
        perm[p:p + w] = o + np.arange(w)

    put(P_AQ, _O_AQ, 512); put(P_AK, _O_AK, 512); put(P_AV, _O_AV, 512); put(P_AG, _O_AG, 512)
    perm[P_BQ:P_BQ + 512] = _O_BQ + _swa_q_perm()
    perm[P_BG:P_BG + 512] = _O_BG + _swa_out_perm()
    put(P_CV, _O_CV, 512); put(P_CG, _O_CG, 512)
    put(P_DU, _O_DU, 1536); put(P_DG, _O_DG, 512)
    put(P_GM, _O_GM, 4096)
    put(P_CQ, _O_CQ, 256); put(P_CK, _O_CK, 256)
    perm[P_BK:P_BK + 128] = _O_BK + _swa_k_perm()
    put(P_BV, _O_BV, 128)
    put(P_LR, _O_LRF, 16); put(P_LR + 16, _O_LRB, 16)
    return perm


def _build_half_cols():
    s = np.ones(NP_COLS, np.float32)
    for p, w in ((P_AG, 512), (P_BG, 512), (P_CG, 512), (P_DG, 512), (P_GM, 4096)):
        s[p:p + w] = 0.5
    return s


_HALF_COLS = _build_half_cols()
_IN_PERM = _build_in_perm()
_SWA_OUT_PERM = _swa_out_perm()


def _permute(w, perm, axis):
    pieces = []
    i = 0
    n = len(perm)
    while i < n:
        j = i + 1
        if perm[i] < 0:
            while j < n and perm[j] < 0:
                j += 1
            shape = list(w.shape)
            shape[axis] = j - i
            pieces.append(jnp.zeros(shape, w.dtype))
        else:
            while j < n and perm[j] == perm[j - 1] + 1:
                j += 1
            pieces.append(lax.slice_in_dim(w, int(perm[i]), int(perm[j - 1]) + 1, axis=axis))
        i = j
    return jnp.concatenate(pieces, axis=axis)


def _cparams(sem):
    return pltpu.CompilerParams(dimension_semantics=sem, vmem_limit_bytes=VMEM_LIMIT)


def _sigmoid(x):
    return 1.0 / (1.0 + jnp.exp(-x))


def _sigmoid_tanh(x):
    return 0.5 * jnp.tanh(0.5 * x) + 0.5


def _split3(a):
    hi = a.astype(BF16)
    r1 = a - hi.astype(F32)
    mid = r1.astype(BF16)
    lo = (r1 - mid.astype(F32)).astype(BF16)
    return hi, mid, lo


def _dot(a, b):
    return jnp.dot(a, b, preferred_element_type=F32)


def _dot_nt(a, b):
    return lax.dot_general(a, b, (((1,), (1,)), ((), ())), preferred_element_type=F32)


def _dot_tn(a, b):
    return lax.dot_general(a, b, (((0,), (0,)), ((), ())), preferred_element_type=F32)


def _dot_x3(a, b):
    ah, am, _ = _split3(a)
    bh, bm, _ = _split3(b)
    return _dot(ah, bh) + (_dot(ah, bm) + _dot(am, bh))


def _mod_kernel(c_ref, w_ref, b_ref, o_ref):
    c = c_ref[...]
    s = c * _sigmoid(c)
    o_ref[0] = _dot_x3(s, w_ref[0]) + b_ref[0]


def _modulation(cvec, w_mod, b_mod):
    depth, d, n = w_mod.shape
    tn = 512
    return pl.pallas_call(
        _mod_kernel,
        grid=(depth, n // tn),
        in_specs=[pl.BlockSpec((8, d), lambda l, j: (0, 0)),
                  pl.BlockSpec((1, d, tn), lambda l, j: (l, 0, j)),
                  pl.BlockSpec((1, 1, tn), lambda l, j: (l, 0, j))],
        out_specs=pl.BlockSpec((1, 8, tn), lambda l, j: (l, 0, j)),
        out_shape=jax.ShapeDtypeStruct((depth, 8, n), F32),
        compiler_params=_cparams(("parallel", "parallel")),
        name="modulation",
    )(cvec, w_mod, b_mod.reshape(depth, 1, n))


def _inproj_kernel(x_ref, nw_ref, sc_ref, sh_ref, w_ref, o_ref, h_ref):
    @pl.when(pl.program_id(2) == 0)
    def _():
        x = x_ref[0]
        ms = jnp.mean(x * x, axis=-1, keepdims=True)
        y = x * lax.rsqrt(ms + EPS) * nw_ref[...]
        h_ref[...] = (y * (1.0 + sc_ref[0]) + sh_ref[0]).astype(BF16)

    o_ref[0] = _dot(h_ref[...], w_ref[...]).astype(BF16)


def _inproj(x, norm_w, scale, shift, w_perm):
    b, t, d = x.shape
    n = w_perm.shape[1]
    tm = min(t, 2048)
    tn = 1024
    return pl.pallas_call(
        _inproj_kernel,
        grid=(b, t // tm, n // tn),
        in_specs=[pl.BlockSpec((1, tm, d), lambda bi, i, j: (bi, i, 0)),
                  pl.BlockSpec((1, d), lambda bi, i, j: (0, 0)),
                  pl.BlockSpec((1, 1, d), lambda bi, i, j: (bi, 0, 0)),
                  pl.BlockSpec((1, 1, d), lambda bi, i, j: (bi, 0, 0)),
                  pl.BlockSpec((d, tn), lambda bi, i, j: (0, j))],
        out_specs=pl.BlockSpec((1, tm, tn), lambda bi, i, j: (bi, i, j)),
        out_shape=jax.ShapeDtypeStruct((b, t, n), BF16),
        scratch_shapes=[pltpu.VMEM((tm, d), BF16)],
        compiler_params=_cparams(("parallel", "parallel", "arbitrary")),
        name="inproj",
    )(x, norm_w.reshape(1, d), scale, shift, w_perm)


def _merge_kernel(ya, yb, yc, yd, ga, gb, gc, gd, m0, m1, m2, m3, wb_ref, wo_ref, g_ref, x_ref, fw_ref,
                  o_ref, *, final):
    acc = None
    for i, (y, g, gm) in enumerate(((ya, ga, m0), (yb, gb, m1), (yc, gc, m2), (yd, gd, m3))):
        hg = g[0].astype(F32)
        yg = y[0].astype(F32) * hg
        ys = (yg + yg * jnp.tanh(hg)).astype(BF16)
        hp = _dot(ys, wb_ref[i])
        t = hp + hp * jnp.tanh(gm[0].astype(F32))
        acc = t if acc is None else acc + t
    out = x_ref[0] + g_ref[0] * _dot(acc.astype(BF16), wo_ref[...])
    if final:
        ms = jnp.mean(out * out, axis=-1, keepdims=True)
        out = out * lax.rsqrt(ms + EPS) * fw_ref[...]
    o_ref[0] = out


def _merge(ys, proj, w_branch, w_out, gate, x, final_w, final):
    b, t, d = x.shape
    bw = BRANCH_WIDTH
    tm = min(t, 512)
    yspec = pl.BlockSpec((1, tm, bw), lambda bi, i: (bi, i, 0))

    def pspec(col, width):
        blk = col // width
        return pl.BlockSpec((1, tm, width), lambda bi, i: (bi, i, blk))

    in_specs = ([yspec] * 4
                + [pspec(P_AG, bw), pspec(P_BG, bw), pspec(P_CG, bw), pspec(P_DG, bw)]
                + [pspec(P_GM + k * d, d) for k in range(N_BRANCH)]
                + [pl.BlockSpec((N_BRANCH, bw, d), lambda bi, i: (0, 0, 0)),
                   pl.BlockSpec((d, d), lambda bi, i: (0, 0)),
                   pl.BlockSpec((1, 1, d), lambda bi, i: (bi, 0, 0)),
                   pl.BlockSpec((1, tm, d), lambda bi, i: (bi, i, 0)),
                   pl.BlockSpec((1, d), lambda bi, i: (0, 0))])
    return pl.pallas_call(
        functools.partial(_merge_kernel, final=final),
        grid=(b, t // tm),
        in_specs=in_specs,
        out_specs=pl.BlockSpec((1, tm, d), lambda bi, i: (bi, i, 0)),
        out_shape=jax.ShapeDtypeStruct((b, t, d), F32),
        compiler_params=_cparams(("parallel", "parallel")),
        name="merge",
    )(*ys, *([proj] * 8), w_branch, w_out, gate, x, final_w.reshape(1, d))


def _rope_kernel(q_ref, k_ref, cos_ref, sin_ref, qo_ref, ko_ref):
    cos = cos_ref[...]
    sin = sin_ref[...]

    def rot(x):
        return x * cos + pltpu.roll(x, 64, 1) * sin

    for t in range(4):
        q = q_ref[0, :, 128 * t:128 * (t + 1)].astype(F32)
        qo_ref[0, :, 128 * t:128 * (t + 1)] = (rot(q) * (HEAD_DIM ** -0.5 * LOG2E)).astype(BF16)
    ko_ref[0] = rot(k_ref[0].astype(F32)).astype(BF16)


def _rope(proj, cos128, sin128):
    b, t, _ = proj.shape
    tm = min(t, 1024)
    return pl.pallas_call(
        _rope_kernel,
        grid=(b, t // tm),
        in_specs=[pl.BlockSpec((1, tm, 512), lambda bi, i: (bi, i, P_BQ // 512)),
                  pl.BlockSpec((1, tm, 128), lambda bi, i: (bi, i, P_BK // 128)),
                  pl.BlockSpec((tm, 128), lambda bi, i: (i, 0)),
                  pl.BlockSpec((tm, 128), lambda bi, i: (i, 0))],
        out_specs=[pl.BlockSpec((1, tm, 512), lambda bi, i: (bi, i, 0)),
                   pl.BlockSpec((1, tm, 128), lambda bi, i: (bi, i, 0))],
        out_shape=[jax.ShapeDtypeStruct((b, t, 512), BF16), jax.ShapeDtypeStruct((b, t, 128), BF16)],
        compiler_params=_cparams(("parallel", "parallel")),
        name="rope",
    )(proj, proj, cos128, sin128)


def _rope_tables(length):
    t = jnp.arange(length, dtype=jnp.int32)
    row = (t // GRID_W).astype(F32)
    col = (t % GRID_W).astype(F32)
    n_freq = HEAD_DIM // 4
    inv = ROPE_BASE ** (-jnp.arange(n_freq, dtype=F32) / n_freq)
    ang = jnp.concatenate([row[:, None] * inv, col[:, None] * inv], axis=-1)
    cos, sin = jnp.cos(ang), jnp.sin(ang)
    return jnp.tile(cos, (1, 4)), jnp.concatenate([-sin, -sin, sin, sin], axis=-1)


def _swa_kernel(q_ref, k_ref, v_ref, kc_ref, vc_ref, sink_ref, mask_ref, o_ref, *, length):
    blk = SW_BLOCK
    kw = 3 * blk
    nb = length // blk
    kc = kc_ref[0]
    vc = vc_ref[0]
    lane = lax.broadcasted_iota(jnp.int32, (blk, LANES), 1)
    a_lanes = (lane // 32) % 2 == 0
    lo_lanes = lane < 64
    top = lax.broadcasted_iota(jnp.int32, (2 * blk, 1), 0) < blk
    for sb in range(q_ref.shape[1] // blk):
        i = pl.program_id(1) * (q_ref.shape[1] // blk) + sb
        rows = slice(sb * blk, (sb + 1) * blk)
        start = pl.multiple_of(jnp.clip((i - 1) * blk, 0, length - kw), blk)
        kwin = k_ref[0, pl.ds(start, kw), :]
        vwin = v_ref[0, pl.ds(start, kw), :]
        variant = jnp.where(i == 0, 0, jnp.where(i == nb - 1, 2, 1))
        for t in range(4):
            qt = q_ref[0, rows, 128 * t:128 * (t + 1)]
            zero = jnp.zeros_like(qt)
            qq = jnp.concatenate([jnp.where(a_lanes, qt, zero), jnp.where(a_lanes, zero, qt)], axis=0)
            sw = _dot_nt(qq, kwin) + mask_ref[variant]
            sc = _dot_nt(qq, kc)
            sk = jnp.where(top, sink_ref[t:t + 1, 0:1], sink_ref[t + 4:t + 5, 0:1])
            m = jnp.maximum(jnp.maximum(jnp.max(sw, axis=-1, keepdims=True), jnp.max(sc, axis=-1, keepdims=True)), sk)
            pw = jnp.exp2(sw - m)
            pc = jnp.exp2(sc - m)
            den = jnp.sum(pw, axis=-1, keepdims=True) + jnp.sum(pc, axis=-1, keepdims=True) + jnp.exp2(sk - m)
            o = (_dot(pw.astype(BF16), vwin) + _dot(pc.astype(BF16), vc)) / den
            o_ref[0, rows, 128 * t:128 * (t + 1)] = jnp.where(lo_lanes, o[:blk], o[blk:]).astype(BF16)


def _swa(qr, kr, proj_l, kc_rot, proj_c, sink128):
    b, length, _ = qr.shape
    lc = proj_c.shape[1]
    assert length >= 3 * SW_BLOCK
    tq = 2 * SW_BLOCK if length % (2 * SW_BLOCK) == 0 else SW_BLOCK
    r = np.arange(2 * SW_BLOCK)[:, None] % SW_BLOCK
    c = np.arange(3 * SW_BLOCK)[None, :]
    mask_tab = jnp.asarray(np.stack([np.where(np.abs(v * SW_BLOCK + r - c) <= SW_WINDOW, 0.0, NEG_INF)
                                     for v in range(3)]), F32)
    return pl.pallas_call(
        functools.partial(_swa_kernel, length=length),
        grid=(b, length // tq),
        in_specs=[pl.BlockSpec((1, tq, 512), lambda bi, i: (bi, i, 0)),
                  pl.BlockSpec((1, length, 128), lambda bi, i: (bi, 0, 0)),
                  pl.BlockSpec((1, length, 128), lambda bi, i: (bi, 0, P_BV // 128)),
                  pl.BlockSpec((1, lc, 128), lambda bi, i: (bi, 0, P_BK // 128)),
                  pl.BlockSpec((1, lc, 128), lambda bi, i: (bi, 0, P_BV // 128)),
                  pl.BlockSpec((8, 128), lambda bi, i: (0, 0)),
                  pl.BlockSpec((3, 2 * SW_BLOCK, 3 * SW_BLOCK), lambda bi, i: (0, 0, 0))],
        out_specs=pl.BlockSpec((1, tq, 512), lambda bi, i: (bi, i, 0)),
        out_shape=jax.ShapeDtypeStruct((b, length, 512), BF16),
        compiler_params=_cparams(("parallel", "arbitrary")),
        name="swa",
    )(qr, kr, proj_l, kc_rot, proj_c, sink128, mask_tab)


def _ctx_attn_kernel(q_ref, k_ref, v_ref, sink_ref, o_ref, *, swa):
    t_len = q_ref.shape[1]
    lane = lax.broadcasted_iota(jnp.int32, (t_len, LANES), 1)
    lo_lanes = lane < 64
    a_lanes = ((lane // 32) % 2 == 0) if swa else lo_lanes
    top = lax.broadcasted_iota(jnp.int32, (2 * t_len, 1), 0) < t_len
    for t in range(4):
        qt = (q_ref[0, :, 128 * t:128 * (t + 1)].astype(F32) * HEAD_DIM ** -0.5).astype(BF16)
        zero = jnp.zeros_like(qt)
        qq = jnp.concatenate([jnp.where(a_lanes, qt, zero), jnp.where(a_lanes, zero, qt)], axis=0)
        if swa:
            kt, vt = k_ref[0], v_ref[0]
        else:
            kt, vt = k_ref[0, :, 128 * t:128 * (t + 1)], v_ref[0, :, 128 * t:128 * (t + 1)]
        s = _dot_nt(qq, kt)
        m = jnp.max(s, axis=-1, keepdims=True)
        if swa:
            sk = jnp.where(top, sink_ref[t:t + 1, 0:1], sink_ref[t + 4:t + 5, 0:1])
            m = jnp.maximum(m, sk)
        p = jnp.exp(s - m)
        den = jnp.sum(p, axis=-1, keepdims=True)
        if swa:
            den = den + jnp.exp(sk - m)
        o = _dot(p.astype(BF16), vt) / den
        o_ref[0, :, 128 * t:128 * (t + 1)] = jnp.where(lo_lanes, o[:t_len], o[t_len:]).astype(BF16)


def _ctx_attn(proj_c, sink128, swa):
    b, lc, _ = proj_c.shape
    if swa:
        qs = pl.BlockSpec((1, lc, 512), lambda bi: (bi, 0, P_BQ // 512))
        ks = pl.BlockSpec((1, lc, 128), lambda bi: (bi, 0, P_BK // 128))
        vs = pl.BlockSpec((1, lc, 128), lambda bi: (bi, 0, P_BV // 128))
    else:
        qs = pl.BlockSpec((1, lc, 512), lambda bi: (bi, 0, P_AQ // 512))
        ks = pl.BlockSpec((1, lc, 512), lambda bi: (bi, 0, P_AK // 512))
        vs = pl.BlockSpec((1, lc, 512), lambda bi: (bi, 0, P_AV // 512))
    return pl.pallas_call(
        functools.partial(_ctx_attn_kernel, swa=swa),
        grid=(b,),
        in_specs=[qs, ks, vs, pl.BlockSpec((8, 128), lambda bi: (0, 0))],
        out_specs=pl.BlockSpec((1, lc, 512), lambda bi: (bi, 0, 0)),
        out_shape=jax.ShapeDtypeStruct((b, lc, 512), BF16),
        compiler_params=_cparams(("parallel",)),
        name="ctx_attn_swa" if swa else "ctx_attn_na",
    )(proj_c, proj_c, proj_c, sink128)


def _natten_kernel(q_ref, k_ref, v_ref, kc_ref, vc_ref, bias_ref, o_ref, *, rows, rows_per_step):
    blk = pl.program_id(2)
    kc = kc_ref[0]
    vc = vc_ref[0]
    w = GRID_W
    nkeys = NA_KR * w
    lo_all = lax.broadcasted_iota(jnp.int32, (rows_per_step * w, LANES), 1) < 64
    lo_row = lax.broadcasted_iota(jnp.int32, (w, LANES), 1) < 64
    q_all = (q_ref[0].astype(F32) * (HEAD_DIM ** -0.5 * LOG2E)).astype(BF16)
    zero = jnp.zeros_like(q_all)
    q_lo = jnp.where(lo_all, q_all, zero)
    q_hi = jnp.where(lo_all, zero, q_all)
    sc_lo = _dot_nt(q_lo, kc)
    sc_hi = _dot_nt(q_hi, kc)
    o_rows, pc_lo, pc_hi, rd_lo, rd_hi = [], [], [], [], []

    def scores(rr):
        r = blk * rows_per_step + rr
        rs = jnp.clip(r - NA_KR // 2, 0, rows - NA_KR)
        off = rs - r + (NA_KR - 1)
        k0 = pl.multiple_of(rs * w, w)
        sl = slice(rr * w, (rr + 1) * w)
        qq = jnp.concatenate([q_lo[sl], q_hi[sl]], axis=0)
        s = _dot_nt(qq, k_ref[0, pl.ds(k0, nkeys), :]) + bias_ref[off, 0]
        sc = jnp.concatenate([sc_lo[sl], sc_hi[sl]], axis=0)
        return s, sc, k0

    ahead = 2
    queue = [scores(rr) for rr in range(ahead)]
    for rr in range(rows_per_step):
        s, sc, k0 = queue.pop(0)
        if rr + ahead < rows_per_step:
            queue.append(scores(rr + ahead))
        vrows = v_ref[0, pl.ds(k0, nkeys), :]
        m = jnp.maximum(jnp.max(s, axis=-1, keepdims=True), jnp.max(sc, axis=-1, keepdims=True))
        p = jnp.exp2(s - m)
        pc = jnp.exp2(sc - m)
        rden = 1.0 / (jnp.sum(p, axis=-1, keepdims=True) + jnp.sum(pc, axis=-1, keepdims=True))
        o = _dot(p.astype(BF16), vrows)
        pcb = pc.astype(BF16)
        o_rows.append(jnp.where(lo_row, o[:w], o[w:]))
        rd_lo.append(rden[:w])
        rd_hi.append(rden[w:])
        pc_lo.append(pcb[:w])
        pc_hi.append(pcb[w:])
    oc = jnp.where(lo_all, _dot(jnp.concatenate(pc_lo, axis=0), vc), _dot(jnp.concatenate(pc_hi, axis=0), vc))
    rd = jnp.where(lo_all, jnp.concatenate(rd_lo, axis=0), jnp.concatenate(rd_hi, axis=0))
    o_ref[0] = ((jnp.concatenate(o_rows, axis=0) + oc) * rd).astype(BF16)


def _natten(proj_l, proj_c, bias_tab):
    b, length, _ = proj_l.shape
    lc = proj_c.shape[1]
    rows = length // GRID_W
    assert rows >= NA_KR
    rps = 16 if rows % 16 == 0 else 8
    tq = rps * GRID_W
    nk = NA_KR * GRID_W
    return pl.pallas_call(
        functools.partial(_natten_kernel, rows=rows, rows_per_step=rps),
        grid=(b, 4, rows // rps),
        in_specs=[pl.BlockSpec((1, tq, 128), lambda bi, hp, i: (bi, i, P_AQ // 128 + hp)),
                  pl.BlockSpec((1, length, 128), lambda bi, hp, i: (bi, 0, P_AK // 128 + hp)),
                  pl.BlockSpec((1, length, 128), lambda bi, hp, i: (bi, 0, P_AV // 128 + hp)),
                  pl.BlockSpec((1, lc, 128), lambda bi, hp, i: (bi, 0, P_AK // 128 + hp)),
                  pl.BlockSpec((1, lc, 128), lambda bi, hp, i: (bi, 0, P_AV // 128 + hp)),
                  pl.BlockSpec((NA_KR, 1, 2 * GRID_W, nk), lambda bi, hp, i: (0, hp, 0, 0))],
        out_specs=pl.BlockSpec((1, tq, 128), lambda bi, hp, i: (bi, i, hp)),
        out_shape=jax.ShapeDtypeStruct((b, length, 512), BF16),
        compiler_params=_cparams(("parallel", "parallel", "arbitrary")),
        name="natten",
    )(proj_l, proj_l, proj_l, proj_c, proj_c, bias_tab)


def _natten_bias_table(rpb):
    w = GRID_W
    nd = 2 * NA_KC - 1
    qc = np.arange(w)[:, None]
    kcol = np.arange(w)[None, :]
    cs = np.clip(qc - NA_KC // 2, 0, w - NA_KC)
    cmask = (kcol >= cs) & (kcol < cs + NA_KC)
    period = nd + w
    padded = jnp.pad(rpb.astype(F32), ((0, 0), (0, 0), (0, w)))
    flat = jnp.tile(padded, (1, 1, w))[:, :, :w * (period - 1)]
    t = flat.reshape(NA_HEADS, 2 * NA_KR - 1, w, period - 1)[..., NA_KC - 1:NA_KC - 1 + w]
    g = jnp.stack([t[:, off:off + NA_KR] for off in range(NA_KR)])
    g = jnp.where(cmask[None, None, None], g * LOG2E, NEG_INF)
    return g.transpose(0, 1, 3, 2, 4).reshape(NA_KR, NA_HEADS // 2, 2 * w, NA_KR * w)


def _gla_kernel(*refs, rev, ncb, final):
    if final:
        (q_ref, k_ref, v_ref, lr_ref, whi_ref, wlo_ref, b_ref, s0_ref, of_ref, gnw_ref,
         o_ref, sfin_ref, st_ref) = refs
    else:
        (q_ref, k_ref, v_ref, lr_ref, whi_ref, wlo_ref, b_ref, s0_ref,
         o_ref, sfin_ref, st_ref) = refs
    ch = GLA_CHUNK
    nstream = q_ref.shape[0]

    @pl.when(pl.program_id(1) == 0)
    def _():
        st_ref[...] = s0_ref[:, 0]

    tb = ncb * ch
    rix = lax.broadcasted_iota(jnp.int32, (ch, ch), 0)
    cix = lax.broadcasted_iota(jnp.int32, (ch, ch), 1)
    tri = (rix <= cix) if rev else (rix >= cix)
    lo_lanes = lax.broadcasted_iota(jnp.int32, (tb, LANES), 1) < 64
    end = 0 if rev else ch - 1
    pos = lax.broadcasted_iota(jnp.int32, (tb, 1), 0) % ch
    order = [(ncb - 1 - cc) if rev else cc for cc in range(ncb)]

    def prefix(s):
        lr = lr_ref[s]
        arg = _dot(lr, whi_ref[0]) + _dot(lr, wlo_ref[0]) + b_ref[0]
        cum = (jnp.minimum(arg, 0.0) - jnp.log(1.0 + jnp.exp(-jnp.abs(arg)))) * (1.0 / GLA_TAU)
        step = 1
        while step < ch:
            if rev:
                cum = cum + jnp.where(pos < ch - step, pltpu.roll(cum, tb - step, 0), 0.0)
            else:
                cum = cum + jnp.where(pos >= step, pltpu.roll(cum, step, 0), 0.0)
            step *= 2
        q = q_ref[s].astype(F32)
        k = k_ref[s].astype(F32)
        qd = (q * (HEAD_DIM ** -0.5) * jnp.exp(cum)).astype(BF16)
        kd = (k * jnp.exp(-cum)).astype(BF16)
        zero = jnp.zeros_like(qd)
        return cum, k, kd, (jnp.where(lo_lanes, qd, zero), jnp.where(lo_lanes, zero, qd))

    def intra_chunks(s, pre):
        cum, k, kd, qd_h = pre
        res = {}
        for c in order:
            sl = slice(c * ch, (c + 1) * ch)
            cum_c = cum[sl]
            cum_end = cum_c[end:end + 1, :]
            kdec = (k[sl] * jnp.exp(cum_end - cum_c)).astype(BF16)
            for h in range(2):
                vh = v_ref[s, sl, 128 * h:128 * (h + 1)]
                a = jnp.where(tri, _dot_nt(qd_h[h][sl], kd[sl]), 0.0)
                res[c, h] = (_dot(a.astype(BF16), vh), _dot_tn(vh, kdec), jnp.exp(cum_end))
        return res

    def recurrence(s, pre, res):
        qd_h = pre[3]
        st = [st_ref[s, 0], st_ref[s, 1]]
        for c in order:
            sl = slice(c * ch, (c + 1) * ch)
            for h in range(2):
                o_intra, kv_t, gdec = res[c, h]
                o = o_intra + _dot_nt(qd_h[h][sl], st[h].astype(BF16))
                st[h] = st[h] * gdec + kv_t
                if final:
                    o = o + of_ref[s, sl, 128 * h:128 * (h + 1)]
                    ms = jnp.mean(o * o, axis=-1, keepdims=True)
                    o = o * lax.rsqrt(ms + EPS) * gnw_ref[...]
                    o_ref[s, sl, 128 * h:128 * (h + 1)] = o.astype(BF16)
                else:
                    o_ref[s, sl, 128 * h:128 * (h + 1)] = o
        st_ref[s, 0] = st[0]
        st_ref[s, 1] = st[1]

    pres = [prefix(s) for s in range(nstream)]
    ress = [intra_chunks(s, pres[s]) for s in range(nstream)]
    for s in range(nstream):
        recurrence(s, pres[s], ress[s])
    sfin_ref[:, 0] = st_ref[...]


def _gla_pass(proj, w_pad_hi, w_pad_lo, b_pad, s0, rev, o_fwd=None, gnw=None):
    b, t, _ = proj.shape
    final = o_fwd is not None
    tb = min(t, 512)
    nblk = t // tb
    ncb = tb // GLA_CHUNK
    d = 1 if rev else 0

    def bi_map(i):
        return (nblk - 1 - i) if rev else i

    in_specs = [pl.BlockSpec((b, tb, 128), lambda p, i: (0, bi_map(i), P_CQ // 128 + p)),
                pl.BlockSpec((b, tb, 128), lambda p, i: (0, bi_map(i), P_CK // 128 + p)),
                pl.BlockSpec((b, tb, 256), lambda p, i: (0, bi_map(i), P_CV // 256 + p)),
                pl.BlockSpec((b, tb, 128), lambda p, i: (0, bi_map(i), P_LR // 128)),
                pl.BlockSpec((1, 128, 128), lambda p, i: (2 * d + p, 0, 0)),
                pl.BlockSpec((1, 128, 128), lambda p, i: (2 * d + p, 0, 0)),
                pl.BlockSpec((1, 1, 128), lambda p, i: (2 * d + p, 0, 0)),
                pl.BlockSpec((b, 1, 2, 128, 128), lambda p, i: (0, p, 0, 0, 0))]
    args = [proj, proj, proj, proj, w_pad_hi, w_pad_lo, b_pad, s0]
    if final:
        in_specs += [pl.BlockSpec((b, tb, 256), lambda p, i: (0, bi_map(i), p)),
                     pl.BlockSpec((1, 128), lambda p, i: (0, 0))]
        args += [o_fwd, gnw.reshape(1, 128)]
    out, s_fin = pl.pallas_call(
        functools.partial(_gla_kernel, rev=rev, ncb=ncb, final=final),
        grid=(2, nblk),
        in_specs=in_specs,
        out_specs=[pl.BlockSpec((b, tb, 256), lambda p, i: (0, bi_map(i), p)),
                   pl.BlockSpec((b, 1, 2, 128, 128), lambda p, i: (0, p, 0, 0, 0))],
        out_shape=[jax.ShapeDtypeStruct((b, t, 512), BF16 if final else F32),
                   jax.ShapeDtypeStruct((b, 2, 2, 128, 128), F32)],
        scratch_shapes=[pltpu.VMEM((b, 2, 128, 128), F32)],
        compiler_params=_cparams(("parallel", "arbitrary")),
        name="gla_rev" if rev else "gla_fwd",
    )(*args)
    return out, s_fin


def _gla_decay_weights(w_alpha_up, b_alpha):
    w = jnp.zeros((2, 2, 128, 128), F32)
    for d in range(2):
        for p in range(2):
            w = w.at[d, p, 16 * d:16 * d + 16, :].set(w_alpha_up[d][:, 128 * p:128 * (p + 1)].astype(F32))
    w = w.reshape(4, 128, 128)
    hi = w.astype(BF16)
    lo = (w - hi.astype(F32)).astype(BF16)
    return hi, lo, b_alpha.astype(F32).reshape(4, 1, 128)


def _hy_pre_kernel(u0, u1, u2, p0, p1, p2, n0, n1, n2, w_ref, b_ref, x0_ref, z_ref, *, nblk):
    i = pl.program_id(1)
    tm = u0.shape[1]
    row = lax.broadcasted_iota(jnp.int32, (tm, 1), 0)
    has_prev = (i > 0).astype(F32)
    has_next = (i < nblk - 1).astype(F32)

    def conv(u_ref, p_ref, n_ref, j):
        u = u_ref[0].astype(F32)
        prev = p_ref[0, 7:8, :].astype(F32) * has_prev
        nxt = n_ref[0, 0:1, :].astype(F32) * has_next
        up = jnp.where(row == 0, prev, pltpu.roll(u, 1, 0))
        dn = jnp.where(row == tm - 1, nxt, pltpu.roll(u, tm - 1, 0))
        w = w_ref[:, 512 * j:512 * (j + 1)]
        return up * w[0:1] + u * w[1:2] + dn * w[2:3] + b_ref[:, 512 * j:512 * (j + 1)]

    x0_ref[0] = conv(u0, p0, n0, 0).astype(BF16)
    z_ref[0] = (conv(u1, p1, n1, 1) * conv(u2, p2, n2, 2)).astype(BF16)


def _hy_pre(proj, conv_w, conv_b):
    b, t, _ = proj.shape
    tm = min(t, 512)
    nblk = t // tm
    hb = tm // 8
    nrb = t // 8
    c0 = P_DU // 512
    main = [pl.BlockSpec((1, tm, 512), functools.partial(lambda bi, i, j: (bi, i, c0 + j), j=j)) for j in range(3)]
    prev = [pl.BlockSpec((1, 8, 512), functools.partial(lambda bi, i, j: (bi, jnp.maximum(i * hb - 1, 0), c0 + j), j=j))
            for j in range(3)]
    nxt = [pl.BlockSpec((1, 8, 512),
                        functools.partial(lambda bi, i, j: (bi, jnp.minimum((i + 1) * hb, nrb - 1), c0 + j), j=j))
           for j in range(3)]
    return pl.pallas_call(
        functools.partial(_hy_pre_kernel, nblk=nblk),
        grid=(b, nblk),
        in_specs=main + prev + nxt + [pl.BlockSpec((3, 1536), lambda bi, i: (0, 0)),
                                      pl.BlockSpec((1, 1536), lambda bi, i: (0, 0))],
        out_specs=[pl.BlockSpec((1, tm, 512), lambda bi, i: (bi, i, 0))] * 2,
        out_shape=[jax.ShapeDtypeStruct((b, t, 512), BF16)] * 2,
        compiler_params=_cparams(("parallel", "parallel")),
        name="hyena_pre",
    )(*([proj] * 9), conv_w.astype(F32), conv_b.astype(F32).reshape(1, 1536))


def _hy_filter_kernel(z_ref, w1, b1, w2, b2, w3, b3, fr, wo, dl, h_ref, l1_ref, *, tm, t_len):
    i = pl.program_id(0)
    f = fr[...]
    half = tm // 2
    z = z_ref[...]
    hh = jnp.concatenate([z[:half], z[half:]], axis=1)
    hh = jnp.sin(f * (_dot_x3(hh, w1[...]) + b1[...]))
    hh = jnp.sin(f * (_dot_x3(hh, w2[...]) + b2[...]))
    hh = jnp.sin(f * (_dot_x3(hh, w3[...]) + b3[...]))
    hh = _dot_x3(hh, wo[...])
    hh = jnp.concatenate([hh[:, :2 * HY_WIDTH], hh[:, 2 * HY_WIDTH:]], axis=0)
    row = i * tm + lax.broadcasted_iota(jnp.int32, (tm, 1), 0)
    t = row.astype(F32) / (t_len - 1)
    decay = jnp.exp(-t * dl[...])
    h_f = hh[:, :HY_WIDTH] * decay
    h_b = jnp.where(row == 0, 0.0, hh[:, HY_WIDTH:] * decay)
    h_ref[0] = h_f.astype(BF16)
    h_ref[1] = h_b.astype(BF16)
    part = jnp.sum(jnp.abs(h_f) + jnp.abs(h_b), axis=0, keepdims=True)

    @pl.when(i == 0)
    def _():
        l1_ref[...] = part

    @pl.when(i > 0)
    def _():
        l1_ref[...] = l1_ref[...] + part


def _hy_filter(t_len, w1, b1, w2, b2, w3, b3, freq, wout):
    tm = min(t_len, 512)
    t = jnp.linspace(0.0, 1.0, t_len, dtype=F32)[:, None]
    bands = (HY_EMB - 1) // 2
    w_ang = 2.0 * math.pi * jnp.arange(t_len, dtype=F32)[:, None] / t_len
    f = jnp.linspace(1e-4, bands - 1, bands, dtype=F32)[None, :]
    z = jnp.concatenate([t, jnp.cos(f * w_ang), -jnp.sin(f * w_ang),
                         jnp.zeros((t_len, LANES - HY_EMB), F32)], axis=-1)
    w1p = jnp.concatenate([w1.astype(F32), jnp.zeros((LANES - HY_EMB, HY_FFN), F32)], axis=0)
    deltas = jnp.abs(jnp.linspace(math.log(HY_TARGET) / HY_SLOW_PCT, math.log(HY_TARGET) / HY_FAST_PCT,
                                  HY_WIDTH, dtype=F32))[None, :]
    full = lambda shape: pl.BlockSpec(shape, lambda i: tuple(0 for _ in shape))
    r = lambda v: jnp.tile(v.astype(F32).reshape(1, -1), (1, 2))

    def bdiag(w):
        w = w.astype(F32)
        zero = jnp.zeros_like(w)
        return jnp.concatenate([jnp.concatenate([w, zero], axis=1), jnp.concatenate([zero, w], axis=1)], axis=0)

    hid = 2 * HY_FFN
    return pl.pallas_call(
        functools.partial(_hy_filter_kernel, tm=tm, t_len=t_len),
        grid=(t_len // tm,),
        in_specs=[pl.BlockSpec((tm, LANES), lambda i: (i, 0)),
                  full((2 * LANES, hid)), full((1, hid)), full((hid, hid)), full((1, hid)),
                  full((hid, hid)), full((1, hid)), full((1, hid)), full((hid, 4 * HY_WIDTH)),
                  full((1, HY_WIDTH))],
        out_specs=[pl.BlockSpec((2, tm, HY_WIDTH), lambda i: (0, i, 0)),
                   pl.BlockSpec((1, HY_WIDTH), lambda i: (0, 0))],
        out_shape=[jax.ShapeDtypeStruct((2, t_len, HY_WIDTH), BF16), jax.ShapeDtypeStruct((1, HY_WIDTH), F32)],
        compiler_params=_cparams(("arbitrary",)),
        name="hyena_filter",
    )(z, bdiag(w1p), r(b1), bdiag(w2), r(b2), bdiag(w3), r(b3), r(freq), bdiag(wout), deltas)


BF16_TILE_ROWS = 16
FFT_K1_PER_STEP = 4


def _used_k1(n1):
    need = n1 // 2 + 1
    return min(n1, -(-need // BF16_TILE_ROWS) * BF16_TILE_ROWS)


def _dft_tables(t_len):
    n = 2 * t_len
    n2 = FFT_N2
    n1 = n // n2
    half = n1 // 2
    k1u = _used_k1(n1)
    k1 = np.arange(k1u)[:, None]
    j1 = np.arange(half)[None, :]
    ang1 = 2.0 * np.pi * ((k1 * j1) % n1) / n1
    f1 = np.concatenate([np.cos(ang1), -np.sin(ang1)], axis=0)
    wgt = np.where((k1 == 0) | (k1 == half), 1.0, np.where(k1 < half, 2.0, 0.0))
    i1 = np.concatenate([(wgt * np.cos(ang1)).T, (-wgt * np.sin(ang1)).T], axis=1)
    n1 = k1u
    a = np.arange(n2)
    ang2 = 2.0 * np.pi * ((a[:, None] * a[None, :]) % n2) / n2
    fr, fi = np.cos(ang2), -np.sin(ang2)
    fwd = np.block([[fr, -fi], [fi, fr]])
    inv = np.block([[fr, fi], [-fi, fr]])
    angt = 2.0 * np.pi * ((np.arange(n1)[:, None] * a[None, :]) % n) / n
    twc = np.cos(angt)[:, :, None]
    tws = np.sin(angt)[:, :, None]
    return (jnp.asarray(f1, F32).astype(BF16), jnp.asarray(i1, F32).astype(BF16), jnp.asarray(fwd, F32).astype(BF16),
            jnp.asarray(inv, F32).astype(BF16),
            jnp.broadcast_to(jnp.asarray(twc, F32), (n1, n2, LANES)),
            jnp.broadcast_to(jnp.asarray(tws, F32), (n1, n2, LANES)))


FFT_GROUP = BF16_TILE_ROWS


def _swap_table():
    g = FFT_GROUP
    r = np.arange(g * g)
    p = np.zeros((g * g, g * g), np.float32)
    p[r, (r % g) * g + r // g] = 1.0
    return jnp.asarray(p, F32).astype(BF16)


def _regroup(p, ref, n_rows, width):
    g = FFT_GROUP
    swapped = [_dot(p, ref[0, g * blk:g * (blk + 1)].reshape(g * g, width)).astype(BF16)
               for blk in range(n_rows // g)]
    return [jnp.concatenate([s[g * j:g * (j + 1)] for s in swapped], axis=0) for j in range(g)]


def _hy_s1_kernel(z_ref, f_ref, p_ref, xr_ref, xi_ref):
    g = FFT_GROUP
    half, width = z_ref.shape[1], z_ref.shape[3]
    k1u = xr_ref.shape[1]
    p = p_ref[...]
    ys = [_dot(f_ref[...], zj).astype(BF16) for zj in _regroup(p, z_ref, half, width)]
    for part, ref in enumerate((xr_ref, xi_ref)):
        for kb in range(k1u // g):
            r0 = part * k1u + g * kb
            rows = jnp.concatenate([y[r0:r0 + g] for y in ys], axis=0)
            ref[0, g * kb:g * (kb + 1)] = _dot(p, rows).astype(BF16).reshape(g, g, width)


def _hy_s1(z4, f1, swap):
    b, half, n2, w = z4.shape
    g = FFT_GROUP
    k1u = f1.shape[0] // 2
    assert half % g == 0 and n2 % g == 0 and k1u % g == 0
    return pl.pallas_call(
        _hy_s1_kernel,
        grid=(b, n2 // g),
        in_specs=[pl.BlockSpec((1, half, g, w), lambda bi, j: (bi, 0, j, 0)),
                  pl.BlockSpec((2 * k1u, half), lambda bi, j: (0, 0)),
                  pl.BlockSpec((g * g, g * g), lambda bi, j: (0, 0))],
        out_specs=[pl.BlockSpec((1, k1u, g, w), lambda bi, j: (bi, 0, j, 0))] * 2,
        out_shape=[jax.ShapeDtypeStruct((b, k1u, n2, w), BF16)] * 2,
        compiler_params=_cparams(("parallel", "parallel")),
        name="hyena_dft_outer",
    )(z4, f1, swap)


def _twiddle_fwd(xr, xi, c, s):
    return xr * c + xi * s, xi * c - xr * s


def _lane_tile(ref, j, reps):
    v = ref[j]
    return jnp.concatenate([v] * reps, axis=1)


def _hy_filt_spec_kernel(xr_ref, xi_ref, twc_ref, tws_ref, fwd_ref, l1_ref, kr_ref, ki_ref, *, n):
    reps = xr_ref.shape[-1] // LANES
    n2 = FFT_N2
    scale = 1.0 / (l1_ref[...] * n)
    for j in range(FFT_K1_PER_STEP):
        c = _lane_tile(twc_ref, j, reps)
        s = _lane_tile(tws_ref, j, reps)
        parts = []
        for f in range(2):
            ar, ai = _twiddle_fwd(xr_ref[f, j].astype(F32), xi_ref[f, j].astype(F32), c, s)
            y = _dot(fwd_ref[...], jnp.concatenate([ar, ai], axis=0).astype(BF16))
            parts.append((y[:n2], y[n2:]))
        kr_ref[j] = ((parts[0][0] + parts[1][0]) * scale).astype(BF16)
        ki_ref[j] = ((parts[0][1] - parts[1][1]) * scale).astype(BF16)


def _hy_filt_spec(xr, xi, twc, tws, fwd, l1, n):
    _, n1, n2, w = xr.shape
    kb = FFT_K1_PER_STEP
    return pl.pallas_call(
        functools.partial(_hy_filt_spec_kernel, n=n),
        grid=(n1 // kb,),
        in_specs=[pl.BlockSpec((2, kb, n2, w), lambda k: (0, k, 0, 0)),
                  pl.BlockSpec((2, kb, n2, w), lambda k: (0, k, 0, 0)),
                  pl.BlockSpec((kb, n2, LANES), lambda k: (k, 0, 0)),
                  pl.BlockSpec((kb, n2, LANES), lambda k: (k, 0, 0)),
                  pl.BlockSpec((2 * n2, 2 * n2), lambda k: (0, 0)),
                  pl.BlockSpec((1, w), lambda k: (0, 0))],
        out_specs=[pl.BlockSpec((kb, n2, w), lambda k: (k, 0, 0))] * 2,
        out_shape=[jax.ShapeDtypeStruct((n1, n2, w), BF16)] * 2,
        compiler_params=_cparams(("parallel",)),
        name="hyena_filter_spectrum",
    )(xr, xi, twc, tws, fwd, l1)


def _hy_mid_kernel(xr_ref, xi_ref, twc_ref, tws_ref, fwd_ref, inv_ref, kr_ref, ki_ref, tr_ref, ti_ref):
    reps = xr_ref.shape[-1] // LANES
    n2 = FFT_N2
    for j in range(FFT_K1_PER_STEP):
        c = _lane_tile(twc_ref, j, reps)
        s = _lane_tile(tws_ref, j, reps)
        ar, ai = _twiddle_fwd(xr_ref[0, j].astype(F32), xi_ref[0, j].astype(F32), c, s)
        y = _dot(fwd_ref[...], jnp.concatenate([ar, ai], axis=0).astype(BF16))
        yr, yi = y[:n2], y[n2:]
        kr = kr_ref[j].astype(F32)
        ki = ki_ref[j].astype(F32)
        zr = yr * kr - yi * ki
        zi = yr * ki + yi * kr
        u = _dot(inv_ref[...], jnp.concatenate([zr, zi], axis=0).astype(BF16))
        ur, ui = u[:n2], u[n2:]
        tr_ref[0, j] = (ur * c - ui * s).astype(BF16)
        ti_ref[0, j] = (ur * s + ui * c).astype(BF16)


def _hy_mid(xr, xi, twc, tws, fwd, inv, kr, ki):
    b, n1, n2, w = xr.shape
    kb = FFT_K1_PER_STEP
    xspec = pl.BlockSpec((1, kb, n2, w), lambda bi, k: (bi, k, 0, 0))
    return pl.pallas_call(
        _hy_mid_kernel,
        grid=(b, n1 // kb),
        in_specs=[xspec, xspec,
                  pl.BlockSpec((kb, n2, LANES), lambda bi, k: (k, 0, 0)),
                  pl.BlockSpec((kb, n2, LANES), lambda bi, k: (k, 0, 0)),
                  pl.BlockSpec((2 * n2, 2 * n2), lambda bi, k: (0, 0)),
                  pl.BlockSpec((2 * n2, 2 * n2), lambda bi, k: (0, 0)),
                  pl.BlockSpec((kb, n2, w), lambda bi, k: (k, 0, 0)),
                  pl.BlockSpec((kb, n2, w), lambda bi, k: (k, 0, 0))],
        out_specs=[xspec, xspec],
        out_shape=[jax.ShapeDtypeStruct((b, n1, n2, w), BF16)] * 2,
        compiler_params=_cparams(("parallel", "parallel")),
        name="hyena_dft_inner",
    )(xr, xi, twc, tws, fwd, inv, kr, ki)


def _hy_post_kernel(tr_ref, ti_ref, i1_ref, p_ref, x0_ref, z_ref, fb_ref, o_ref):
    g = FFT_GROUP
    k1u, width = tr_ref.shape[1], tr_ref.shape[3]
    half = x0_ref.shape[1]
    p = p_ref[...]
    t_re = _regroup(p, tr_ref, k1u, width)
    t_im = _regroup(p, ti_ref, k1u, width)
    ys = [_dot(i1_ref[...], jnp.concatenate([t_re[j], t_im[j]], axis=0)).astype(BF16) for j in range(g)]
    for nb in range(half // g):
        rows = jnp.concatenate([y[g * nb:g * (nb + 1)] for y in ys], axis=0)
        y_nat = _dot(p, rows)
        sl = slice(g * nb, g * (nb + 1))
        x0 = x0_ref[0, sl].reshape(g * g, width).astype(F32)
        z = z_ref[0, sl].reshape(g * g, width).astype(F32)
        o_ref[0, sl] = (x0 * (y_nat + z * fb_ref[...])).astype(BF16).reshape(g, g, width)


def _hy_post(tr, ti, i1, swap, x0_4, z4, fb):
    b, k1u, n2, w = tr.shape
    half = i1.shape[0]
    g = FFT_GROUP
    hspec = pl.BlockSpec((1, half, g, w), lambda bi, j: (bi, 0, j, 0))
    tspec = pl.BlockSpec((1, k1u, g, w), lambda bi, j: (bi, 0, j, 0))
    return pl.pallas_call(
        _hy_post_kernel,
        grid=(b, n2 // g),
        in_specs=[tspec, tspec, pl.BlockSpec((half, 2 * k1u), lambda bi, j: (0, 0)),
                  pl.BlockSpec((g * g, g * g), lambda bi, j: (0, 0)), hspec, hspec,
                  pl.BlockSpec((1, w), lambda bi, j: (0, 0))],
        out_specs=hspec,
        out_shape=jax.ShapeDtypeStruct((b, half, n2, w), BF16),
        compiler_params=_cparams(("parallel", "parallel")),
        name="hyena_dft_outer_inv",
    )(tr, ti, i1, swap, x0_4, z4, fb)


def _hyena_long(proj, conv_w, conv_b, filt, filt_bias):
    b, t, _ = proj.shape
    w = HY_WIDTH
    n = 2 * t
    n2 = FFT_N2
    n1 = n // n2
    k1u = _used_k1(n1)
    f1, i1, fwd, inv, twc, tws = _dft_tables(t)
    swap = _swap_table()
    hfb, l1 = _hy_filter(t, *filt)
    hr, hi = _hy_s1(hfb.reshape(2, n1 // 2, n2, w), f1, swap)
    kr, ki = _hy_filt_spec(hr, hi, twc, tws, fwd, l1, n)
    x0, z = _hy_pre(proj, conv_w, conv_b)
    z4 = z.reshape(b, n1 // 2, n2, w)
    xr, xi = _hy_s1(z4, f1, swap)
    tr, ti = _hy_mid(xr, xi, twc, tws, fwd, inv, kr, ki)
    d = _hy_post(tr, ti, i1, swap, x0.reshape(b, n1 // 2, n2, w), z4, filt_bias.astype(F32).reshape(1, w))
    return d.reshape(b, t, w)


def _hy_ctx_kernel(x0_ref, z_ref, h_ref, l1_ref, cm_ref, sm_ref, cmt_ref, smt_ref, fb_ref, o_ref, *, n):
    cm = cm_ref[...]
    sm = sm_ref[...]

    def spectrum(v):
        return _dot(cm, v), -_dot(sm, v)

    z = z_ref[0]
    zr, zi = spectrum(z)
    ar, ai = spectrum(h_ref[0])
    br, bi = spectrum(h_ref[1])
    scale = 1.0 / (l1_ref[...] * n)
    kr = (ar + br) * scale
    ki = (ai - bi) * scale
    yr = (zr * kr - zi * ki).astype(BF16)
    yi = (zr * ki + zi * kr).astype(BF16)
    y = _dot(cmt_ref[...], yr) - _dot(smt_ref[...], yi)
    o_ref[0] = (x0_ref[0].astype(F32) * (y + z.astype(F32) * fb_ref[...])).astype(BF16)


def _hyena_short(proj, conv_w, conv_b, filt, filt_bias):
    b, t, _ = proj.shape
    w = HY_WIDTH
    n = 2 * t
    hfb, l1 = _hy_filter(t, *filt)
    x0, z = _hy_pre(proj, conv_w, conv_b)
    k = np.arange(n)[:, None]
    j = np.arange(t)[None, :]
    ang = 2.0 * np.pi * ((k * j) % n) / n
    cm, sm = np.cos(ang), np.sin(ang)
    tabs = [jnp.asarray(a, F32).astype(BF16) for a in (cm, sm, cm.T, sm.T)]
    full2 = lambda shape: pl.BlockSpec(shape, lambda bi: (0, 0))
    bspec = pl.BlockSpec((1, t, w), lambda bi: (bi, 0, 0))
    return pl.pallas_call(
        functools.partial(_hy_ctx_kernel, n=n),
        grid=(b,),
        in_specs=[bspec, bspec, pl.BlockSpec((2, t, w), lambda bi: (0, 0, 0)), full2((1, w)),
                  full2((n, t)), full2((n, t)), full2((t, n)), full2((t, n)), full2((1, w))],
        out_specs=bspec,
        out_shape=jax.ShapeDtypeStruct((b, t, w), BF16),
        compiler_params=_cparams(("parallel",)),
        name="hyena_ctx",
    )(x0, z, hfb, l1, *tabs, filt_bias.astype(F32).reshape(1, w))


def _layer(xc, xl, mod, cos128, sin128, norm_w, w_in, rpb, sink, w_alpha_up, b_alpha, gla_norm_w, conv_w, conv_b,
           filt, filt_bias, w_branch, w_out, final_w, with_ctx_out):
    b, length, d = xl.shape
    mod_l = mod[:b].reshape(b, 1, 3 * d)
    mod_c = jnp.broadcast_to(mod[b].reshape(1, 1, 3 * d), (b, 1, 3 * d))
    sh_l, sc_l, g_l = mod_l[..., :d], mod_l[..., d:2 * d], mod_l[..., 2 * d:]
    sh_c, sc_c, g_c = mod_c[..., :d], mod_c[..., d:2 * d], mod_c[..., 2 * d:]

    w_perm = _permute(w_in.astype(BF16), _IN_PERM, axis=1) * jnp.asarray(_HALF_COLS, BF16)[None, :]
    wbf = w_branch.astype(BF16) * 0.5
    wb = jnp.stack([wbf[0], _permute(wbf[1], _SWA_OUT_PERM, axis=0), wbf[2], wbf[3]])
    wo = w_out.astype(BF16)

    proj_l = _inproj(xl, norm_w, sc_l, sh_l, w_perm)
    proj_c = _inproj(xc, norm_w, sc_c, sh_c, w_perm)

    y_a = _natten(proj_l, proj_c, _natten_bias_table(rpb))
    sink128 = jnp.broadcast_to(sink.astype(F32)[:, None], (SW_HEADS, LANES))
    qr, kr = _rope(proj_l, cos128, sin128)
    y_b = _swa(qr, kr, proj_l, proj_c, proj_c, sink128 * LOG2E)
    whi, wlo, bpad = _gla_decay_weights(w_alpha_up, b_alpha)
    s_zero = jnp.zeros((b, 2, 2, 128, 128), F32)
    of_c, s_cf = _gla_pass(proj_c, whi, wlo, bpad, s_zero, rev=False)
    y_cc, s_cb = _gla_pass(proj_c, whi, wlo, bpad, s_zero, rev=True, o_fwd=of_c, gnw=gla_norm_w.astype(F32))
    of_l, _ = _gla_pass(proj_l, whi, wlo, bpad, s_cf, rev=False)
    y_c, _ = _gla_pass(proj_l, whi, wlo, bpad, s_cb, rev=True, o_fwd=of_l, gnw=gla_norm_w.astype(F32))
    y_d = _hyena_long(proj_l, conv_w, conv_b, filt, filt_bias)

    xl_new = _merge([y_a, y_b, y_c, y_d], proj_l, wb, wo, g_l, xl, final_w, final=not with_ctx_out)
    if with_ctx_out:
        ya_c = _ctx_attn(proj_c, sink128, swa=False)
        yb_c = _ctx_attn(proj_c, sink128, swa=True)
        yd_c = _hyena_short(proj_c, conv_w, conv_b, filt, filt_bias)
        xc = _merge([ya_c, yb_c, y_cc, yd_c], proj_c, wb, wo, g_c, xc, final_w, final=False)
    return xc, xl_new


def kernel(x, c, ctx, c_ctx, norm_w, w_mod, b_mod, w_in, rpb, sink, w_alpha_up, b_alpha, gla_norm_w, conv_w, conv_b,
           filt_w1, filt_b1, filt_w2, filt_b2, filt_w3, filt_b3, filt_freq, filt_wout, filt_bias, w_branch, w_out,
           final_norm_w):
    b, length, d = x.shape
    depth = norm_w.shape[0]
    cvec = jnp.zeros((8, d), F32).at[:b].set(c.astype(F32)).at[b].set(c_ctx.astype(F32))
    mod = _modulation(cvec, w_mod.astype(F32), b_mod.astype(F32))
    cos128, sin128 = _rope_tables(length)
    xc, xl = ctx, x
    for i in range(depth):
        filt = (filt_w1[i], filt_b1[i], filt_w2[i], filt_b2[i], filt_w3[i], filt_b3[i], filt_freq[i], filt_wout[i])
        xc, xl = _layer(xc, xl, mod[i], cos128, sin128, norm_w[i], w_in[i], rpb[i], sink[i], w_alpha_up[i],
                        b_alpha[i], gla_norm_w[i], conv_w[i], conv_b[i], filt, filt_bias[i], w_branch[i], w_out[i],
                        final_norm_w, with_ctx_out=(i < depth - 1))
    return xl
```

```python
import functools
import math

import numpy as np
import jax
import jax.numpy as jnp
from jax import lax
from jax.experimental import pallas as pl
from jax.experimental.pallas import tpu as pltpu

F32 = jnp.float32
BF16 = jnp.bfloat16

D_MODEL = 1024
GRID_W = 64
HEAD_DIM = 64
BRANCH_WIDTH = D_MODEL // 2
N_BRANCH = 4
NA_HEADS = 8
NA_KR = 8
NA_KC = 16
SW_HEADS = 8
SW_KV_HEADS = 2
SW_WINDOW = 128
SW_BLOCK = 128
GLA_HEADS = 4
GLA_DK = BRANCH_WIDTH // 2
GLA_DV = BRANCH_WIDTH
GLA_RANK = 16
GLA_TAU = 16.0
GLA_CHUNK = 64
HY_WIDTH = BRANCH_WIDTH
HY_EMB = 33
HY_FFN = 64
HY_TARGET = 1e-2
HY_FAST_PCT = 0.3
HY_SLOW_PCT = 1.5
ROPE_BASE = 10000.0
EPS = 1e-6
NEG_INF = -1e30
LOG2E = math.log2(math.e)
LANES = 128
FFT_N2 = 128
VMEM_LIMIT = 56 * 1024 * 1024

_IN_WIDTHS = (512, 512, 512, 512, 512, 128, 128, 512, 256, 256, 512, 16, 16, 512, 1536, 512, 4096)
_IN_OFF = np.concatenate([[0], np.cumsum(_IN_WIDTHS)])
(_O_AQ, _O_AK, _O_AV, _O_AG, _O_BQ, _O_BK, _O_BV, _O_BG, _O_CQ, _O_CK, _O_CV, _O_LRF, _O_LRB, _O_CG,
 _O_DU, _O_DG, _O_GM) = [int(v) for v in _IN_OFF[:-1]]
IN_TOTAL = int(_IN_OFF[-1])

P_AQ, P_AK, P_AV, P_AG = 0, 512, 1024, 1536
P_BQ, P_BG = 2048, 2560
P_CV, P_CG = 3072, 3584
P_DU, P_DG = 4096, 5632
P_GM = 6144
P_CQ, P_CK = 10240, 10496
P_BK, P_BV = 10752, 10880
P_LR = 11008
NP_COLS = 11264


def _swa_q_perm():
    idx = np.zeros(512, np.int64)
    for t in range(4):
        a, b = t, t + 4
        base = 128 * t
        idx[base + 0:base + 32] = 64 * a + np.arange(32)
        idx[base + 32:base + 64] = 64 * b + np.arange(32)
        idx[base + 64:base + 96] = 64 * a + 32 + np.arange(32)
        idx[base + 96:base + 128] = 64 * b + 32 + np.arange(32)
    return idx


def _swa_k_perm():
    idx = np.zeros(128, np.int64)
    idx[0:32] = np.arange(32)
    idx[32:64] = 64 + np.arange(32)
    idx[64:96] = 32 + np.arange(32)
    idx[96:128] = 96 + np.arange(32)
    return idx


def _swa_out_perm():
    idx = np.zeros(512, np.int64)
    for t in range(4):
        idx[128 * t:128 * t + 64] = 64 * t + np.arange(64)
        idx[128 * t + 64:128 * t + 128] = 64 * (t + 4) + np.arange(64)
    return idx


def _build_in_perm():
    perm = np.full(NP_COLS, -1, np.int64)

    def put(p, o, w):
        perm[p:p + w] = o + np.arange(w)

    put(P_AQ, _O_AQ, 512); put(P_AK, _O_AK, 512); put(P_AV, _O_AV, 512); put(P_AG, _O_AG, 512)
    perm[P_BQ:P_BQ + 512] = _O_BQ + _swa_q_perm()
    perm[P_BG:P_BG + 512] = _O_BG + _swa_out_perm()
    put(P_CV, _O_CV, 512); put(P_CG, _O_CG, 512)
    put(P_DU, _O_DU, 1536); put(P_DG, _O_DG, 512)
    put(P_GM, _O_GM, 4096)
    put(P_CQ, _O_CQ, 256); put(P_CK, _O_CK, 256)
    perm[P_BK:P_BK + 128] = _O_BK + _swa_k_perm()
    put(P_BV, _O_BV, 128)
    put(P_LR, _O_LRF, 16); put(P_LR + 16, _O_LRB, 16)
    return perm


def _build_half_cols():
    s = np.ones(NP_COLS, np.float32)
    for p, w in ((P_AG, 512), (P_BG, 512), (P_CG, 512), (P_DG, 512), (P_GM, 4096)):
        s[p:p + w] = 0.5
    return s


_HALF_COLS = _build_half_cols()
_IN_PERM = _build_in_perm()
_SWA_OUT_PERM = _swa_out_perm()


def _permute(w, perm, axis):
    pieces = []
    i = 0
    n = len(perm)
    while i < n:
        j = i + 1
        if perm[i] < 0:
            while j < n and perm[j] < 0:
                j += 1
            shape = list(w.shape)
            shape[axis] = j - i
            pieces.append(jnp.zeros(shape, w.dtype))
        else:
            while j < n and perm[j] == perm[j - 1] + 1:
                j += 1
            pieces.append(lax.slice_in_dim(w, int(perm[i]), int(perm[j - 1]) + 1, axis=axis))
        i = j
    return jnp.concatenate(pieces, axis=axis)


PERM_GRAIN = 32


def _win_prep_kernel(w_ref, o_ref):
    rows = w_ref.shape[1]
    for j in range(NP_COLS // LANES):
        pieces = []
        for q in range(LANES // PERM_GRAIN):
            src = int(_IN_PERM[LANES * j + PERM_GRAIN * q])
            if src < 0:
                pieces.append(jnp.zeros((rows, PERM_GRAIN), F32))
            else:
                pieces.append(w_ref[0, :, src:src + PERM_GRAIN])
        tile = jnp.concatenate(pieces, axis=1) * float(_HALF_COLS[LANES * j])
        o_ref[0, :, LANES * j:LANES * (j + 1)] = tile.astype(BF16)


def _win_prep(w_in):
    depth, d, n = w_in.shape
    grain = np.arange(0, NP_COLS, PERM_GRAIN)
    blocks = _IN_PERM.reshape(-1, PERM_GRAIN)
    assert np.all((blocks[:, :1] < 0) | (np.diff(blocks, axis=1) == 1).all(axis=1, keepdims=True)), grain
    assert np.all(_HALF_COLS.reshape(-1, LANES) == _HALF_COLS.reshape(-1, LANES)[:, :1])
    tr = 256
    return pl.pallas_call(
        _win_prep_kernel,
        grid=(depth, d // tr),
        in_specs=[pl.BlockSpec((1, tr, n), lambda l, i: (l, i, 0))],
        out_specs=pl.BlockSpec((1, tr, NP_COLS), lambda l, i: (l, i, 0)),
        out_shape=jax.ShapeDtypeStruct((depth, d, NP_COLS), BF16),
        compiler_params=_cparams(("parallel", "parallel")),
        name="win_prep",
    )(w_in)


def _cparams(sem):
    return pltpu.CompilerParams(dimension_semantics=sem, vmem_limit_bytes=VMEM_LIMIT)


def _sigmoid(x):
    return 1.0 / (1.0 + jnp.exp(-x))


def _sigmoid_tanh(x):
    return 0.5 * jnp.tanh(0.5 * x) + 0.5


def _split3(a):
    hi = a.astype(BF16)
    r1 = a - hi.astype(F32)
    mid = r1.astype(BF16)
    lo = (r1 - mid.astype(F32)).astype(BF16)
    return hi, mid, lo


def _dot(a, b):
    return jnp.dot(a, b, preferred_element_type=F32)


def _dot_nt(a, b):
    return lax.dot_general(a, b, (((1,), (1,)), ((), ())), preferred_element_type=F32)


def _dot_tn(a, b):
    return lax.dot_general(a, b, (((0,), (0,)), ((), ())), preferred_element_type=F32)


def _dot_x3(a, b):
    ah, am, _ = _split3(a)
    bh, bm, _ = _split3(b)
    return _dot(ah, bh) + (_dot(ah, bm) + _dot(am, bh))


def _mod_kernel(c_ref, w_ref, b_ref, o_ref):
    c = c_ref[...]
    s = c * _sigmoid(c)
    o_ref[0] = _dot_x3(s, w_ref[0]) + b_ref[0]


def _modulation(cvec, w_mod, b_mod):
    depth, d, n = w_mod.shape
    tn = 512
    return pl.pallas_call(
        _mod_kernel,
        grid=(depth, n // tn),
        in_specs=[pl.BlockSpec((8, d), lambda l, j: (0, 0)),
                  pl.BlockSpec((1, d, tn), lambda l, j: (l, 0, j)),
                  pl.BlockSpec((1, 1, tn), lambda l, j: (l, 0, j))],
        out_specs=pl.BlockSpec((1, 8, tn), lambda l, j: (l, 0, j)),
        out_shape=jax.ShapeDtypeStruct((depth, 8, n), F32),
        compiler_params=_cparams(("parallel", "parallel")),
        name="modulation",
    )(cvec, w_mod, b_mod.reshape(depth, 1, n))


def _inproj_kernel(x_ref, nw_ref, sc_ref, sh_ref, w_ref, o_ref, h_ref):
    @pl.when(pl.program_id(2) == 0)
    def _():
        x = x_ref[0]
        ms = jnp.mean(x * x, axis=-1, keepdims=True)
        y = x * lax.rsqrt(ms + EPS) * nw_ref[...]
        h_ref[...] = (y * (1.0 + sc_ref[0]) + sh_ref[0]).astype(BF16)

    o_ref[0] = _dot(h_ref[...], w_ref[...]).astype(BF16)


def _inproj(x, norm_w, scale, shift, w_perm, layer):
    b, t, d = x.shape
    n = w_perm.shape[2]
    tm = min(t, 2048)
    tn = 1024
    return pl.pallas_call(
        _inproj_kernel,
        grid=(b, t // tm, n // tn),
        in_specs=[pl.BlockSpec((1, tm, d), lambda bi, i, j: (bi, i, 0)),
                  pl.BlockSpec((1, d), lambda bi, i, j: (0, 0)),
                  pl.BlockSpec((1, 1, d), lambda bi, i, j: (bi, 0, 0)),
                  pl.BlockSpec((1, 1, d), lambda bi, i, j: (bi, 0, 0)),
                  pl.BlockSpec((None, d, tn), lambda bi, i, j: (layer, 0, j))],
        out_specs=pl.BlockSpec((1, tm, tn), lambda bi, i, j: (bi, i, j)),
        out_shape=jax.ShapeDtypeStruct((b, t, n), BF16),
        scratch_shapes=[pltpu.VMEM((tm, d), BF16)],
        compiler_params=_cparams(("parallel", "parallel", "arbitrary")),
        name="inproj",
    )(x, norm_w.reshape(1, d), scale, shift, w_perm)


def _merge_kernel(ya, yb, yc, yd, ga, gb, gc, gd, m0, m1, m2, m3, wb_ref, wo_ref, g_ref, x_ref, fw_ref,
                  o_ref, *, final):
    acc = None
    for i, (y, g, gm) in enumerate(((ya, ga, m0), (yb, gb, m1), (yc, gc, m2), (yd, gd, m3))):
        hg = g[0].astype(F32)
        yg = y[0].astype(F32) * hg
        ys = (yg + yg * jnp.tanh(hg)).astype(BF16)
        hp = _dot(ys, wb_ref[i])
        t = hp + hp * jnp.tanh(gm[0].astype(F32))
        acc = t if acc is None else acc + t
    out = x_ref[0] + g_ref[0] * _dot(acc.astype(BF16), wo_ref[...])
    if final:
        ms = jnp.mean(out * out, axis=-1, keepdims=True)
        out = out * lax.rsqrt(ms + EPS) * fw_ref[...]
    o_ref[0] = out


def _merge(ys, proj, w_branch, w_out, gate, x, final_w, final):
    b, t, d = x.shape
    bw = BRANCH_WIDTH
    tm = min(t, 512)
    yspec = pl.BlockSpec((1, tm, bw), lambda bi, i: (bi, i, 0))

    def pspec(col, width):
        blk = col // width
        return pl.BlockSpec((1, tm, width), lambda bi, i: (bi, i, blk))

    in_specs = ([yspec] * 4
                + [pspec(P_AG, bw), pspec(P_BG, bw), pspec(P_CG, bw), pspec(P_DG, bw)]
                + [pspec(P_GM + k * d, d) for k in range(N_BRANCH)]
                + [pl.BlockSpec((N_BRANCH, bw, d), lambda bi, i: (0, 0, 0)),
                   pl.BlockSpec((d, d), lambda bi, i: (0, 0)),
                   pl.BlockSpec((1, 1, d), lambda bi, i: (bi, 0, 0)),
                   pl.BlockSpec((1, tm, d), lambda bi, i: (bi, i, 0)),
                   pl.BlockSpec((1, d), lambda bi, i: (0, 0))])
    return pl.pallas_call(
        functools.partial(_merge_kernel, final=final),
        grid=(b, t // tm),
        in_specs=in_specs,
        out_specs=pl.BlockSpec((1, tm, d), lambda bi, i: (bi, i, 0)),
        out_shape=jax.ShapeDtypeStruct((b, t, d), F32),
        compiler_params=_cparams(("parallel", "parallel")),
        name="merge",
    )(*ys, *([proj] * 8), w_branch, w_out, gate, x, final_w.reshape(1, d))


def _rope_kernel(q_ref, k_ref, cos_ref, sin_ref, qo_ref, ko_ref):
    cos = cos_ref[...]
    sin = sin_ref[...]

    def rot(x):
        return x * cos + pltpu.roll(x, 64, 1) * sin

    for t in range(4):
        q = q_ref[0, :, 128 * t:128 * (t + 1)].astype(F32)
        qo_ref[0, :, 128 * t:128 * (t + 1)] = (rot(q) * (HEAD_DIM ** -0.5 * LOG2E)).astype(BF16)
    ko_ref[0] = rot(k_ref[0].astype(F32)).astype(BF16)


def _rope(proj, cos128, sin128):
    b, t, _ = proj.shape
    tm = min(t, 1024)
    return pl.pallas_call(
        _rope_kernel,
        grid=(b, t // tm),
        in_specs=[pl.BlockSpec((1, tm, 512), lambda bi, i: (bi, i, P_BQ // 512)),
                  pl.BlockSpec((1, tm, 128), lambda bi, i: (bi, i, P_BK // 128)),
                  pl.BlockSpec((tm, 128), lambda bi, i: (i, 0)),
                  pl.BlockSpec((tm, 128), lambda bi, i: (i, 0))],
        out_specs=[pl.BlockSpec((1, tm, 512), lambda bi, i: (bi, i, 0)),
                   pl.BlockSpec((1, tm, 128), lambda bi, i: (bi, i, 0))],
        out_shape=[jax.ShapeDtypeStruct((b, t, 512), BF16), jax.ShapeDtypeStruct((b, t, 128), BF16)],
        compiler_params=_cparams(("parallel", "parallel")),
        name="rope",
    )(proj, proj, cos128, sin128)


def _rope_tables(length):
    t = jnp.arange(length, dtype=jnp.int32)
    row = (t // GRID_W).astype(F32)
    col = (t % GRID_W).astype(F32)
    n_freq = HEAD_DIM // 4
    inv = ROPE_BASE ** (-jnp.arange(n_freq, dtype=F32) / n_freq)
    ang = jnp.concatenate([row[:, None] * inv, col[:, None] * inv], axis=-1)
    cos, sin = jnp.cos(ang), jnp.sin(ang)
    return jnp.tile(cos, (1, 4)), jnp.concatenate([-sin, -sin, sin, sin], axis=-1)


def _swa_kernel(q_ref, k_ref, v_ref, kc_ref, vc_ref, sink_ref, mask_ref, o_ref, *, length):
    blk = SW_BLOCK
    kw = 3 * blk
    nb = length // blk
    kc = kc_ref[0]
    vc = vc_ref[0]
    lane = lax.broadcasted_iota(jnp.int32, (blk, LANES), 1)
    a_lanes = (lane // 32) % 2 == 0
    lo_lanes = lane < 64
    top = lax.broadcasted_iota(jnp.int32, (2 * blk, 1), 0) < blk
    nsb = q_ref.shape[1] // blk
    units = [(sb, t) for sb in range(nsb) for t in range(4)]

    def stacked_q(sb, t):
        qt = q_ref[0, sb * blk:(sb + 1) * blk, 128 * t:128 * (t + 1)]
        zero = jnp.zeros_like(qt)
        return jnp.concatenate([jnp.where(a_lanes, qt, zero), jnp.where(a_lanes, zero, qt)], axis=0)

    sc_all = _dot_nt(jnp.concatenate([stacked_q(sb, t) for sb, t in units], axis=0), kc)
    o_win, pcs, rdens = [], [], []

    for u, (sb, t) in enumerate(units):
        i = pl.program_id(1) * nsb + sb
        start = pl.multiple_of(jnp.clip((i - 1) * blk, 0, length - kw), blk)
        variant = jnp.where(i == 0, 0, jnp.where(i == nb - 1, 2, 1))
        sw = _dot_nt(stacked_q(sb, t), k_ref[0, pl.ds(start, kw), :]) + mask_ref[variant]
        sc = sc_all[2 * blk * u:2 * blk * (u + 1)]
        sk = jnp.where(top, sink_ref[t:t + 1, 0:1], sink_ref[t + 4:t + 5, 0:1])
        m = jnp.maximum(jnp.maximum(jnp.max(sw, axis=-1, keepdims=True), jnp.max(sc, axis=-1, keepdims=True)), sk)
        pw = jnp.exp2(sw - m)
        pc = jnp.exp2(sc - m)
        den = jnp.sum(pw, axis=-1, keepdims=True) + jnp.sum(pc, axis=-1, keepdims=True) + jnp.exp2(sk - m)
        o_win.append(_dot(pw.astype(BF16), v_ref[0, pl.ds(start, kw), :]))
        pcs.append(pc.astype(BF16))
        rdens.append(1.0 / den)
    o_ctx = _dot(jnp.concatenate(pcs, axis=0), vc)
    for u, (sb, t) in enumerate(units):
        o = (o_win[u] + o_ctx[2 * blk * u:2 * blk * (u + 1)]) * rdens[u]
        o_ref[0, sb * blk:(sb + 1) * blk, 128 * t:128 * (t + 1)] = jnp.where(lo_lanes, o[:blk], o[blk:]).astype(BF16)


def _swa(qr, kr, proj_l, kc_rot, proj_c, sink128):
    b, length, _ = qr.shape
    lc = proj_c.shape[1]
    assert length >= 3 * SW_BLOCK
    tq = 4 * SW_BLOCK if length % (4 * SW_BLOCK) == 0 else SW_BLOCK
    r = np.arange(2 * SW_BLOCK)[:, None] % SW_BLOCK
    c = np.arange(3 * SW_BLOCK)[None, :]
    mask_tab = jnp.asarray(np.stack([np.where(np.abs(v * SW_BLOCK + r - c) <= SW_WINDOW, 0.0, NEG_INF)
                                     for v in range(3)]), F32)
    return pl.pallas_call(
        functools.partial(_swa_kernel, length=length),
        grid=(b, length // tq),
        in_specs=[pl.BlockSpec((1, tq, 512), lambda bi, i: (bi, i, 0)),
                  pl.BlockSpec((1, length, 128), lambda bi, i: (bi, 0, 0)),
                  pl.BlockSpec((1, length, 128), lambda bi, i: (bi, 0, P_BV // 128)),
                  pl.BlockSpec((1, lc, 128), lambda bi, i: (bi, 0, P_BK // 128)),
                  pl.BlockSpec((1, lc, 128), lambda bi, i: (bi, 0, P_BV // 128)),
                  pl.BlockSpec((8, 128), lambda bi, i: (0, 0)),
                  pl.BlockSpec((3, 2 * SW_BLOCK, 3 * SW_BLOCK), lambda bi, i: (0, 0, 0))],
        out_specs=pl.BlockSpec((1, tq, 512), lambda bi, i: (bi, i, 0)),
        out_shape=jax.ShapeDtypeStruct((b, length, 512), BF16),
        compiler_params=_cparams(("parallel", "arbitrary")),
        name="swa",
    )(qr, kr, proj_l, kc_rot, proj_c, sink128, mask_tab)


def _ctx_attn_kernel(q_ref, k_ref, v_ref, sink_ref, o_ref, *, swa):
    t_len = q_ref.shape[1]
    lane = lax.broadcasted_iota(jnp.int32, (t_len, LANES), 1)
    lo_lanes = lane < 64
    a_lanes = ((lane // 32) % 2 == 0) if swa else lo_lanes
    top = lax.broadcasted_iota(jnp.int32, (2 * t_len, 1), 0) < t_len
    for t in range(4):
        qt = (q_ref[0, :, 128 * t:128 * (t + 1)].astype(F32) * HEAD_DIM ** -0.5).astype(BF16)
        zero = jnp.zeros_like(qt)
        qq = jnp.concatenate([jnp.where(a_lanes, qt, zero), jnp.where(a_lanes, zero, qt)], axis=0)
        if swa:
            kt, vt = k_ref[0], v_ref[0]
        else:
            kt, vt = k_ref[0, :, 128 * t:128 * (t + 1)], v_ref[0, :, 128 * t:128 * (t + 1)]
        s = _dot_nt(qq, kt)
        m = jnp.max(s, axis=-1, keepdims=True)
        if swa:
            sk = jnp.where(top, sink_ref[t:t + 1, 0:1], sink_ref[t + 4:t + 5, 0:1])
            m = jnp.maximum(m, sk)
        p = jnp.exp(s - m)
        den = jnp.sum(p, axis=-1, keepdims=True)
        if swa:
            den = den + jnp.exp(sk - m)
        o = _dot(p.astype(BF16), vt) / den
        o_ref[0, :, 128 * t:128 * (t + 1)] = jnp.where(lo_lanes, o[:t_len], o[t_len:]).astype(BF16)


def _ctx_attn(proj_c, sink128, swa):
    b, lc, _ = proj_c.shape
    if swa:
        qs = pl.BlockSpec((1, lc, 512), lambda bi: (bi, 0, P_BQ // 512))
        ks = pl.BlockSpec((1, lc, 128), lambda bi: (bi, 0, P_BK // 128))
        vs = pl.BlockSpec((1, lc, 128), lambda bi: (bi, 0, P_BV // 128))
    else:
        qs = pl.BlockSpec((1, lc, 512), lambda bi: (bi, 0, P_AQ // 512))
        ks = pl.BlockSpec((1, lc, 512), lambda bi: (bi, 0, P_AK // 512))
        vs = pl.BlockSpec((1, lc, 512), lambda bi: (bi, 0, P_AV // 512))
    return pl.pallas_call(
        functools.partial(_ctx_attn_kernel, swa=swa),
        grid=(b,),
        in_specs=[qs, ks, vs, pl.BlockSpec((8, 128), lambda bi: (0, 0))],
        out_specs=pl.BlockSpec((1, lc, 512), lambda bi: (bi, 0, 0)),
        out_shape=jax.ShapeDtypeStruct((b, lc, 512), BF16),
        compiler_params=_cparams(("parallel",)),
        name="ctx_attn_swa" if swa else "ctx_attn_na",
    )(proj_c, proj_c, proj_c, sink128)


def _natten_kernel(q_ref, k_ref, v_ref, kc_ref, vc_ref, bias_ref, o_ref, *, rows, rows_per_step):
    blk = pl.program_id(2)
    kc = kc_ref[0]
    vc = vc_ref[0]
    w = GRID_W
    nkeys = NA_KR * w
    lo_all = lax.broadcasted_iota(jnp.int32, (rows_per_step * w, LANES), 1) < 64
    lo_row = lax.broadcasted_iota(jnp.int32, (w, LANES), 1) < 64
    q_all = (q_ref[0].astype(F32) * (HEAD_DIM ** -0.5 * LOG2E)).astype(BF16)
    zero = jnp.zeros_like(q_all)
    q_lo = jnp.where(lo_all, q_all, zero)
    q_hi = jnp.where(lo_all, zero, q_all)
    sc_lo = _dot_nt(q_lo, kc)
    sc_hi = _dot_nt(q_hi, kc)
    o_rows, pc_lo, pc_hi, rd_lo, rd_hi = [], [], [], [], []

    def scores(rr):
        r = blk * rows_per_step + rr
        rs = jnp.clip(r - NA_KR // 2, 0, rows - NA_KR)
        off = rs - r + (NA_KR - 1)
        k0 = pl.multiple_of(rs * w, w)
        sl = slice(rr * w, (rr + 1) * w)
        qq = jnp.concatenate([q_lo[sl], q_hi[sl]], axis=0)
        s = _dot_nt(qq, k_ref[0, pl.ds(k0, nkeys), :]) + bias_ref[off, 0]
        sc = jnp.concatenate([sc_lo[sl], sc_hi[sl]], axis=0)
        return s, sc, k0

    ahead = 2
    queue = [scores(rr) for rr in range(ahead)]
    for rr in range(rows_per_step):
        s, sc, k0 = queue.pop(0)
        if rr + ahead < rows_per_step:
            queue.append(scores(rr + ahead))
        vrows = v_ref[0, pl.ds(k0, nkeys), :]
        m = jnp.maximum(jnp.max(s, axis=-1, keepdims=True), jnp.max(sc, axis=-1, keepdims=True))
        p = jnp.exp2(s - m)
        pc = jnp.exp2(sc - m)
        rden = 1.0 / (jnp.sum(p, axis=-1, keepdims=True) + jnp.sum(pc, axis=-1, keepdims=True))
        o = _dot(p.astype(BF16), vrows)
        pcb = pc.astype(BF16)
        o_rows.append(jnp.where(lo_row, o[:w], o[w:]))
        rd_lo.append(rden[:w])
        rd_hi.append(rden[w:])
        pc_lo.append(pcb[:w])
        pc_hi.append(pcb[w:])
    oc = jnp.where(lo_all, _dot(jnp.concatenate(pc_lo, axis=0), vc), _dot(jnp.concatenate(pc_hi, axis=0), vc))
    rd = jnp.where(lo_all, jnp.concatenate(rd_lo, axis=0), jnp.concatenate(rd_hi, axis=0))
    o_ref[0] = ((jnp.concatenate(o_rows, axis=0) + oc) * rd).astype(BF16)


def _natten(proj_l, proj_c, bias_tab):
    b, length, _ = proj_l.shape
    lc = proj_c.shape[1]
    rows = length // GRID_W
    assert rows >= NA_KR
    rps = 16 if rows % 16 == 0 else 8
    tq = rps * GRID_W
    nk = NA_KR * GRID_W
    return pl.pallas_call(
        functools.partial(_natten_kernel, rows=rows, rows_per_step=rps),
        grid=(b, 4, rows // rps),
        in_specs=[pl.BlockSpec((1, tq, 128), lambda bi, hp, i: (bi, i, P_AQ // 128 + hp)),
                  pl.BlockSpec((1, length, 128), lambda bi, hp, i: (bi, 0, P_AK // 128 + hp)),
                  pl.BlockSpec((1, length, 128), lambda bi, hp, i: (bi, 0, P_AV // 128 + hp)),
                  pl.BlockSpec((1, lc, 128), lambda bi, hp, i: (bi, 0, P_AK // 128 + hp)),
                  pl.BlockSpec((1, lc, 128), lambda bi, hp, i: (bi, 0, P_AV // 128 + hp)),
                  pl.BlockSpec((NA_KR, 1, 2 * GRID_W, nk), lambda bi, hp, i: (0, hp, 0, 0))],
        out_specs=pl.BlockSpec((1, tq, 128), lambda bi, hp, i: (bi, i, hp)),
        out_shape=jax.ShapeDtypeStruct((b, length, 512), BF16),
        compiler_params=_cparams(("parallel", "parallel", "arbitrary")),
        name="natten",
    )(proj_l, proj_l, proj_l, proj_c, proj_c, bias_tab)


def _natten_bias_table(rpb):
    w = GRID_W
    nd = 2 * NA_KC - 1
    qc = np.arange(w)[:, None]
    kcol = np.arange(w)[None, :]
    cs = np.clip(qc - NA_KC // 2, 0, w - NA_KC)
    cmask = (kcol >= cs) & (kcol < cs + NA_KC)
    period = nd + w
    padded = jnp.pad(rpb.astype(F32), ((0, 0), (0, 0), (0, w)))
    flat = jnp.tile(padded, (1, 1, w))[:, :, :w * (period - 1)]
    t = flat.reshape(NA_HEADS, 2 * NA_KR - 1, w, period - 1)[..., NA_KC - 1:NA_KC - 1 + w]
    g = jnp.stack([t[:, off:off + NA_KR] for off in range(NA_KR)])
    g = jnp.where(cmask[None, None, None], g * LOG2E, NEG_INF)
    return g.transpose(0, 1, 3, 2, 4).reshape(NA_KR, NA_HEADS // 2, 2 * w, NA_KR * w)


def _gla_kernel(*refs, rev, ncb, final):
    if final:
        (q_ref, k_ref, v_ref, lr_ref, whi_ref, wlo_ref, b_ref, s0_ref, of_ref, gnw_ref,
         o_ref, sfin_ref, st_ref) = refs
    else:
        (q_ref, k_ref, v_ref, lr_ref, whi_ref, wlo_ref, b_ref, s0_ref,
         o_ref, sfin_ref, st_ref) = refs
    ch = GLA_CHUNK
    nstream = q_ref.shape[0]

    @pl.when(pl.program_id(1) == 0)
    def _():
        st_ref[...] = s0_ref[:, 0]

    tb = ncb * ch
    rix = lax.broadcasted_iota(jnp.int32, (ch, ch), 0)
    cix = lax.broadcasted_iota(jnp.int32, (ch, ch), 1)
    tri = (rix <= cix) if rev else (rix >= cix)
    lo_lanes = lax.broadcasted_iota(jnp.int32, (tb, LANES), 1) < 64
    end = 0 if rev else ch - 1
    pos = lax.broadcasted_iota(jnp.int32, (tb, 1), 0) % ch
    order = [(ncb - 1 - cc) if rev else cc for cc in range(ncb)]

    def prefix(s):
        lr = lr_ref[s]
        arg = _dot(lr, whi_ref[0]) + _dot(lr, wlo_ref[0]) + b_ref[0]
        cum = (jnp.minimum(arg, 0.0) - jnp.log(1.0 + jnp.exp(-jnp.abs(arg)))) * (1.0 / GLA_TAU)
        step = 1
        while step < ch:
            if rev:
                cum = cum + jnp.where(pos < ch - step, pltpu.roll(cum, tb - step, 0), 0.0)
            else:
                cum = cum + jnp.where(pos >= step, pltpu.roll(cum, step, 0), 0.0)
            step *= 2
        q = q_ref[s].astype(F32)
        k = k_ref[s].astype(F32)
        qd = (q * (HEAD_DIM ** -0.5) * jnp.exp(cum)).astype(BF16)
        kd = (k * jnp.exp(-cum)).astype(BF16)
        zero = jnp.zeros_like(qd)
        return cum, k, kd, (jnp.where(lo_lanes, qd, zero), jnp.where(lo_lanes, zero, qd))

    def intra_chunks(s, pre):
        cum, k, kd, qd_h = pre
        res = {}
        for c in order:
            sl = slice(c * ch, (c + 1) * ch)
            cum_c = cum[sl]
            cum_end = cum_c[end:end + 1, :]
            kdec = (k[sl] * jnp.exp(cum_end - cum_c)).astype(BF16)
            for h in range(2):
                vh = v_ref[s, sl, 128 * h:128 * (h + 1)]
                a = jnp.where(tri, _dot_nt(qd_h[h][sl], kd[sl]), 0.0)
                res[c, h] = (_dot(a.astype(BF16), vh), _dot_tn(vh, kdec), jnp.exp(cum_end))
        return res

    def recurrence(s, pre, res):
        qd_h = pre[3]
        st = [st_ref[s, 0], st_ref[s, 1]]
        for c in order:
            sl = slice(c * ch, (c + 1) * ch)
            for h in range(2):
                o_intra, kv_t, gdec = res[c, h]
                o = o_intra + _dot_nt(qd_h[h][sl], st[h].astype(BF16))
                st[h] = st[h] * gdec + kv_t
                if final:
                    o = o + of_ref[s, sl, 128 * h:128 * (h + 1)]
                    ms = jnp.mean(o * o, axis=-1, keepdims=True)
                    o = o * lax.rsqrt(ms + EPS) * gnw_ref[...]
                    o_ref[s, sl, 128 * h:128 * (h + 1)] = o.astype(BF16)
                else:
                    o_ref[s, sl, 128 * h:128 * (h + 1)] = o
        st_ref[s, 0] = st[0]
        st_ref[s, 1] = st[1]

    pres = [prefix(s) for s in range(nstream)]
    ress = [intra_chunks(s, pres[s]) for s in range(nstream)]
    for s in range(nstream):
        recurrence(s, pres[s], ress[s])
    sfin_ref[:, 0] = st_ref[...]


def _gla_pass(proj, w_pad_hi, w_pad_lo, b_pad, s0, rev, o_fwd=None, gnw=None):
    b, t, _ = proj.shape
    final = o_fwd is not None
    tb = min(t, 512)
    nblk = t // tb
    ncb = tb // GLA_CHUNK
    d = 1 if rev else 0

    def bi_map(i):
        return (nblk - 1 - i) if rev else i

    in_specs = [pl.BlockSpec((b, tb, 128), lambda p, i: (0, bi_map(i), P_CQ // 128 + p)),
                pl.BlockSpec((b, tb, 128), lambda p, i: (0, bi_map(i), P_CK // 128 + p)),
                pl.BlockSpec((b, tb, 256), lambda p, i: (0, bi_map(i), P_CV // 256 + p)),
                pl.BlockSpec((b, tb, 128), lambda p, i: (0, bi_map(i), P_LR // 128)),
                pl.BlockSpec((1, 128, 128), lambda p, i: (2 * d + p, 0, 0)),
                pl.BlockSpec((1, 128, 128), lambda p, i: (2 * d + p, 0, 0)),
                pl.BlockSpec((1, 1, 128), lambda p, i: (2 * d + p, 0, 0)),
                pl.BlockSpec((b, 1, 2, 128, 128), lambda p, i: (0, p, 0, 0, 0))]
    args = [proj, proj, proj, proj, w_pad_hi, w_pad_lo, b_pad, s0]
    if final:
        in_specs += [pl.BlockSpec((b, tb, 256), lambda p, i: (0, bi_map(i), p)),
                     pl.BlockSpec((1, 128), lambda p, i: (0, 0))]
        args += [o_fwd, gnw.reshape(1, 128)]
    out, s_fin = pl.pallas_call(
        functools.partial(_gla_kernel, rev=rev, ncb=ncb, final=final),
        grid=(2, nblk),
        in_specs=in_specs,
        out_specs=[pl.BlockSpec((b, tb, 256), lambda p, i: (0, bi_map(i), p)),
                   pl.BlockSpec((b, 1, 2, 128, 128), lambda p, i: (0, p, 0, 0, 0))],
        out_shape=[jax.ShapeDtypeStruct((b, t, 512), BF16 if final else F32),
                   jax.ShapeDtypeStruct((b, 2, 2, 128, 128), F32)],
        scratch_shapes=[pltpu.VMEM((b, 2, 128, 128), F32)],
        compiler_params=_cparams(("parallel", "arbitrary")),
        name="gla_rev" if rev else "gla_fwd",
    )(*args)
    return out, s_fin


def _gla_decay_weights(w_alpha_up, b_alpha):
    w = jnp.zeros((2, 2, 128, 128), F32)
    for d in range(2):
        for p in range(2):
            w = w.at[d, p, 16 * d:16 * d + 16, :].set(w_alpha_up[d][:, 128 * p:128 * (p + 1)].astype(F32))
    w = w.reshape(4, 128, 128)
    hi = w.astype(BF16)
    lo = (w - hi.astype(F32)).astype(BF16)
    return hi, lo, b_alpha.astype(F32).reshape(4, 1, 128)


def _hy_pre_kernel(u0, u1, u2, p0, p1, p2, n0, n1, n2, w_ref, b_ref, x0_ref, z_ref, *, nblk):
    i = pl.program_id(1)
    tm = u0.shape[1]
    row = lax.broadcasted_iota(jnp.int32, (tm, 1), 0)
    has_prev = (i > 0).astype(F32)
    has_next = (i < nblk - 1).astype(F32)

    def conv(u_ref, p_ref, n_ref, j):
        u = u_ref[0].astype(F32)
        prev = p_ref[0, 7:8, :].astype(F32) * has_prev
        nxt = n_ref[0, 0:1, :].astype(F32) * has_next
        up = jnp.where(row == 0, prev, pltpu.roll(u, 1, 0))
        dn = jnp.where(row == tm - 1, nxt, pltpu.roll(u, tm - 1, 0))
        w = w_ref[:, 512 * j:512 * (j + 1)]
        return up * w[0:1] + u * w[1:2] + dn * w[2:3] + b_ref[:, 512 * j:512 * (j + 1)]

    x0_ref[0] = conv(u0, p0, n0, 0).astype(BF16)
    z_ref[0] = (conv(u1, p1, n1, 1) * conv(u2, p2, n2, 2)).astype(BF16)


def _hy_pre(proj, conv_w, conv_b):
    b, t, _ = proj.shape
    tm = min(t, 512)
    nblk = t // tm
    hb = tm // 8
    nrb = t // 8
    c0 = P_DU // 512
    main = [pl.BlockSpec((1, tm, 512), functools.partial(lambda bi, i, j: (bi, i, c0 + j), j=j)) for j in range(3)]
    prev = [pl.BlockSpec((1, 8, 512), functools.partial(lambda bi, i, j: (bi, jnp.maximum(i * hb - 1, 0), c0 + j), j=j))
            for j in range(3)]
    nxt = [pl.BlockSpec((1, 8, 512),
                        functools.partial(lambda bi, i, j: (bi, jnp.minimum((i + 1) * hb, nrb - 1), c0 + j), j=j))
           for j in range(3)]
    return pl.pallas_call(
        functools.partial(_hy_pre_kernel, nblk=nblk),
        grid=(b, nblk),
        in_specs=main + prev + nxt + [pl.BlockSpec((3, 1536), lambda bi, i: (0, 0)),
                                      pl.BlockSpec((1, 1536), lambda bi, i: (0, 0))],
        out_specs=[pl.BlockSpec((1, tm, 512), lambda bi, i: (bi, i, 0))] * 2,
        out_shape=[jax.ShapeDtypeStruct((b, t, 512), BF16)] * 2,
        compiler_params=_cparams(("parallel", "parallel")),
        name="hyena_pre",
    )(*([proj] * 9), conv_w.astype(F32), conv_b.astype(F32).reshape(1, 1536))


def _hy_filter_kernel(z_ref, w1, b1, w2, b2, w3, b3, fr, wo, dl, h_ref, l1_ref, *, tm, t_len):
    i = pl.program_id(0)
    f = fr[...]
    half = tm // 2
    z = z_ref[...]
    hh = jnp.concatenate([z[:half], z[half:]], axis=1)
    hh = jnp.sin(f * (_dot_x3(hh, w1[...]) + b1[...]))
    hh = jnp.sin(f * (_dot_x3(hh, w2[...]) + b2[...]))
    hh = jnp.sin(f * (_dot_x3(hh, w3[...]) + b3[...]))
    hh = _dot_x3(hh, wo[...])
    hh = jnp.concatenate([hh[:, :2 * HY_WIDTH], hh[:, 2 * HY_WIDTH:]], axis=0)
    row = i * tm + lax.broadcasted_iota(jnp.int32, (tm, 1), 0)
    t = row.astype(F32) / (t_len - 1)
    decay = jnp.exp(-t * dl[...])
    h_f = hh[:, :HY_WIDTH] * decay
    h_b = jnp.where(row == 0, 0.0, hh[:, HY_WIDTH:] * decay)
    h_ref[0] = h_f.astype(BF16)
    h_ref[1] = h_b.astype(BF16)
    part = jnp.sum(jnp.abs(h_f) + jnp.abs(h_b), axis=0, keepdims=True)

    @pl.when(i == 0)
    def _():
        l1_ref[...] = part

    @pl.when(i > 0)
    def _():
        l1_ref[...] = l1_ref[...] + part


def _hy_filter(t_len, w1, b1, w2, b2, w3, b3, freq, wout):
    tm = min(t_len, 512)
    t = jnp.linspace(0.0, 1.0, t_len, dtype=F32)[:, None]
    bands = (HY_EMB - 1) // 2
    w_ang = 2.0 * math.pi * jnp.arange(t_len, dtype=F32)[:, None] / t_len
    f = jnp.linspace(1e-4, bands - 1, bands, dtype=F32)[None, :]
    z = jnp.concatenate([t, jnp.cos(f * w_ang), -jnp.sin(f * w_ang),
                         jnp.zeros((t_len, LANES - HY_EMB), F32)], axis=-1)
    w1p = jnp.concatenate([w1.astype(F32), jnp.zeros((LANES - HY_EMB, HY_FFN), F32)], axis=0)
    deltas = jnp.abs(jnp.linspace(math.log(HY_TARGET) / HY_SLOW_PCT, math.log(HY_TARGET) / HY_FAST_PCT,
                                  HY_WIDTH, dtype=F32))[None, :]
    full = lambda shape: pl.BlockSpec(shape, lambda i: tuple(0 for _ in shape))
    r = lambda v: jnp.tile(v.astype(F32).reshape(1, -1), (1, 2))

    def bdiag(w):
        w = w.astype(F32)
        zero = jnp.zeros_like(w)
        return jnp.concatenate([jnp.concatenate([w, zero], axis=1), jnp.concatenate([zero, w], axis=1)], axis=0)

    hid = 2 * HY_FFN
    return pl.pallas_call(
        functools.partial(_hy_filter_kernel, tm=tm, t_len=t_len),
        grid=(t_len // tm,),
        in_specs=[pl.BlockSpec((tm, LANES), lambda i: (i, 0)),
                  full((2 * LANES, hid)), full((1, hid)), full((hid, hid)), full((1, hid)),
                  full((hid, hid)), full((1, hid)), full((1, hid)), full((hid, 4 * HY_WIDTH)),
                  full((1, HY_WIDTH))],
        out_specs=[pl.BlockSpec((2, tm, HY_WIDTH), lambda i: (0, i, 0)),
                   pl.BlockSpec((1, HY_WIDTH), lambda i: (0, 0))],
        out_shape=[jax.ShapeDtypeStruct((2, t_len, HY_WIDTH), BF16), jax.ShapeDtypeStruct((1, HY_WIDTH), F32)],
        compiler_params=_cparams(("arbitrary",)),
        name="hyena_filter",
    )(z, bdiag(w1p), r(b1), bdiag(w2), r(b2), bdiag(w3), r(b3), r(freq), bdiag(wout), deltas)


BF16_TILE_ROWS = 16
FFT_K1_PER_STEP = 4


def _used_k1(n1):
    need = n1 // 2 + 1
    return min(n1, -(-need // BF16_TILE_ROWS) * BF16_TILE_ROWS)


def _dft_tables(t_len):
    n = 2 * t_len
    n2 = FFT_N2
    n1 = n // n2
    half = n1 // 2
    k1u = _used_k1(n1)
    k1 = np.arange(k1u)[:, None]
    j1 = np.arange(half)[None, :]
    ang1 = 2.0 * np.pi * ((k1 * j1) % n1) / n1
    f1 = np.concatenate([np.cos(ang1), -np.sin(ang1)], axis=0)
    wgt = np.where((k1 == 0) | (k1 == half), 1.0, np.where(k1 < half, 2.0, 0.0))
    i1 = np.concatenate([(wgt * np.cos(ang1)).T, (-wgt * np.sin(ang1)).T], axis=1)
    n1 = k1u
    a = np.arange(n2)
    ang2 = 2.0 * np.pi * ((a[:, None] * a[None, :]) % n2) / n2
    fr, fi = np.cos(ang2), -np.sin(ang2)
    fwd = np.block([[fr, -fi], [fi, fr]])
    inv = np.block([[fr, fi], [-fi, fr]])
    angt = 2.0 * np.pi * ((np.arange(n1)[:, None] * a[None, :]) % n) / n
    twc = np.cos(angt)[:, :, None]
    tws = np.sin(angt)[:, :, None]
    return (jnp.asarray(f1, F32).astype(BF16), jnp.asarray(i1, F32).astype(BF16), jnp.asarray(fwd, F32).astype(BF16),
            jnp.asarray(inv, F32).astype(BF16),
            jnp.broadcast_to(jnp.asarray(twc, F32), (n1, n2, LANES)),
            jnp.broadcast_to(jnp.asarray(tws, F32), (n1, n2, LANES)))


FFT_GROUP = BF16_TILE_ROWS


def _swap_table():
    g = FFT_GROUP
    r = np.arange(g * g)
    p = np.zeros((g * g, g * g), np.float32)
    p[r, (r % g) * g + r // g] = 1.0
    return jnp.asarray(p, F32).astype(BF16)


def _regroup(p, ref, n_rows, width):
    g = FFT_GROUP
    swapped = [_dot(p, ref[0, g * blk:g * (blk + 1)].reshape(g * g, width)).astype(BF16)
               for blk in range(n_rows // g)]
    return [jnp.concatenate([s[g * j:g * (j + 1)] for s in swapped], axis=0) for j in range(g)]


def _hy_s1_kernel(z_ref, f_ref, p_ref, xr_ref, xi_ref):
    g = FFT_GROUP
    half, width = z_ref.shape[1], z_ref.shape[3]
    k1u = xr_ref.shape[1]
    p = p_ref[...]
    ys = [_dot(f_ref[...], zj).astype(BF16) for zj in _regroup(p, z_ref, half, width)]
    for part, ref in enumerate((xr_ref, xi_ref)):
        for kb in range(k1u // g):
            r0 = part * k1u + g * kb
            rows = jnp.concatenate([y[r0:r0 + g] for y in ys], axis=0)
            ref[0, g * kb:g * (kb + 1)] = _dot(p, rows).astype(BF16).reshape(g, g, width)


def _hy_s1(z4, f1, swap):
    b, half, n2, w = z4.shape
    g = FFT_GROUP
    k1u = f1.shape[0] // 2
    assert half % g == 0 and n2 % g == 0 and k1u % g == 0
    return pl.pallas_call(
        _hy_s1_kernel,
        grid=(b, n2 // g),
        in_specs=[pl.BlockSpec((1, half, g, w), lambda bi, j: (bi, 0, j, 0)),
                  pl.BlockSpec((2 * k1u, half), lambda bi, j: (0, 0)),
                  pl.BlockSpec((g * g, g * g), lambda bi, j: (0, 0))],
        out_specs=[pl.BlockSpec((1, k1u, g, w), lambda bi, j: (bi, 0, j, 0))] * 2,
        out_shape=[jax.ShapeDtypeStruct((b, k1u, n2, w), BF16)] * 2,
        compiler_params=_cparams(("parallel", "parallel")),
        name="hyena_dft_outer",
    )(z4, f1, swap)


def _twiddle_fwd(xr, xi, c, s):
    return xr * c + xi * s, xi * c - xr * s


def _lane_tile(ref, j, reps):
    v = ref[j]
    return jnp.concatenate([v] * reps, axis=1)


def _hy_filt_spec_kernel(xr_ref, xi_ref, twc_ref, tws_ref, fwd_ref, l1_ref, kr_ref, ki_ref, *, n):
    reps = xr_ref.shape[-1] // LANES
    n2 = FFT_N2
    scale = 1.0 / (l1_ref[...] * n)
    for j in range(FFT_K1_PER_STEP):
        c = _lane_tile(twc_ref, j, reps)
        s = _lane_tile(tws_ref, j, reps)
        parts = []
        for f in range(2):
            ar, ai = _twiddle_fwd(xr_ref[f, j].astype(F32), xi_ref[f, j].astype(F32), c, s)
            y = _dot(fwd_ref[...], jnp.concatenate([ar, ai], axis=0).astype(BF16))
            parts.append((y[:n2], y[n2:]))
        kr_ref[j] = ((parts[0][0] + parts[1][0]) * scale).astype(BF16)
        ki_ref[j] = ((parts[0][1] - parts[1][1]) * scale).astype(BF16)


def _hy_filt_spec(xr, xi, twc, tws, fwd, l1, n):
    _, n1, n2, w = xr.shape
    kb = FFT_K1_PER_STEP
    return pl.pallas_call(
        functools.partial(_hy_filt_spec_kernel, n=n),
        grid=(n1 // kb,),
        in_specs=[pl.BlockSpec((2, kb, n2, w), lambda k: (0, k, 0, 0)),
                  pl.BlockSpec((2, kb, n2, w), lambda k: (0, k, 0, 0)),
                  pl.BlockSpec((kb, n2, LANES), lambda k: (k, 0, 0)),
                  pl.BlockSpec((kb, n2, LANES), lambda k: (k, 0, 0)),
                  pl.BlockSpec((2 * n2, 2 * n2), lambda k: (0, 0)),
                  pl.BlockSpec((1, w), lambda k: (0, 0))],
        out_specs=[pl.BlockSpec((kb, n2, w), lambda k: (k, 0, 0))] * 2,
        out_shape=[jax.ShapeDtypeStruct((n1, n2, w), BF16)] * 2,
        compiler_params=_cparams(("parallel",)),
        name="hyena_filter_spectrum",
    )(xr, xi, twc, tws, fwd, l1)


def _hy_mid_kernel(xr_ref, xi_ref, twc_ref, tws_ref, fwd_ref, inv_ref, kr_ref, ki_ref, tr_ref, ti_ref):
    reps = xr_ref.shape[-1] // LANES
    n2 = FFT_N2
    for j in range(FFT_K1_PER_STEP):
        c = _lane_tile(twc_ref, j, reps)
        s = _lane_tile(tws_ref, j, reps)
        ar, ai = _twiddle_fwd(xr_ref[0, j].astype(F32), xi_ref[0, j].astype(F32), c, s)
        y = _dot(fwd_ref[...], jnp.concatenate([ar, ai], axis=0).astype(BF16))
        yr, yi = y[:n2], y[n2:]
        kr = kr_ref[j].astype(F32)
        ki = ki_ref[j].astype(F32)
        zr = yr * kr - yi * ki
        zi = yr * ki + yi * kr
        u = _dot(inv_ref[...], jnp.concatenate([zr, zi], axis=0).astype(BF16))
        ur, ui = u[:n2], u[n2:]
        tr_ref[0, j] = (ur * c - ui * s).astype(BF16)
        ti_ref[0, j] = (ur * s + ui * c).astype(BF16)


def _hy_mid(xr, xi, twc, tws, fwd, inv, kr, ki):
    b, n1, n2, w = xr.shape
    kb = FFT_K1_PER_STEP
    xspec = pl.BlockSpec((1, kb, n2, w), lambda bi, k: (bi, k, 0, 0))
    return pl.pallas_call(
        _hy_mid_kernel,
        grid=(b, n1 // kb),
        in_specs=[xspec, xspec,
                  pl.BlockSpec((kb, n2, LANES), lambda bi, k: (k, 0, 0)),
                  pl.BlockSpec((kb, n2, LANES), lambda bi, k: (k, 0, 0)),
                  pl.BlockSpec((2 * n2, 2 * n2), lambda bi, k: (0, 0)),
                  pl.BlockSpec((2 * n2, 2 * n2), lambda bi, k: (0, 0)),
                  pl.BlockSpec((kb, n2, w), lambda bi, k: (k, 0, 0)),
                  pl.BlockSpec((kb, n2, w), lambda bi, k: (k, 0, 0))],
        out_specs=[xspec, xspec],
        out_shape=[jax.ShapeDtypeStruct((b, n1, n2, w), BF16)] * 2,
        compiler_params=_cparams(("parallel", "parallel")),
        name="hyena_dft_inner",
    )(xr, xi, twc, tws, fwd, inv, kr, ki)


def _hy_post_kernel(tr_ref, ti_ref, i1_ref, p_ref, x0_ref, z_ref, fb_ref, o_ref):
    g = FFT_GROUP
    k1u, width = tr_ref.shape[1], tr_ref.shape[3]
    half = x0_ref.shape[1]
    p = p_ref[...]
    t_re = _regroup(p, tr_ref, k1u, width)
    t_im = _regroup(p, ti_ref, k1u, width)
    ys = [_dot(i1_ref[...], jnp.concatenate([t_re[j], t_im[j]], axis=0)).astype(BF16) for j in range(g)]
    for nb in range(half // g):
        rows = jnp.concatenate([y[g * nb:g * (nb + 1)] for y in ys], axis=0)
        y_nat = _dot(p, rows)
        sl = slice(g * nb, g * (nb + 1))
        x0 = x0_ref[0, sl].reshape(g * g, width).astype(F32)
        z = z_ref[0, sl].reshape(g * g, width).astype(F32)
        o_ref[0, sl] = (x0 * (y_nat + z * fb_ref[...])).astype(BF16).reshape(g, g, width)


def _hy_post(tr, ti, i1, swap, x0_4, z4, fb):
    b, k1u, n2, w = tr.shape
    half = i1.shape[0]
    g = FFT_GROUP
    hspec = pl.BlockSpec((1, half, g, w), lambda bi, j: (bi, 0, j, 0))
    tspec = pl.BlockSpec((1, k1u, g, w), lambda bi, j: (bi, 0, j, 0))
    return pl.pallas_call(
        _hy_post_kernel,
        grid=(b, n2 // g),
        in_specs=[tspec, tspec, pl.BlockSpec((half, 2 * k1u), lambda bi, j: (0, 0)),
                  pl.BlockSpec((g * g, g * g), lambda bi, j: (0, 0)), hspec, hspec,
                  pl.BlockSpec((1, w), lambda bi, j: (0, 0))],
        out_specs=hspec,
        out_shape=jax.ShapeDtypeStruct((b, half, n2, w), BF16),
        compiler_params=_cparams(("parallel", "parallel")),
        name="hyena_dft_outer_inv",
    )(tr, ti, i1, swap, x0_4, z4, fb)


def _hyena_long(proj, conv_w, conv_b, filt, filt_bias):
    b, t, _ = proj.shape
    w = HY_WIDTH
    n = 2 * t
    n2 = FFT_N2
    n1 = n // n2
    k1u = _used_k1(n1)
    f1, i1, fwd, inv, twc, tws = _dft_tables(t)
    swap = _swap_table()
    hfb, l1 = _hy_filter(t, *filt)
    hr, hi = _hy_s1(hfb.reshape(2, n1 // 2, n2, w), f1, swap)
    kr, ki = _hy_filt_spec(hr, hi, twc, tws, fwd, l1, n)
    x0, z = _hy_pre(proj, conv_w, conv_b)
    z4 = z.reshape(b, n1 // 2, n2, w)
    xr, xi = _hy_s1(z4, f1, swap)
    tr, ti = _hy_mid(xr, xi, twc, tws, fwd, inv, kr, ki)
    d = _hy_post(tr, ti, i1, swap, x0.reshape(b, n1 // 2, n2, w), z4, filt_bias.astype(F32).reshape(1, w))
    return d.reshape(b, t, w)


def _hy_ctx_kernel(x0_ref, z_ref, h_ref, l1_ref, cm_ref, sm_ref, cmt_ref, smt_ref, fb_ref, o_ref, *, n):
    cm = cm_ref[...]
    sm = sm_ref[...]

    def spectrum(v):
        return _dot(cm, v), -_dot(sm, v)

    z = z_ref[0]
    zr, zi = spectrum(z)
    ar, ai = spectrum(h_ref[0])
    br, bi = spectrum(h_ref[1])
    scale = 1.0 / (l1_ref[...] * n)
    kr = (ar + br) * scale
    ki = (ai - bi) * scale
    yr = (zr * kr - zi * ki).astype(BF16)
    yi = (zr * ki + zi * kr).astype(BF16)
    y = _dot(cmt_ref[...], yr) - _dot(smt_ref[...], yi)
    o_ref[0] = (x0_ref[0].astype(F32) * (y + z.astype(F32) * fb_ref[...])).astype(BF16)


def _hyena_short(proj, conv_w, conv_b, filt, filt_bias):
    b, t, _ = proj.shape
    w = HY_WIDTH
    n = 2 * t
    hfb, l1 = _hy_filter(t, *filt)
    x0, z = _hy_pre(proj, conv_w, conv_b)
    k = np.arange(n)[:, None]
    j = np.arange(t)[None, :]
    ang = 2.0 * np.pi * ((k * j) % n) / n
    cm, sm = np.cos(ang), np.sin(ang)
    tabs = [jnp.asarray(a, F32).astype(BF16) for a in (cm, sm, cm.T, sm.T)]
    full2 = lambda shape: pl.BlockSpec(shape, lambda bi: (0, 0))
    bspec = pl.BlockSpec((1, t, w), lambda bi: (bi, 0, 0))
    return pl.pallas_call(
        functools.partial(_hy_ctx_kernel, n=n),
        grid=(b,),
        in_specs=[bspec, bspec, pl.BlockSpec((2, t, w), lambda bi: (0, 0, 0)), full2((1, w)),
                  full2((n, t)), full2((n, t)), full2((t, n)), full2((t, n)), full2((1, w))],
        out_specs=bspec,
        out_shape=jax.ShapeDtypeStruct((b, t, w), BF16),
        compiler_params=_cparams(("parallel",)),
        name="hyena_ctx",
    )(x0, z, hfb, l1, *tabs, filt_bias.astype(F32).reshape(1, w))


def _layer(xc, xl, mod, cos128, sin128, norm_w, w_perm, rpb, sink, w_alpha_up, b_alpha, gla_norm_w, conv_w, conv_b,
           filt, filt_bias, w_branch, w_out, final_w, layer, with_ctx_out):
    b, length, d = xl.shape
    mod_l = mod[:b].reshape(b, 1, 3 * d)
    mod_c = jnp.broadcast_to(mod[b].reshape(1, 1, 3 * d), (b, 1, 3 * d))
    sh_l, sc_l, g_l = mod_l[..., :d], mod_l[..., d:2 * d], mod_l[..., 2 * d:]
    sh_c, sc_c, g_c = mod_c[..., :d], mod_c[..., d:2 * d], mod_c[..., 2 * d:]

    wbf = w_branch.astype(BF16) * 0.5
    wb = jnp.stack([wbf[0], _permute(wbf[1], _SWA_OUT_PERM, axis=0), wbf[2], wbf[3]])
    wo = w_out.astype(BF16)

    proj_l = _inproj(xl, norm_w, sc_l, sh_l, w_perm, layer)
    proj_c = _inproj(xc, norm_w, sc_c, sh_c, w_perm, layer)

    y_a = _natten(proj_l, proj_c, _natten_bias_table(rpb))
    sink128 = jnp.broadcast_to(sink.astype(F32)[:, None], (SW_HEADS, LANES))
    qr, kr = _rope(proj_l, cos128, sin128)
    y_b = _swa(qr, kr, proj_l, proj_c, proj_c, sink128 * LOG2E)
    whi, wlo, bpad = _gla_decay_weights(w_alpha_up, b_alpha)
    s_zero = jnp.zeros((b, 2, 2, 128, 128), F32)
    of_c, s_cf = _gla_pass(proj_c, whi, wlo, bpad, s_zero, rev=False)
    y_cc, s_cb = _gla_pass(proj_c, whi, wlo, bpad, s_zero, rev=True, o_fwd=of_c, gnw=gla_norm_w.astype(F32))
    of_l, _ = _gla_pass(proj_l, whi, wlo, bpad, s_cf, rev=False)
    y_c, _ = _gla_pass(proj_l, whi, wlo, bpad, s_cb, rev=True, o_fwd=of_l, gnw=gla_norm_w.astype(F32))
    y_d = _hyena_long(proj_l, conv_w, conv_b, filt, filt_bias)

    xl_new = _merge([y_a, y_b, y_c, y_d], proj_l, wb, wo, g_l, xl, final_w, final=not with_ctx_out)
    if with_ctx_out:
        ya_c = _ctx_attn(proj_c, sink128, swa=False)
        yb_c = _ctx_attn(proj_c, sink128, swa=True)
        yd_c = _hyena_short(proj_c, conv_w, conv_b, filt, filt_bias)
        xc = _merge([ya_c, yb_c, y_cc, yd_c], proj_c, wb, wo, g_c, xc, final_w, final=False)
    return xc, xl_new


def kernel(x, c, ctx, c_ctx, norm_w, w_mod, b_mod, w_in, rpb, sink, w_alpha_up, b_alpha, gla_norm_w, conv_w, conv_b,
           filt_w1, filt_b1, filt_w2, filt_b2, filt_w3, filt_b3, filt_freq, filt_wout, filt_bias, w_branch, w_out,
           final_norm_w):
    b, length, d = x.shape
    depth = norm_w.shape[0]
    cvec = jnp.zeros((8, d), F32).at[:b].set(c.astype(F32)).at[b].set(c_ctx.astype(F32))
    mod = _modulation(cvec, w_mod.astype(F32), b_mod.astype(F32))
    cos128, sin128 = _rope_tables(length)
    w_perm = _win_prep(w_in.astype(F32))
    xc, xl = ctx, x
    for i in range(depth):
        filt = (filt_w1[i], filt_b1[i], filt_w2[i], filt_b2[i], filt_w3[i], filt_b3[i], filt_freq[i], filt_wout[i])
        xc, xl = _layer(xc, xl, mod[i], cos128, sin128, norm_w[i], w_perm, rpb[i], sink[i], w_alpha_up[i],
                        b_alpha[i], gla_norm_w[i], conv_w[i], conv_b[i], filt, filt_bias[i], w_branch[i], w_out[i],
                        final_norm_w, layer=i, with_ctx_out=(i < depth - 1))
    return xl
```

```python
import functools
import math

import numpy as np
import jax
import jax.numpy as jnp
from jax import lax
from jax.experimental import pallas as pl
from jax.experimental.pallas import tpu as pltpu

F32 = jnp.float32
BF16 = jnp.bfloat16

D_MODEL = 1024
GRID_W = 64
HEAD_DIM = 64
BRANCH_WIDTH = D_MODEL // 2
N_BRANCH = 4
NA_HEADS = 8
NA_KR = 8
NA_KC = 16
SW_HEADS = 8
SW_KV_HEADS = 2
SW_WINDOW = 128
SW_BLOCK = 128
GLA_HEADS = 4
GLA_DK = BRANCH_WIDTH // 2
GLA_DV = BRANCH_WIDTH
GLA_RANK = 16
GLA_TAU = 16.0
GLA_CHUNK = 64
HY_WIDTH = BRANCH_WIDTH
HY_EMB = 33
HY_FFN = 64
HY_TARGET = 1e-2
HY_FAST_PCT = 0.3
HY_SLOW_PCT = 1.5
ROPE_BASE = 10000.0
EPS = 1e-6
NEG_INF = -1e30
LOG2E = math.log2(math.e)
LANES = 128
FFT_N2 = 128
VMEM_LIMIT = 56 * 1024 * 1024

_IN_WIDTHS = (512, 512, 512, 512, 512, 128, 128, 512, 256, 256, 512, 16, 16, 512, 1536, 512, 4096)
_IN_OFF = np.concatenate([[0], np.cumsum(_IN_WIDTHS)])
(_O_AQ, _O_AK, _O_AV, _O_AG, _O_BQ, _O_BK, _O_BV, _O_BG, _O_CQ, _O_CK, _O_CV, _O_LRF, _O_LRB, _O_CG,
 _O_DU, _O_DG, _O_GM) = [int(v) for v in _IN_OFF[:-1]]
IN_TOTAL = int(_IN_OFF[-1])

P_AQ, P_AK, P_AV, P_AG = 0, 512, 1024, 1536
P_BQ, P_BG = 2048, 2560
P_CV, P_CG = 3072, 3584
P_DU, P_DG = 4096, 5632
P_GM = 6144
P_CQ, P_CK = 10240, 10496
P_BK, P_BV = 10752, 10880
P_LR = 11008
NP_COLS = 11264


def _swa_q_perm():
    idx = np.zeros(512, np.int64)
    for t in range(4):
        a, b = t, t + 4
        base = 128 * t
        idx[base + 0:base + 32] = 64 * a + np.arange(32)
        idx[base + 32:base + 64] = 64 * b + np.arange(32)
        idx[base + 64:base + 96] = 64 * a + 32 + np.arange(32)
        idx[base + 96:base + 128] = 64 * b + 32 + np.arange(32)
    return idx


def _swa_k_perm():
    idx = np.zeros(128, np.int64)
    idx[0:32] = np.arange(32)
    idx[32:64] = 64 + np.arange(32)
    idx[64:96] = 32 + np.arange(32)
    idx[96:128] = 96 + np.arange(32)
    return idx


def _swa_out_perm():
    idx = np.zeros(512, np.int64)
    for t in range(4):
        idx[128 * t:128 * t + 64] = 64 * t + np.arange(64)
        idx[128 * t + 64:128 * t + 128] = 64 * (t + 4) + np.arange(64)
    return idx


def _build_in_perm():
    perm = np.full(NP_COLS, -1, np.int64)

    def put(p, o, w):
        perm[p:p + w] = o + np.arange(w)

    put(P_AQ, _O_AQ, 512); put(P_AK, _O_AK, 512); put(P_AV, _O_AV, 512); put(P_AG, _O_AG, 512)
    perm[P_BQ:P_BQ + 512] = _O_BQ + _swa_q_perm()
    perm[P_BG:P_BG + 512] = _O_BG + _swa_out_perm()
    put(P_CV, _O_CV, 512); put(P_CG, _O_CG, 512)
    put(P_DU, _O_DU, 1536); put(P_DG, _O_DG, 512)
    put(P_GM, _O_GM, 4096)
    put(P_CQ, _O_CQ, 256); put(P_CK, _O_CK, 256)
    perm[P_BK:P_BK + 128] = _O_BK + _swa_k_perm()
    put(P_BV, _O_BV, 128)
    put(P_LR, _O_LRF, 16); put(P_LR + 16, _O_LRB, 16)
    return perm


def _build_half_cols():
    s = np.ones(NP_COLS, np.float32)
    for p, w in ((P_AG, 512), (P_BG, 512), (P_CG, 512), (P_DG, 512), (P_GM, 4096)):
        s[p:p + w] = 0.5
    return s


_HALF_COLS = _build_half_cols()
_IN_PERM = _build_in_perm()
_SWA_OUT_PERM = _swa_out_perm()


def _permute(w, perm, axis):
    pieces = []
    i = 0
    n = len(perm)
    while i < n:
        j = i + 1
        if perm[i] < 0:
            while j < n and perm[j] < 0:
                j += 1
            shape = list(w.shape)
            shape[axis] = j - i
            pieces.append(jnp.zeros(shape, w.dtype))
        else:
            while j < n and perm[j] == perm[j - 1] + 1:
                j += 1
            pieces.append(lax.slice_in_dim(w, int(perm[i]), int(perm[j - 1]) + 1, axis=axis))
        i = j
    return jnp.concatenate(pieces, axis=axis)


PERM_GRAIN = 32


def _win_prep_kernel(w_ref, o_ref):
    rows = w_ref.shape[1]
    for j in range(NP_COLS // LANES):
        pieces = []
        for q in range(LANES // PERM_GRAIN):
            src = int(_IN_PERM[LANES * j + PERM_GRAIN * q])
            if src < 0:
                pieces.append(jnp.zeros((rows, PERM_GRAIN), F32))
            else:
                pieces.append(w_ref[0, :, src:src + PERM_GRAIN])
        tile = jnp.concatenate(pieces, axis=1) * float(_HALF_COLS[LANES * j])
        o_ref[0, :, LANES * j:LANES * (j + 1)] = tile.astype(BF16)


def _win_prep(w_in):
    depth, d, n = w_in.shape
    grain = np.arange(0, NP_COLS, PERM_GRAIN)
    blocks = _IN_PERM.reshape(-1, PERM_GRAIN)
    assert np.all((blocks[:, :1] < 0) | (np.diff(blocks, axis=1) == 1).all(axis=1, keepdims=True)), grain
    assert np.all(_HALF_COLS.reshape(-1, LANES) == _HALF_COLS.reshape(-1, LANES)[:, :1])
    tr = 256
    return pl.pallas_call(
        _win_prep_kernel,
        grid=(depth, d // tr),
        in_specs=[pl.BlockSpec((1, tr, n), lambda l, i: (l, i, 0))],
        out_specs=pl.BlockSpec((1, tr, NP_COLS), lambda l, i: (l, i, 0)),
        out_shape=jax.ShapeDtypeStruct((depth, d, NP_COLS), BF16),
        compiler_params=_cparams(("parallel", "parallel")),
        name="win_prep",
    )(w_in)


def _cparams(sem):
    return pltpu.CompilerParams(dimension_semantics=sem, vmem_limit_bytes=VMEM_LIMIT)


def _sigmoid(x):
    return 1.0 / (1.0 + jnp.exp(-x))


def _sigmoid_tanh(x):
    return 0.5 * jnp.tanh(0.5 * x) + 0.5


def _split3(a):
    hi = a.astype(BF16)
    r1 = a - hi.astype(F32)
    mid = r1.astype(BF16)
    lo = (r1 - mid.astype(F32)).astype(BF16)
    return hi, mid, lo


def _dot(a, b):
    return jnp.dot(a, b, preferred_element_type=F32)


def _dot_nt(a, b):
    return lax.dot_general(a, b, (((1,), (1,)), ((), ())), preferred_element_type=F32)


def _dot_tn(a, b):
    return lax.dot_general(a, b, (((0,), (0,)), ((), ())), preferred_element_type=F32)


def _dot_x3(a, b):
    ah, am, _ = _split3(a)
    bh, bm, _ = _split3(b)
    return _dot(ah, bh) + (_dot(ah, bm) + _dot(am, bh))


def _mod_kernel(c_ref, w_ref, b_ref, o_ref):
    c = c_ref[...]
    s = c * _sigmoid(c)
    o_ref[0] = _dot_x3(s, w_ref[0]) + b_ref[0]


def _modulation(cvec, w_mod, b_mod):
    depth, d, n = w_mod.shape
    tn = 512
    return pl.pallas_call(
        _mod_kernel,
        grid=(depth, n // tn),
        in_specs=[pl.BlockSpec((8, d), lambda l, j: (0, 0)),
                  pl.BlockSpec((1, d, tn), lambda l, j: (l, 0, j)),
                  pl.BlockSpec((1, 1, tn), lambda l, j: (l, 0, j))],
        out_specs=pl.BlockSpec((1, 8, tn), lambda l, j: (l, 0, j)),
        out_shape=jax.ShapeDtypeStruct((depth, 8, n), F32),
        compiler_params=_cparams(("parallel", "parallel")),
        name="modulation",
    )(cvec, w_mod, b_mod.reshape(depth, 1, n))


def _inproj_kernel(x_ref, nw_ref, sc_ref, sh_ref, w_ref, o_ref, h_ref):
    @pl.when(pl.program_id(2) == 0)
    def _():
        x = x_ref[0]
        ms = jnp.mean(x * x, axis=-1, keepdims=True)
        y = x * lax.rsqrt(ms + EPS) * nw_ref[...]
        h_ref[...] = (y * (1.0 + sc_ref[0]) + sh_ref[0]).astype(BF16)

    o_ref[0] = _dot(h_ref[...], w_ref[...]).astype(BF16)


def _inproj(x, norm_w, scale, shift, w_perm, layer):
    b, t, d = x.shape
    n = w_perm.shape[2]
    tm = min(t, 2048)
    tn = 1024
    return pl.pallas_call(
        _inproj_kernel,
        grid=(b, t // tm, n // tn),
        in_specs=[pl.BlockSpec((1, tm, d), lambda bi, i, j: (bi, i, 0)),
                  pl.BlockSpec((1, d), lambda bi, i, j: (0, 0)),
                  pl.BlockSpec((1, 1, d), lambda bi, i, j: (bi, 0, 0)),
                  pl.BlockSpec((1, 1, d), lambda bi, i, j: (bi, 0, 0)),
                  pl.BlockSpec((None, d, tn), lambda bi, i, j: (layer, 0, j))],
        out_specs=pl.BlockSpec((1, tm, tn), lambda bi, i, j: (bi, i, j)),
        out_shape=jax.ShapeDtypeStruct((b, t, n), BF16),
        scratch_shapes=[pltpu.VMEM((tm, d), BF16)],
        compiler_params=_cparams(("parallel", "parallel", "arbitrary")),
        name="inproj",
    )(x, norm_w.reshape(1, d), scale, shift, w_perm)


def _merge_kernel(ya, yb, yc, yd, ga, gb, gc, gd, m0, m1, m2, m3, wb_ref, wo_ref, g_ref, x_ref, fw_ref,
                  o_ref, *, final):
    acc = None
    for i, (y, g, gm) in enumerate(((ya, ga, m0), (yb, gb, m1), (yc, gc, m2), (yd, gd, m3))):
        hg = g[0].astype(F32)
        yg = y[0].astype(F32) * hg
        ys = (yg + yg * jnp.tanh(hg)).astype(BF16)
        hp = _dot(ys, wb_ref[i])
        t = hp + hp * jnp.tanh(gm[0].astype(F32))
        acc = t if acc is None else acc + t
    out = x_ref[0] + g_ref[0] * _dot(acc.astype(BF16), wo_ref[...])
    if final:
        ms = jnp.mean(out * out, axis=-1, keepdims=True)
        out = out * lax.rsqrt(ms + EPS) * fw_ref[...]
    o_ref[0] = out


def _merge(ys, proj, w_branch, w_out, gate, x, final_w, final):
    b, t, d = x.shape
    bw = BRANCH_WIDTH
    tm = min(t, 512)
    yspec = pl.BlockSpec((1, tm, bw), lambda bi, i: (bi, i, 0))

    def pspec(col, width):
        blk = col // width
        return pl.BlockSpec((1, tm, width), lambda bi, i: (bi, i, blk))

    in_specs = ([yspec] * 4
                + [pspec(P_AG, bw), pspec(P_BG, bw), pspec(P_CG, bw), pspec(P_DG, bw)]
                + [pspec(P_GM + k * d, d) for k in range(N_BRANCH)]
                + [pl.BlockSpec((N_BRANCH, bw, d), lambda bi, i: (0, 0, 0)),
                   pl.BlockSpec((d, d), lambda bi, i: (0, 0)),
                   pl.BlockSpec((1, 1, d), lambda bi, i: (bi, 0, 0)),
                   pl.BlockSpec((1, tm, d), lambda bi, i: (bi, i, 0)),
                   pl.BlockSpec((1, d), lambda bi, i: (0, 0))])
    return pl.pallas_call(
        functools.partial(_merge_kernel, final=final),
        grid=(b, t // tm),
        in_specs=in_specs,
        out_specs=pl.BlockSpec((1, tm, d), lambda bi, i: (bi, i, 0)),
        out_shape=jax.ShapeDtypeStruct((b, t, d), F32),
        compiler_params=_cparams(("parallel", "parallel")),
        name="merge",
    )(*ys, *([proj] * 8), w_branch, w_out, gate, x, final_w.reshape(1, d))


def _rope_kernel(q_ref, k_ref, cos_ref, sin_ref, qo_ref, ko_ref):
    cos = cos_ref[...]
    sin = sin_ref[...]

    def rot(x):
        return x * cos + pltpu.roll(x, 64, 1) * sin

    for t in range(4):
        q = q_ref[0, :, 128 * t:128 * (t + 1)].astype(F32)
        qo_ref[0, :, 128 * t:128 * (t + 1)] = (rot(q) * (HEAD_DIM ** -0.5 * LOG2E)).astype(BF16)
    ko_ref[0] = rot(k_ref[0].astype(F32)).astype(BF16)


def _rope(proj, cos128, sin128):
    b, t, _ = proj.shape
    tm = min(t, 1024)
    return pl.pallas_call(
        _rope_kernel,
        grid=(b, t // tm),
        in_specs=[pl.BlockSpec((1, tm, 512), lambda bi, i: (bi, i, P_BQ // 512)),
                  pl.BlockSpec((1, tm, 128), lambda bi, i: (bi, i, P_BK // 128)),
                  pl.BlockSpec((tm, 128), lambda bi, i: (i, 0)),
                  pl.BlockSpec((tm, 128), lambda bi, i: (i, 0))],
        out_specs=[pl.BlockSpec((1, tm, 512), lambda bi, i: (bi, i, 0)),
                   pl.BlockSpec((1, tm, 128), lambda bi, i: (bi, i, 0))],
        out_shape=[jax.ShapeDtypeStruct((b, t, 512), BF16), jax.ShapeDtypeStruct((b, t, 128), BF16)],
        compiler_params=_cparams(("parallel", "parallel")),
        name="rope",
    )(proj, proj, cos128, sin128)


def _rope_tables(length):
    t = jnp.arange(length, dtype=jnp.int32)
    row = (t // GRID_W).astype(F32)
    col = (t % GRID_W).astype(F32)
    n_freq = HEAD_DIM // 4
    inv = ROPE_BASE ** (-jnp.arange(n_freq, dtype=F32) / n_freq)
    ang = jnp.concatenate([row[:, None] * inv, col[:, None] * inv], axis=-1)
    cos, sin = jnp.cos(ang), jnp.sin(ang)
    return jnp.tile(cos, (1, 4)), jnp.concatenate([-sin, -sin, sin, sin], axis=-1)


def _swa_kernel(q_ref, k_ref, v_ref, kc_ref, vc_ref, sink_ref, mask_ref, o_ref, *, length):
    blk = SW_BLOCK
    kw = 3 * blk
    nb = length // blk
    kc = kc_ref[0]
    vc = vc_ref[0]
    lane = lax.broadcasted_iota(jnp.int32, (blk, LANES), 1)
    a_lanes = (lane // 32) % 2 == 0
    lo_lanes = lane < 64
    top = lax.broadcasted_iota(jnp.int32, (2 * blk, 1), 0) < blk
    nsb = q_ref.shape[1] // blk
    units = [(sb, t) for sb in range(nsb) for t in range(4)]

    def stacked_q(sb, t):
        qt = q_ref[0, sb * blk:(sb + 1) * blk, 128 * t:128 * (t + 1)]
        zero = jnp.zeros_like(qt)
        return jnp.concatenate([jnp.where(a_lanes, qt, zero), jnp.where(a_lanes, zero, qt)], axis=0)

    sc_all = _dot_nt(jnp.concatenate([stacked_q(sb, t) for sb, t in units], axis=0), kc)
    o_win, pcs, rdens = [], [], []

    for u, (sb, t) in enumerate(units):
        i = pl.program_id(1) * nsb + sb
        start = pl.multiple_of(jnp.clip((i - 1) * blk, 0, length - kw), blk)
        variant = jnp.where(i == 0, 0, jnp.where(i == nb - 1, 2, 1))
        sw = _dot_nt(stacked_q(sb, t), k_ref[0, pl.ds(start, kw), :]) + mask_ref[variant]
        sc = sc_all[2 * blk * u:2 * blk * (u + 1)]
        sk = jnp.where(top, sink_ref[t:t + 1, 0:1], sink_ref[t + 4:t + 5, 0:1])
        m = jnp.maximum(jnp.maximum(jnp.max(sw, axis=-1, keepdims=True), jnp.max(sc, axis=-1, keepdims=True)), sk)
        pw = jnp.exp2(sw - m)
        pc = jnp.exp2(sc - m)
        den = jnp.sum(pw, axis=-1, keepdims=True) + jnp.sum(pc, axis=-1, keepdims=True) + jnp.exp2(sk - m)
        o_win.append(_dot(pw.astype(BF16), v_ref[0, pl.ds(start, kw), :]))
        pcs.append(pc.astype(BF16))
        rdens.append(1.0 / den)
    o_ctx = _dot(jnp.concatenate(pcs, axis=0), vc)
    for u, (sb, t) in enumerate(units):
        o = (o_win[u] + o_ctx[2 * blk * u:2 * blk * (u + 1)]) * rdens[u]
        o_ref[0, sb * blk:(sb + 1) * blk, 128 * t:128 * (t + 1)] = jnp.where(lo_lanes, o[:blk], o[blk:]).astype(BF16)


def _swa(qr, kr, proj_l, kc_rot, proj_c, sink128):
    b, length, _ = qr.shape
    lc = proj_c.shape[1]
    assert length >= 3 * SW_BLOCK
    tq = 8 * SW_BLOCK if length % (8 * SW_BLOCK) == 0 else SW_BLOCK
    r = np.arange(2 * SW_BLOCK)[:, None] % SW_BLOCK
    c = np.arange(3 * SW_BLOCK)[None, :]
    mask_tab = jnp.asarray(np.stack([np.where(np.abs(v * SW_BLOCK + r - c) <= SW_WINDOW, 0.0, NEG_INF)
                                     for v in range(3)]), F32)
    return pl.pallas_call(
        functools.partial(_swa_kernel, length=length),
        grid=(b, length // tq),
        in_specs=[pl.BlockSpec((1, tq, 512), lambda bi, i: (bi, i, 0)),
                  pl.BlockSpec((1, length, 128), lambda bi, i: (bi, 0, 0)),
                  pl.BlockSpec((1, length, 128), lambda bi, i: (bi, 0, P_BV // 128)),
                  pl.BlockSpec((1, lc, 128), lambda bi, i: (bi, 0, P_BK // 128)),
                  pl.BlockSpec((1, lc, 128), lambda bi, i: (bi, 0, P_BV // 128)),
                  pl.BlockSpec((8, 128), lambda bi, i: (0, 0)),
                  pl.BlockSpec((3, 2 * SW_BLOCK, 3 * SW_BLOCK), lambda bi, i: (0, 0, 0))],
        out_specs=pl.BlockSpec((1, tq, 512), lambda bi, i: (bi, i, 0)),
        out_shape=jax.ShapeDtypeStruct((b, length, 512), BF16),
        compiler_params=_cparams(("parallel", "arbitrary")),
        name="swa",
    )(qr, kr, proj_l, kc_rot, proj_c, sink128, mask_tab)


def _ctx_attn_kernel(q_ref, k_ref, v_ref, sink_ref, o_ref, *, swa):
    t_len = q_ref.shape[1]
    lane = lax.broadcasted_iota(jnp.int32, (t_len, LANES), 1)
    lo_lanes = lane < 64
    a_lanes = ((lane // 32) % 2 == 0) if swa else lo_lanes
    top = lax.broadcasted_iota(jnp.int32, (2 * t_len, 1), 0) < t_len
    for t in range(4):
        qt = (q_ref[0, :, 128 * t:128 * (t + 1)].astype(F32) * HEAD_DIM ** -0.5).astype(BF16)
        zero = jnp.zeros_like(qt)
        qq = jnp.concatenate([jnp.where(a_lanes, qt, zero), jnp.where(a_lanes, zero, qt)], axis=0)
        if swa:
            kt, vt = k_ref[0], v_ref[0]
        else:
            kt, vt = k_ref[0, :, 128 * t:128 * (t + 1)], v_ref[0, :, 128 * t:128 * (t + 1)]
        s = _dot_nt(qq, kt)
        m = jnp.max(s, axis=-1, keepdims=True)
        if swa:
            sk = jnp.where(top, sink_ref[t:t + 1, 0:1], sink_ref[t + 4:t + 5, 0:1])
            m = jnp.maximum(m, sk)
        p = jnp.exp(s - m)
        den = jnp.sum(p, axis=-1, keepdims=True)
        if swa:
            den = den + jnp.exp(sk - m)
        o = _dot(p.astype(BF16), vt) / den
        o_ref[0, :, 128 * t:128 * (t + 1)] = jnp.where(lo_lanes, o[:t_len], o[t_len:]).astype(BF16)


def _ctx_attn(proj_c, sink128, swa):
    b, lc, _ = proj_c.shape
    if swa:
        qs = pl.BlockSpec((1, lc, 512), lambda bi: (bi, 0, P_BQ // 512))
        ks = pl.BlockSpec((1, lc, 128), lambda bi: (bi, 0, P_BK // 128))
        vs = pl.BlockSpec((1, lc, 128), lambda bi: (bi, 0, P_BV // 128))
    else:
        qs = pl.BlockSpec((1, lc, 512), lambda bi: (bi, 0, P_AQ // 512))
        ks = pl.BlockSpec((1, lc, 512), lambda bi: (bi, 0, P_AK // 512))
        vs = pl.BlockSpec((1, lc, 512), lambda bi: (bi, 0, P_AV // 512))
    return pl.pallas_call(
        functools.partial(_ctx_attn_kernel, swa=swa),
        grid=(b,),
        in_specs=[qs, ks, vs, pl.BlockSpec((8, 128), lambda bi: (0, 0))],
        out_specs=pl.BlockSpec((1, lc, 512), lambda bi: (bi, 0, 0)),
        out_shape=jax.ShapeDtypeStruct((b, lc, 512), BF16),
        compiler_params=_cparams(("parallel",)),
        name="ctx_attn_swa" if swa else "ctx_attn_na",
    )(proj_c, proj_c, proj_c, sink128)


def _natten_kernel(q_ref, k_ref, v_ref, kc_ref, vc_ref, bias_ref, o_ref, *, rows, rows_per_step):
    blk = pl.program_id(2)
    kc = kc_ref[0]
    vc = vc_ref[0]
    w = GRID_W
    nkeys = NA_KR * w
    lo_all = lax.broadcasted_iota(jnp.int32, (rows_per_step * w, LANES), 1) < 64
    lo_row = lax.broadcasted_iota(jnp.int32, (w, LANES), 1) < 64
    q_all = (q_ref[0].astype(F32) * (HEAD_DIM ** -0.5 * LOG2E)).astype(BF16)
    zero = jnp.zeros_like(q_all)
    q_lo = jnp.where(lo_all, q_all, zero)
    q_hi = jnp.where(lo_all, zero, q_all)
    sc_lo = _dot_nt(q_lo, kc)
    sc_hi = _dot_nt(q_hi, kc)
    o_rows, pc_lo, pc_hi, rd_lo, rd_hi = [], [], [], [], []

    def scores(rr):
        r = blk * rows_per_step + rr
        rs = jnp.clip(r - NA_KR // 2, 0, rows - NA_KR)
        off = rs - r + (NA_KR - 1)
        k0 = pl.multiple_of(rs * w, w)
        sl = slice(rr * w, (rr + 1) * w)
        qq = jnp.concatenate([q_lo[sl], q_hi[sl]], axis=0)
        s = _dot_nt(qq, k_ref[0, pl.ds(k0, nkeys), :]) + bias_ref[off, 0]
        sc = jnp.concatenate([sc_lo[sl], sc_hi[sl]], axis=0)
        return s, sc, k0

    ahead = 2
    queue = [scores(rr) for rr in range(ahead)]
    for rr in range(rows_per_step):
        s, sc, k0 = queue.pop(0)
        if rr + ahead < rows_per_step:
            queue.append(scores(rr + ahead))
        vrows = v_ref[0, pl.ds(k0, nkeys), :]
        m = jnp.maximum(jnp.max(s, axis=-1, keepdims=True), jnp.max(sc, axis=-1, keepdims=True))
        p = jnp.exp2(s - m)
        pc = jnp.exp2(sc - m)
        rden = 1.0 / (jnp.sum(p, axis=-1, keepdims=True) + jnp.sum(pc, axis=-1, keepdims=True))
        o = _dot(p.astype(BF16), vrows)
        pcb = pc.astype(BF16)
        o_rows.append(jnp.where(lo_row, o[:w], o[w:]))
        rd_lo.append(rden[:w])
        rd_hi.append(rden[w:])
        pc_lo.append(pcb[:w])
        pc_hi.append(pcb[w:])
    oc = jnp.where(lo_all, _dot(jnp.concatenate(pc_lo, axis=0), vc), _dot(jnp.concatenate(pc_hi, axis=0), vc))
    rd = jnp.where(lo_all, jnp.concatenate(rd_lo, axis=0), jnp.concatenate(rd_hi, axis=0))
    o_ref[0] = ((jnp.concatenate(o_rows, axis=0) + oc) * rd).astype(BF16)


def _natten(proj_l, proj_c, bias_tab):
    b, length, _ = proj_l.shape
    lc = proj_c.shape[1]
    rows = length // GRID_W
    assert rows >= NA_KR
    rps = 32 if rows % 32 == 0 else 8
    tq = rps * GRID_W
    nk = NA_KR * GRID_W
    return pl.pallas_call(
        functools.partial(_natten_kernel, rows=rows, rows_per_step=rps),
        grid=(b, 4, rows // rps),
        in_specs=[pl.BlockSpec((1, tq, 128), lambda bi, hp, i: (bi, i, P_AQ // 128 + hp)),
                  pl.BlockSpec((1, length, 128), lambda bi, hp, i: (bi, 0, P_AK // 128 + hp)),
                  pl.BlockSpec((1, length, 128), lambda bi, hp, i: (bi, 0, P_AV // 128 + hp)),
                  pl.BlockSpec((1, lc, 128), lambda bi, hp, i: (bi, 0, P_AK // 128 + hp)),
                  pl.BlockSpec((1, lc, 128), lambda bi, hp, i: (bi, 0, P_AV // 128 + hp)),
                  pl.BlockSpec((NA_KR, 1, 2 * GRID_W, nk), lambda bi, hp, i: (0, hp, 0, 0))],
        out_specs=pl.BlockSpec((1, tq, 128), lambda bi, hp, i: (bi, i, hp)),
        out_shape=jax.ShapeDtypeStruct((b, length, 512), BF16),
        compiler_params=_cparams(("parallel", "parallel", "arbitrary")),
        name="natten",
    )(proj_l, proj_l, proj_l, proj_c, proj_c, bias_tab)


def _natten_bias_table(rpb):
    w = GRID_W
    nd = 2 * NA_KC - 1
    qc = np.arange(w)[:, None]
    kcol = np.arange(w)[None, :]
    cs = np.clip(qc - NA_KC // 2, 0, w - NA_KC)
    cmask = (kcol >= cs) & (kcol < cs + NA_KC)
    period = nd + w
    padded = jnp.pad(rpb.astype(F32), ((0, 0), (0, 0), (0, w)))
    flat = jnp.tile(padded, (1, 1, w))[:, :, :w * (period - 1)]
    t = flat.reshape(NA_HEADS, 2 * NA_KR - 1, w, period - 1)[..., NA_KC - 1:NA_KC - 1 + w]
    g = jnp.stack([t[:, off:off + NA_KR] for off in range(NA_KR)])
    g = jnp.where(cmask[None, None, None], g * LOG2E, NEG_INF)
    return g.transpose(0, 1, 3, 2, 4).reshape(NA_KR, NA_HEADS // 2, 2 * w, NA_KR * w)


def _gla_kernel(*refs, rev, ncb, final):
    if final:
        (q_ref, k_ref, v_ref, lr_ref, whi_ref, wlo_ref, b_ref, s0_ref, of_ref, gnw_ref,
         o_ref, sfin_ref, st_ref) = refs
    else:
        (q_ref, k_ref, v_ref, lr_ref, whi_ref, wlo_ref, b_ref, s0_ref,
         o_ref, sfin_ref, st_ref) = refs
    ch = GLA_CHUNK
    streams = [(bb, p) for bb in range(q_ref.shape[0]) for p in range(2)]

    @pl.when(pl.program_id(0) == 0)
    def _():
        st_ref[...] = s0_ref[...]

    tb = ncb * ch
    rix = lax.broadcasted_iota(jnp.int32, (ch, ch), 0)
    cix = lax.broadcasted_iota(jnp.int32, (ch, ch), 1)
    tri = (rix <= cix) if rev else (rix >= cix)
    lo_lanes = lax.broadcasted_iota(jnp.int32, (tb, LANES), 1) < 64
    end = 0 if rev else ch - 1
    pos = lax.broadcasted_iota(jnp.int32, (tb, 1), 0) % ch
    order = [(ncb - 1 - cc) if rev else cc for cc in range(ncb)]

    def prefix(s):
        bb, p = s
        lr = lr_ref[bb]
        arg = _dot(lr, whi_ref[p]) + _dot(lr, wlo_ref[p]) + b_ref[p]
        cum = (jnp.minimum(arg, 0.0) - jnp.log(1.0 + jnp.exp(-jnp.abs(arg)))) * (1.0 / GLA_TAU)
        step = 1
        while step < ch:
            if rev:
                cum = cum + jnp.where(pos < ch - step, pltpu.roll(cum, tb - step, 0), 0.0)
            else:
                cum = cum + jnp.where(pos >= step, pltpu.roll(cum, step, 0), 0.0)
            step *= 2
        q = q_ref[bb, :, LANES * p:LANES * (p + 1)].astype(F32)
        k = k_ref[bb, :, LANES * p:LANES * (p + 1)].astype(F32)
        qd = (q * (HEAD_DIM ** -0.5) * jnp.exp(cum)).astype(BF16)
        kd = (k * jnp.exp(-cum)).astype(BF16)
        zero = jnp.zeros_like(qd)
        return cum, k, kd, (jnp.where(lo_lanes, qd, zero), jnp.where(lo_lanes, zero, qd))

    def intra_chunks(s, pre):
        cum, k, kd, qd_h = pre
        bb, p = s
        res = {}
        for c in order:
            sl = slice(c * ch, (c + 1) * ch)
            cum_c = cum[sl]
            cum_end = cum_c[end:end + 1, :]
            kdec = (k[sl] * jnp.exp(cum_end - cum_c)).astype(BF16)
            for h in range(2):
                vh = v_ref[bb, sl, 128 * (2 * p + h):128 * (2 * p + h + 1)]
                a = jnp.where(tri, _dot_nt(qd_h[h][sl], kd[sl]), 0.0)
                res[c, h] = (_dot(a.astype(BF16), vh), _dot_tn(vh, kdec), jnp.exp(cum_end))
        return res

    def recurrence(s, pre, res):
        qd_h = pre[3]
        bb, p = s
        st = [st_ref[bb, p, 0], st_ref[bb, p, 1]]
        for c in order:
            sl = slice(c * ch, (c + 1) * ch)
            for h in range(2):
                o_intra, kv_t, gdec = res[c, h]
                o = o_intra + _dot_nt(qd_h[h][sl], st[h].astype(BF16))
                st[h] = st[h] * gdec + kv_t
                cols = slice(128 * (2 * p + h), 128 * (2 * p + h + 1))
                if final:
                    o = o + of_ref[bb, sl, cols]
                    ms = jnp.mean(o * o, axis=-1, keepdims=True)
                    o = o * lax.rsqrt(ms + EPS) * gnw_ref[...]
                    o_ref[bb, sl, cols] = o.astype(BF16)
                else:
                    o_ref[bb, sl, cols] = o
        st_ref[bb, p, 0] = st[0]
        st_ref[bb, p, 1] = st[1]

    pres = [prefix(s) for s in streams]
    ress = [intra_chunks(s, pre) for s, pre in zip(streams, pres)]
    for s, pre, res in zip(streams, pres, ress):
        recurrence(s, pre, res)
    sfin_ref[...] = st_ref[...]


def _gla_pass(proj, w_pad_hi, w_pad_lo, b_pad, s0, rev, o_fwd=None, gnw=None):
    b, t, _ = proj.shape
    final = o_fwd is not None
    tb = min(t, 512)
    nblk = t // tb
    ncb = tb // GLA_CHUNK
    d = 1 if rev else 0

    def bi_map(i):
        return (nblk - 1 - i) if rev else i

    state_spec = pl.BlockSpec((b, 2, 2, 128, 128), lambda i: (0, 0, 0, 0, 0))
    in_specs = [pl.BlockSpec((b, tb, 256), lambda i: (0, bi_map(i), P_CQ // 256)),
                pl.BlockSpec((b, tb, 256), lambda i: (0, bi_map(i), P_CK // 256)),
                pl.BlockSpec((b, tb, 512), lambda i: (0, bi_map(i), P_CV // 512)),
                pl.BlockSpec((b, tb, 128), lambda i: (0, bi_map(i), P_LR // 128)),
                pl.BlockSpec((2, 128, 128), lambda i: (d, 0, 0)),
                pl.BlockSpec((2, 128, 128), lambda i: (d, 0, 0)),
                pl.BlockSpec((2, 1, 128), lambda i: (d, 0, 0)),
                state_spec]
    args = [proj, proj, proj, proj, w_pad_hi, w_pad_lo, b_pad, s0]
    if final:
        in_specs += [pl.BlockSpec((b, tb, 512), lambda i: (0, bi_map(i), 0)),
                     pl.BlockSpec((1, 128), lambda i: (0, 0))]
        args += [o_fwd, gnw.reshape(1, 128)]
    out, s_fin = pl.pallas_call(
        functools.partial(_gla_kernel, rev=rev, ncb=ncb, final=final),
        grid=(nblk,),
        in_specs=in_specs,
        out_specs=[pl.BlockSpec((b, tb, 512), lambda i: (0, bi_map(i), 0)), state_spec],
        out_shape=[jax.ShapeDtypeStruct((b, t, 512), BF16 if final else F32),
                   jax.ShapeDtypeStruct((b, 2, 2, 128, 128), F32)],
        scratch_shapes=[pltpu.VMEM((b, 2, 2, 128, 128), F32)],
        compiler_params=_cparams(("arbitrary",)),
        name="gla_rev" if rev else "gla_fwd",
    )(*args)
    return out, s_fin


def _gla_decay_weights(w_alpha_up, b_alpha):
    w = jnp.zeros((2, 2, 128, 128), F32)
    for d in range(2):
        for p in range(2):
            w = w.at[d, p, 16 * d:16 * d + 16, :].set(w_alpha_up[d][:, 128 * p:128 * (p + 1)].astype(F32))
    w = w.reshape(4, 128, 128)
    hi = w.astype(BF16)
    lo = (w - hi.astype(F32)).astype(BF16)
    return hi, lo, b_alpha.astype(F32).reshape(4, 1, 128)


def _hy_pre_kernel(u0, u1, u2, p0, p1, p2, n0, n1, n2, w_ref, b_ref, x0_ref, z_ref, *, nblk):
    i = pl.program_id(1)
    tm = u0.shape[1]
    row = lax.broadcasted_iota(jnp.int32, (tm, 1), 0)
    has_prev = (i > 0).astype(F32)
    has_next = (i < nblk - 1).astype(F32)

    def conv(u_ref, p_ref, n_ref, j):
        u = u_ref[0].astype(F32)
        prev = p_ref[0, 7:8, :].astype(F32) * has_prev
        nxt = n_ref[0, 0:1, :].astype(F32) * has_next
        up = jnp.where(row == 0, prev, pltpu.roll(u, 1, 0))
        dn = jnp.where(row == tm - 1, nxt, pltpu.roll(u, tm - 1, 0))
        w = w_ref[:, 512 * j:512 * (j + 1)]
        return up * w[0:1] + u * w[1:2] + dn * w[2:3] + b_ref[:, 512 * j:512 * (j + 1)]

    x0_ref[0] = conv(u0, p0, n0, 0).astype(BF16)
    z_ref[0] = (conv(u1, p1, n1, 1) * conv(u2, p2, n2, 2)).astype(BF16)


def _hy_pre(proj, conv_w, conv_b):
    b, t, _ = proj.shape
    tm = min(t, 512)
    nblk = t // tm
    hb = tm // 8
    nrb = t // 8
    c0 = P_DU // 512
    main = [pl.BlockSpec((1, tm, 512), functools.partial(lambda bi, i, j: (bi, i, c0 + j), j=j)) for j in range(3)]
    prev = [pl.BlockSpec((1, 8, 512), functools.partial(lambda bi, i, j: (bi, jnp.maximum(i * hb - 1, 0), c0 + j), j=j))
            for j in range(3)]
    nxt = [pl.BlockSpec((1, 8, 512),
                        functools.partial(lambda bi, i, j: (bi, jnp.minimum((i + 1) * hb, nrb - 1), c0 + j), j=j))
           for j in range(3)]
    return pl.pallas_call(
        functools.partial(_hy_pre_kernel, nblk=nblk),
        grid=(b, nblk),
        in_specs=main + prev + nxt + [pl.BlockSpec((3, 1536), lambda bi, i: (0, 0)),
                                      pl.BlockSpec((1, 1536), lambda bi, i: (0, 0))],
        out_specs=[pl.BlockSpec((1, tm, 512), lambda bi, i: (bi, i, 0))] * 2,
        out_shape=[jax.ShapeDtypeStruct((b, t, 512), BF16)] * 2,
        compiler_params=_cparams(("parallel", "parallel")),
        name="hyena_pre",
    )(*([proj] * 9), conv_w.astype(F32), conv_b.astype(F32).reshape(1, 1536))


def _hy_filter_kernel(z_ref, w1, b1, w2, b2, w3, b3, fr, wo, dl, h_ref, l1_ref, *, tm, t_len):
    i = pl.program_id(0)
    f = fr[...]
    half = tm // 2
    z = z_ref[...]
    hh = jnp.concatenate([z[:half], z[half:]], axis=1)
    hh = jnp.sin(f * (_dot_x3(hh, w1[...]) + b1[...]))
    hh = jnp.sin(f * (_dot_x3(hh, w2[...]) + b2[...]))
    hh = jnp.sin(f * (_dot_x3(hh, w3[...]) + b3[...]))
    hh = _dot_x3(hh, wo[...])
    hh = jnp.concatenate([hh[:, :2 * HY_WIDTH], hh[:, 2 * HY_WIDTH:]], axis=0)
    row = i * tm + lax.broadcasted_iota(jnp.int32, (tm, 1), 0)
    t = row.astype(F32) / (t_len - 1)
    decay = jnp.exp(-t * dl[...])
    h_f = hh[:, :HY_WIDTH] * decay
    h_b = jnp.where(row == 0, 0.0, hh[:, HY_WIDTH:] * decay)
    h_ref[0] = h_f.astype(BF16)
    h_ref[1] = h_b.astype(BF16)
    part = jnp.sum(jnp.abs(h_f) + jnp.abs(h_b), axis=0, keepdims=True)

    @pl.when(i == 0)
    def _():
        l1_ref[...] = part

    @pl.when(i > 0)
    def _():
        l1_ref[...] = l1_ref[...] + part


def _hy_filter(t_len, w1, b1, w2, b2, w3, b3, freq, wout):
    tm = min(t_len, 512)
    t = jnp.linspace(0.0, 1.0, t_len, dtype=F32)[:, None]
    bands = (HY_EMB - 1) // 2
    w_ang = 2.0 * math.pi * jnp.arange(t_len, dtype=F32)[:, None] / t_len
    f = jnp.linspace(1e-4, bands - 1, bands, dtype=F32)[None, :]
    z = jnp.concatenate([t, jnp.cos(f * w_ang), -jnp.sin(f * w_ang),
                         jnp.zeros((t_len, LANES - HY_EMB), F32)], axis=-1)
    w1p = jnp.concatenate([w1.astype(F32), jnp.zeros((LANES - HY_EMB, HY_FFN), F32)], axis=0)
    deltas = jnp.abs(jnp.linspace(math.log(HY_TARGET) / HY_SLOW_PCT, math.log(HY_TARGET) / HY_FAST_PCT,
                                  HY_WIDTH, dtype=F32))[None, :]
    full = lambda shape: pl.BlockSpec(shape, lambda i: tuple(0 for _ in shape))
    r = lambda v: jnp.tile(v.astype(F32).reshape(1, -1), (1, 2))

    def bdiag(w):
        w = w.astype(F32)
        zero = jnp.zeros_like(w)
        return jnp.concatenate([jnp.concatenate([w, zero], axis=1), jnp.concatenate([zero, w], axis=1)], axis=0)

    hid = 2 * HY_FFN
    return pl.pallas_call(
        functools.partial(_hy_filter_kernel, tm=tm, t_len=t_len),
        grid=(t_len // tm,),
        in_specs=[pl.BlockSpec((tm, LANES), lambda i: (i, 0)),
                  full((2 * LANES, hid)), full((1, hid)), full((hid, hid)), full((1, hid)),
                  full((hid, hid)), full((1, hid)), full((1, hid)), full((hid, 4 * HY_WIDTH)),
                  full((1, HY_WIDTH))],
        out_specs=[pl.BlockSpec((2, tm, HY_WIDTH), lambda i: (0, i, 0)),
                   pl.BlockSpec((1, HY_WIDTH), lambda i: (0, 0))],
        out_shape=[jax.ShapeDtypeStruct((2, t_len, HY_WIDTH), BF16), jax.ShapeDtypeStruct((1, HY_WIDTH), F32)],
        compiler_params=_cparams(("arbitrary",)),
        name="hyena_filter",
    )(z, bdiag(w1p), r(b1), bdiag(w2), r(b2), bdiag(w3), r(b3), r(freq), bdiag(wout), deltas)


BF16_TILE_ROWS = 16
FFT_K1_PER_STEP = 8


def _used_k1(n1):
    need = n1 // 2 + 1
    return min(n1, -(-need // BF16_TILE_ROWS) * BF16_TILE_ROWS)


def _dft_tables(t_len):
    n = 2 * t_len
    n2 = FFT_N2
    n1 = n // n2
    half = n1 // 2
    k1u = _used_k1(n1)
    k1 = np.arange(k1u)[:, None]
    j1 = np.arange(half)[None, :]
    ang1 = 2.0 * np.pi * ((k1 * j1) % n1) / n1
    f1 = np.concatenate([np.cos(ang1), -np.sin(ang1)], axis=0)
    wgt = np.where((k1 == 0) | (k1 == half), 1.0, np.where(k1 < half, 2.0, 0.0))
    i1 = np.concatenate([(wgt * np.cos(ang1)).T, (-wgt * np.sin(ang1)).T], axis=1)
    n1 = k1u
    a = np.arange(n2)
    ang2 = 2.0 * np.pi * ((a[:, None] * a[None, :]) % n2) / n2
    fr, fi = np.cos(ang2), -np.sin(ang2)
    fwd = np.block([[fr, -fi], [fi, fr]])
    inv = np.block([[fr, fi], [-fi, fr]])
    angt = 2.0 * np.pi * ((np.arange(n1)[:, None] * a[None, :]) % n) / n
    twc = np.cos(angt)[:, :, None]
    tws = np.sin(angt)[:, :, None]
    return (jnp.asarray(f1, F32).astype(BF16), jnp.asarray(i1, F32).astype(BF16), jnp.asarray(fwd, F32).astype(BF16),
            jnp.asarray(inv, F32).astype(BF16),
            jnp.broadcast_to(jnp.asarray(twc, F32), (n1, n2, LANES)),
            jnp.broadcast_to(jnp.asarray(tws, F32), (n1, n2, LANES)))


FFT_GROUP = BF16_TILE_ROWS


def _swap_table():
    g = FFT_GROUP
    r = np.arange(g * g)
    p = np.zeros((g * g, g * g), np.float32)
    p[r, (r % g) * g + r // g] = 1.0
    return jnp.asarray(p, F32).astype(BF16)


def _regroup(p, ref, n_rows, width):
    g = FFT_GROUP
    swapped = [_dot(p, ref[0, g * blk:g * (blk + 1)].reshape(g * g, width)).astype(BF16)
               for blk in range(n_rows // g)]
    return [jnp.concatenate([s[g * j:g * (j + 1)] for s in swapped], axis=0) for j in range(g)]


def _hy_s1_kernel(z_ref, f_ref, p_ref, xr_ref, xi_ref):
    g = FFT_GROUP
    half, width = z_ref.shape[1], z_ref.shape[3]
    k1u = xr_ref.shape[1]
    p = p_ref[...]
    ys = [_dot(f_ref[...], zj).astype(BF16) for zj in _regroup(p, z_ref, half, width)]
    for part, ref in enumerate((xr_ref, xi_ref)):
        for kb in range(k1u // g):
            r0 = part * k1u + g * kb
            rows = jnp.concatenate([y[r0:r0 + g] for y in ys], axis=0)
            ref[0, g * kb:g * (kb + 1)] = _dot(p, rows).astype(BF16).reshape(g, g, width)


def _hy_s1(z4, f1, swap):
    b, half, n2, w = z4.shape
    g = FFT_GROUP
    k1u = f1.shape[0] // 2
    assert half % g == 0 and n2 % g == 0 and k1u % g == 0
    return pl.pallas_call(
        _hy_s1_kernel,
        grid=(b, n2 // g),
        in_specs=[pl.BlockSpec((1, half, g, w), lambda bi, j: (bi, 0, j, 0)),
                  pl.BlockSpec((2 * k1u, half), lambda bi, j: (0, 0)),
                  pl.BlockSpec((g * g, g * g), lambda bi, j: (0, 0))],
        out_specs=[pl.BlockSpec((1, k1u, g, w), lambda bi, j: (bi, 0, j, 0))] * 2,
        out_shape=[jax.ShapeDtypeStruct((b, k1u, n2, w), BF16)] * 2,
        compiler_params=_cparams(("parallel", "parallel")),
        name="hyena_dft_outer",
    )(z4, f1, swap)


def _twiddle_fwd(xr, xi, c, s):
    return xr * c + xi * s, xi * c - xr * s


def _lane_tile(ref, j, reps):
    v = ref[j]
    return jnp.concatenate([v] * reps, axis=1)


def _hy_filt_spec_kernel(xr_ref, xi_ref, twc_ref, tws_ref, fwd_ref, l1_ref, kr_ref, ki_ref, *, n):
    reps = xr_ref.shape[-1] // LANES
    n2 = FFT_N2
    scale = 1.0 / (l1_ref[...] * n)
    for j in range(FFT_K1_PER_STEP):
        c = _lane_tile(twc_ref, j, reps)
        s = _lane_tile(tws_ref, j, reps)
        parts = []
        for f in range(2):
            ar, ai = _twiddle_fwd(xr_ref[f, j].astype(F32), xi_ref[f, j].astype(F32), c, s)
            y = _dot(fwd_ref[...], jnp.concatenate([ar, ai], axis=0).astype(BF16))
            parts.append((y[:n2], y[n2:]))
        kr_ref[j] = ((parts[0][0] + parts[1][0]) * scale).astype(BF16)
        ki_ref[j] = ((parts[0][1] - parts[1][1]) * scale).astype(BF16)


def _hy_filt_spec(xr, xi, twc, tws, fwd, l1, n):
    _, n1, n2, w = xr.shape
    kb = FFT_K1_PER_STEP
    return pl.pallas_call(
        functools.partial(_hy_filt_spec_kernel, n=n),
        grid=(n1 // kb,),
        in_specs=[pl.BlockSpec((2, kb, n2, w), lambda k: (0, k, 0, 0)),
                  pl.BlockSpec((2, kb, n2, w), lambda k: (0, k, 0, 0)),
                  pl.BlockSpec((kb, n2, LANES), lambda k: (k, 0, 0)),
                  pl.BlockSpec((kb, n2, LANES), lambda k: (k, 0, 0)),
                  pl.BlockSpec((2 * n2, 2 * n2), lambda k: (0, 0)),
                  pl.BlockSpec((1, w), lambda k: (0, 0))],
        out_specs=[pl.BlockSpec((kb, n2, w), lambda k: (k, 0, 0))] * 2,
        out_shape=[jax.ShapeDtypeStruct((n1, n2, w), BF16)] * 2,
        compiler_params=_cparams(("parallel",)),
        name="hyena_filter_spectrum",
    )(xr, xi, twc, tws, fwd, l1)


def _hy_mid_kernel(xr_ref, xi_ref, twc_ref, tws_ref, fwd_ref, inv_ref, kr_ref, ki_ref, tr_ref, ti_ref):
    reps = xr_ref.shape[-1] // LANES
    n2 = FFT_N2
    for j in range(FFT_K1_PER_STEP):
        c = _lane_tile(twc_ref, j, reps)
        s = _lane_tile(tws_ref, j, reps)
        ar, ai = _twiddle_fwd(xr_ref[0, j].astype(F32), xi_ref[0, j].astype(F32), c, s)
        y = _dot(fwd_ref[...], jnp.concatenate([ar, ai], axis=0).astype(BF16))
        yr, yi = y[:n2], y[n2:]
        kr = kr_ref[j].astype(F32)
        ki = ki_ref[j].astype(F32)
        zr = yr * kr - yi * ki
        zi = yr * ki + yi * kr
        u = _dot(inv_ref[...], jnp.concatenate([zr, zi], axis=0).astype(BF16))
        ur, ui = u[:n2], u[n2:]
        tr_ref[0, j] = (ur * c - ui * s).astype(BF16)
        ti_ref[0, j] = (ur * s + ui * c).astype(BF16)


def _hy_mid(xr, xi, twc, tws, fwd, inv, kr, ki):
    b, n1, n2, w = xr.shape
    kb = FFT_K1_PER_STEP
    xspec = pl.BlockSpec((1, kb, n2, w), lambda bi, k: (bi, k, 0, 0))
    return pl.pallas_call(
        _hy_mid_kernel,
        grid=(b, n1 // kb),
        in_specs=[xspec, xspec,
                  pl.BlockSpec((kb, n2, LANES), lambda bi, k: (k, 0, 0)),
                  pl.BlockSpec((kb, n2, LANES), lambda bi, k: (k, 0, 0)),
                  pl.BlockSpec((2 * n2, 2 * n2), lambda bi, k: (0, 0)),
                  pl.BlockSpec((2 * n2, 2 * n2), lambda bi, k: (0, 0)),
                  pl.BlockSpec((kb, n2, w), lambda bi, k: (k, 0, 0)),
                  pl.BlockSpec((kb, n2, w), lambda bi, k: (k, 0, 0))],
        out_specs=[xspec, xspec],
        out_shape=[jax.ShapeDtypeStruct((b, n1, n2, w), BF16)] * 2,
        compiler_params=_cparams(("parallel", "parallel")),
        name="hyena_dft_inner",
    )(xr, xi, twc, tws, fwd, inv, kr, ki)


def _hy_post_kernel(tr_ref, ti_ref, i1_ref, p_ref, x0_ref, z_ref, fb_ref, o_ref):
    g = FFT_GROUP
    k1u, width = tr_ref.shape[1], tr_ref.shape[3]
    half = x0_ref.shape[1]
    p = p_ref[...]
    t_re = _regroup(p, tr_ref, k1u, width)
    t_im = _regroup(p, ti_ref, k1u, width)
    ys = [_dot(i1_ref[...], jnp.concatenate([t_re[j], t_im[j]], axis=0)).astype(BF16) for j in range(g)]
    for nb in range(half // g):
        rows = jnp.concatenate([y[g * nb:g * (nb + 1)] for y in ys], axis=0)
        y_nat = _dot(p, rows)
        sl = slice(g * nb, g * (nb + 1))
        x0 = x0_ref[0, sl].reshape(g * g, width).astype(F32)
        z = z_ref[0, sl].reshape(g * g, width).astype(F32)
        o_ref[0, sl] = (x0 * (y_nat + z * fb_ref[...])).astype(BF16).reshape(g, g, width)


def _hy_post(tr, ti, i1, swap, x0_4, z4, fb):
    b, k1u, n2, w = tr.shape
    half = i1.shape[0]
    g = FFT_GROUP
    hspec = pl.BlockSpec((1, half, g, w), lambda bi, j: (bi, 0, j, 0))
    tspec = pl.BlockSpec((1, k1u, g, w), lambda bi, j: (bi, 0, j, 0))
    return pl.pallas_call(
        _hy_post_kernel,
        grid=(b, n2 // g),
        in_specs=[tspec, tspec, pl.BlockSpec((half, 2 * k1u), lambda bi, j: (0, 0)),
                  pl.BlockSpec((g * g, g * g), lambda bi, j: (0, 0)), hspec, hspec,
                  pl.BlockSpec((1, w), lambda bi, j: (0, 0))],
        out_specs=hspec,
        out_shape=jax.ShapeDtypeStruct((b, half, n2, w), BF16),
        compiler_params=_cparams(("parallel", "parallel")),
        name="hyena_dft_outer_inv",
    )(tr, ti, i1, swap, x0_4, z4, fb)


def _hyena_long(proj, conv_w, conv_b, filt, filt_bias):
    b, t, _ = proj.shape
    w = HY_WIDTH
    n = 2 * t
    n2 = FFT_N2
    n1 = n // n2
    k1u = _used_k1(n1)
    f1, i1, fwd, inv, twc, tws = _dft_tables(t)
    swap = _swap_table()
    hfb, l1 = _hy_filter(t, *filt)
    hr, hi = _hy_s1(hfb.reshape(2, n1 // 2, n2, w), f1, swap)
    kr, ki = _hy_filt_spec(hr, hi, twc, tws, fwd, l1, n)
    x0, z = _hy_pre(proj, conv_w, conv_b)
    z4 = z.reshape(b, n1 // 2, n2, w)
    xr, xi = _hy_s1(z4, f1, swap)
    tr, ti = _hy_mid(xr, xi, twc, tws, fwd, inv, kr, ki)
    d = _hy_post(tr, ti, i1, swap, x0.reshape(b, n1 // 2, n2, w), z4, filt_bias.astype(F32).reshape(1, w))
    return d.reshape(b, t, w)


def _hy_ctx_kernel(x0_ref, z_ref, h_ref, l1_ref, cm_ref, sm_ref, cmt_ref, smt_ref, fb_ref, o_ref, *, n):
    cm = cm_ref[...]
    sm = sm_ref[...]

    def spectrum(v):
        return _dot(cm, v), -_dot(sm, v)

    z = z_ref[0]
    zr, zi = spectrum(z)
    ar, ai = spectrum(h_ref[0])
    br, bi = spectrum(h_ref[1])
    scale = 1.0 / (l1_ref[...] * n)
    kr = (ar + br) * scale
    ki = (ai - bi) * scale
    yr = (zr * kr - zi * ki).astype(BF16)
    yi = (zr * ki + zi * kr).astype(BF16)
    y = _dot(cmt_ref[...], yr) - _dot(smt_ref[...], yi)
    o_ref[0] = (x0_ref[0].astype(F32) * (y + z.astype(F32) * fb_ref[...])).astype(BF16)


def _hyena_short(proj, conv_w, conv_b, filt, filt_bias):
    b, t, _ = proj.shape
    w = HY_WIDTH
    n = 2 * t
    hfb, l1 = _hy_filter(t, *filt)
    x0, z = _hy_pre(proj, conv_w, conv_b)
    k = np.arange(n)[:, None]
    j = np.arange(t)[None, :]
    ang = 2.0 * np.pi * ((k * j) % n) / n
    cm, sm = np.cos(ang), np.sin(ang)
    tabs = [jnp.asarray(a, F32).astype(BF16) for a in (cm, sm, cm.T, sm.T)]
    full2 = lambda shape: pl.BlockSpec(shape, lambda bi: (0, 0))
    bspec = pl.BlockSpec((1, t, w), lambda bi: (bi, 0, 0))
    return pl.pallas_call(
        functools.partial(_hy_ctx_kernel, n=n),
        grid=(b,),
        in_specs=[bspec, bspec, pl.BlockSpec((2, t, w), lambda bi: (0, 0, 0)), full2((1, w)),
                  full2((n, t)), full2((n, t)), full2((t, n)), full2((t, n)), full2((1, w))],
        out_specs=bspec,
        out_shape=jax.ShapeDtypeStruct((b, t, w), BF16),
        compiler_params=_cparams(("parallel",)),
        name="hyena_ctx",
    )(x0, z, hfb, l1, *tabs, filt_bias.astype(F32).reshape(1, w))


def _layer(xc, xl, mod, cos128, sin128, norm_w, w_perm, rpb, sink, w_alpha_up, b_alpha, gla_norm_w, conv_w, conv_b,
           filt, filt_bias, w_branch, w_out, final_w, layer, with_ctx_out):
    b, length, d = xl.shape
    mod_l = mod[:b].reshape(b, 1, 3 * d)
    mod_c = jnp.broadcast_to(mod[b].reshape(1, 1, 3 * d), (b, 1, 3 * d))
    sh_l, sc_l, g_l = mod_l[..., :d], mod_l[..., d:2 * d], mod_l[..., 2 * d:]
    sh_c, sc_c, g_c = mod_c[..., :d], mod_c[..., d:2 * d], mod_c[..., 2 * d:]

    wbf = w_branch.astype(BF16) * 0.5
    wb = jnp.stack([wbf[0], _permute(wbf[1], _SWA_OUT_PERM, axis=0), wbf[2], wbf[3]])
    wo = w_out.astype(BF16)

    proj_l = _inproj(xl, norm_w, sc_l, sh_l, w_perm, layer)
    proj_c = _inproj(xc, norm_w, sc_c, sh_c, w_perm, layer)

    y_a = _natten(proj_l, proj_c, _natten_bias_table(rpb))
    sink128 = jnp.broadcast_to(sink.astype(F32)[:, None], (SW_HEADS, LANES))
    qr, kr = _rope(proj_l, cos128, sin128)
    y_b = _swa(qr, kr, proj_l, proj_c, proj_c, sink128 * LOG2E)
    whi, wlo, bpad = _gla_decay_weights(w_alpha_up, b_alpha)
    s_zero = jnp.zeros((b, 2, 2, 128, 128), F32)
    of_c, s_cf = _gla_pass(proj_c, whi, wlo, bpad, s_zero, rev=False)
    y_cc, s_cb = _gla_pass(proj_c, whi, wlo, bpad, s_zero, rev=True, o_fwd=of_c, gnw=gla_norm_w.astype(F32))
    of_l, _ = _gla_pass(proj_l, whi, wlo, bpad, s_cf, rev=False)
    y_c, _ = _gla_pass(proj_l, whi, wlo, bpad, s_cb, rev=True, o_fwd=of_l, gnw=gla_norm_w.astype(F32))
    y_d = _hyena_long(proj_l, conv_w, conv_b, filt, filt_bias)

    xl_new = _merge([y_a, y_b, y_c, y_d], proj_l, wb, wo, g_l, xl, final_w, final=not with_ctx_out)
    if with_ctx_out:
        ya_c = _ctx_attn(proj_c, sink128, swa=False)
        yb_c = _ctx_attn(proj_c, sink128, swa=True)
        yd_c = _hyena_short(proj_c, conv_w, conv_b, filt, filt_bias)
        xc = _merge([ya_c, yb_c, y_cc, yd_c], proj_c, wb, wo, g_c, xc, final_w, final=False)
    return xc, xl_new


def kernel(x, c, ctx, c_ctx, norm_w, w_mod, b_mod, w_in, rpb, sink, w_alpha_up, b_alpha, gla_norm_w, conv_w, conv_b,
           filt_w1, filt_b1, filt_w2, filt_b2, filt_w3, filt_b3, filt_freq, filt_wout, filt_bias, w_branch, w_out,
           final_norm_w):
    b, length, d = x.shape
    depth = norm_w.shape[0]
    cvec = jnp.zeros((8, d), F32).at[:b].set(c.astype(F32)).at[b].set(c_ctx.astype(F32))
    mod = _modulation(cvec, w_mod.astype(F32), b_mod.astype(F32))
    cos128, sin128 = _rope_tables(length)
    w_perm = _win_prep(w_in.astype(F32))
    xc, xl = ctx, x
    for i in range(depth):
        filt = (filt_w1[i], filt_b1[i], filt_w2[i], filt_b2[i], filt_w3[i], filt_b3[i], filt_freq[i], filt_wout[i])
        xc, xl = _layer(xc, xl, mod[i], cos128, sin128, norm_w[i], w_perm, rpb[i], sink[i], w_alpha_up[i],
                        b_alpha[i], gla_norm_w[i], conv_w[i], conv_b[i], filt, filt_bias[i], w_branch[i], w_out[i],
                        final_norm_w, layer=i, with_ctx_out=(i < depth - 1))
    return xl
```

```python
import functools
import math

import numpy as np
import jax
import jax.numpy as jnp
from jax import lax
from jax.experimental import pallas as pl
from jax.experimental.pallas import tpu as pltpu

F32 = jnp.float32
BF16 = jnp.bfloat16

D_MODEL = 1024
GRID_W = 64
HEAD_DIM = 64
BRANCH_WIDTH = D_MODEL // 2
N_BRANCH = 4
NA_HEADS = 8
NA_KR = 8
NA_KC = 16
SW_HEADS = 8
SW_KV_HEADS = 2
SW_WINDOW = 128
SW_BLOCK = 128
GLA_HEADS = 4
GLA_DK = BRANCH_WIDTH // 2
GLA_DV = BRANCH_WIDTH
GLA_RANK = 16
GLA_TAU = 16.0
GLA_CHUNK = 64
HY_WIDTH = BRANCH_WIDTH
HY_EMB = 33
HY_FFN = 64
HY_TARGET = 1e-2
HY_FAST_PCT = 0.3
HY_SLOW_PCT = 1.5
ROPE_BASE = 10000.0
EPS = 1e-6
NEG_INF = -1e30
LOG2E = math.log2(math.e)
LANES = 128
FFT_N2 = 128
VMEM_LIMIT = 56 * 1024 * 1024

_IN_WIDTHS = (512, 512, 512, 512, 512, 128, 128, 512, 256, 256, 512, 16, 16, 512, 1536, 512, 4096)
_IN_OFF = np.concatenate([[0], np.cumsum(_IN_WIDTHS)])
(_O_AQ, _O_AK, _O_AV, _O_AG, _O_BQ, _O_BK, _O_BV, _O_BG, _O_CQ, _O_CK, _O_CV, _O_LRF, _O_LRB, _O_CG,
 _O_DU, _O_DG, _O_GM) = [int(v) for v in _IN_OFF[:-1]]
IN_TOTAL = int(_IN_OFF[-1])

P_AQ, P_AK, P_AV, P_AG = 0, 512, 1024, 1536
P_BQ, P_BG = 2048, 2560
P_CV, P_CG = 3072, 3584
P_DU, P_DG = 4096, 5632
P_GM = 6144
P_CQ, P_CK = 10240, 10496
P_BK, P_BV = 10752, 10880
P_LR = 11008
NP_COLS = 11264


def _swa_q_perm():
    idx = np.zeros(512, np.int64)
    for t in range(4):
        a, b = t, t + 4
        base = 128 * t
        idx[base + 0:base + 32] = 64 * a + np.arange(32)
        idx[base + 32:base + 64] = 64 * b + np.arange(32)
        idx[base + 64:base + 96] = 64 * a + 32 + np.arange(32)
        idx[base + 96:base + 128] = 64 * b + 32 + np.arange(32)
    return idx


def _swa_k_perm():
    idx = np.zeros(128, np.int64)
    idx[0:32] = np.arange(32)
    idx[32:64] = 64 + np.arange(32)
    idx[64:96] = 32 + np.arange(32)
    idx[96:128] = 96 + np.arange(32)
    return idx


def _swa_out_perm():
    idx = np.zeros(512, np.int64)
    for t in range(4):
        idx[128 * t:128 * t + 64] = 64 * t + np.arange(64)
        idx[128 * t + 64:128 * t + 128] = 64 * (t + 4) + np.arange(64)
    return idx


def _build_in_perm():
    perm = np.full(NP_COLS, -1, np.int64)

    def put(p, o, w):
        perm[p:p + w] = o + np.arange(w)

    put(P_AQ, _O_AQ, 512); put(P_AK, _O_AK, 512); put(P_AV, _O_AV, 512); put(P_AG, _O_AG, 512)
    perm[P_BQ:P_BQ + 512] = _O_BQ + _swa_q_perm()
    perm[P_BG:P_BG + 512] = _O_BG + _swa_out_perm()
    put(P_CV, _O_CV, 512); put(P_CG, _O_CG, 512)
    put(P_DU, _O_DU, 1536); put(P_DG, _O_DG, 512)
    put(P_GM, _O_GM, 4096)
    put(P_CQ, _O_CQ, 256); put(P_CK, _O_CK, 256)
    perm[P_BK:P_BK + 128] = _O_BK + _swa_k_perm()
    put(P_BV, _O_BV, 128)
    put(P_LR, _O_LRF, 16); put(P_LR + 16, _O_LRB, 16)
    return perm


def _build_half_cols():
    s = np.ones(NP_COLS, np.float32)
    for p, w in ((P_AG, 512), (P_BG, 512), (P_CG, 512), (P_DG, 512), (P_GM, 4096)):
        s[p:p + w] = 0.5
    return s


_HALF_COLS = _build_half_cols()
_IN_PERM = _build_in_perm()
_SWA_OUT_PERM = _swa_out_perm()


def _permute(w, perm, axis):
    pieces = []
    i = 0
    n = len(perm)
    while i < n:
        j = i + 1
        if perm[i] < 0:
            while j < n and perm[j] < 0:
                j += 1
            shape = list(w.shape)
            shape[axis] = j - i
            pieces.append(jnp.zeros(shape, w.dtype))
        else:
            while j < n and perm[j] == perm[j - 1] + 1:
                j += 1
            pieces.append(lax.slice_in_dim(w, int(perm[i]), int(perm[j - 1]) + 1, axis=axis))
        i = j
    return jnp.concatenate(pieces, axis=axis)


PERM_GRAIN = 32


def _win_prep_kernel(w_ref, o_ref):
    rows = w_ref.shape[1]
    for j in range(NP_COLS // LANES):
        pieces = []
        for q in range(LANES // PERM_GRAIN):
            src = int(_IN_PERM[LANES * j + PERM_GRAIN * q])
            if src < 0:
                pieces.append(jnp.zeros((rows, PERM_GRAIN), F32))
            else:
                pieces.append(w_ref[0, :, src:src + PERM_GRAIN])
        tile = jnp.concatenate(pieces, axis=1) * float(_HALF_COLS[LANES * j])
        o_ref[0, :, LANES * j:LANES * (j + 1)] = tile.astype(BF16)


def _win_prep(w_in):
    depth, d, n = w_in.shape
    grain = np.arange(0, NP_COLS, PERM_GRAIN)
    blocks = _IN_PERM.reshape(-1, PERM_GRAIN)
    assert np.all((blocks[:, :1] < 0) | (np.diff(blocks, axis=1) == 1).all(axis=1, keepdims=True)), grain
    assert np.all(_HALF_COLS.reshape(-1, LANES) == _HALF_COLS.reshape(-1, LANES)[:, :1])
    tr = 256
    return pl.pallas_call(
        _win_prep_kernel,
        grid=(depth, d // tr),
        in_specs=[pl.BlockSpec((1, tr, n), lambda l, i: (l, i, 0))],
        out_specs=pl.BlockSpec((1, tr, NP_COLS), lambda l, i: (l, i, 0)),
        out_shape=jax.ShapeDtypeStruct((depth, d, NP_COLS), BF16),
        compiler_params=_cparams(("parallel", "parallel")),
        name="win_prep",
    )(w_in)


def _cparams(sem):
    return pltpu.CompilerParams(dimension_semantics=sem, vmem_limit_bytes=VMEM_LIMIT)


def _sigmoid(x):
    return 1.0 / (1.0 + jnp.exp(-x))


def _sigmoid_tanh(x):
    return 0.5 * jnp.tanh(0.5 * x) + 0.5


def _split3(a):
    hi = a.astype(BF16)
    r1 = a - hi.astype(F32)
    mid = r1.astype(BF16)
    lo = (r1 - mid.astype(F32)).astype(BF16)
    return hi, mid, lo


def _dot(a, b):
    return jnp.dot(a, b, preferred_element_type=F32)


def _dot_nt(a, b):
    return lax.dot_general(a, b, (((1,), (1,)), ((), ())), preferred_element_type=F32)


def _dot_tn(a, b):
    return lax.dot_general(a, b, (((0,), (0,)), ((), ())), preferred_element_type=F32)


def _dot_x3(a, b):
    ah, am, _ = _split3(a)
    bh, bm, _ = _split3(b)
    return _dot(ah, bh) + (_dot(ah, bm) + _dot(am, bh))


def _mod_kernel(c_ref, w_ref, b_ref, o_ref):
    c = c_ref[...]
    s = c * _sigmoid(c)
    o_ref[0] = _dot_x3(s, w_ref[0]) + b_ref[0]


def _modulation(cvec, w_mod, b_mod):
    depth, d, n = w_mod.shape
    tn = 512
    return pl.pallas_call(
        _mod_kernel,
        grid=(depth, n // tn),
        in_specs=[pl.BlockSpec((8, d), lambda l, j: (0, 0)),
                  pl.BlockSpec((1, d, tn), lambda l, j: (l, 0, j)),
                  pl.BlockSpec((1, 1, tn), lambda l, j: (l, 0, j))],
        out_specs=pl.BlockSpec((1, 8, tn), lambda l, j: (l, 0, j)),
        out_shape=jax.ShapeDtypeStruct((depth, 8, n), F32),
        compiler_params=_cparams(("parallel", "parallel")),
        name="modulation",
    )(cvec, w_mod, b_mod.reshape(depth, 1, n))


def _inproj_kernel(x_ref, nw_ref, sc_ref, sh_ref, w_ref, o_ref, h_ref):
    @pl.when(pl.program_id(2) == 0)
    def _():
        x = x_ref[0]
        ms = jnp.mean(x * x, axis=-1, keepdims=True)
        y = x * lax.rsqrt(ms + EPS) * nw_ref[...]
        h_ref[...] = (y * (1.0 + sc_ref[0]) + sh_ref[0]).astype(BF16)

    o_ref[0] = _dot(h_ref[...], w_ref[...]).astype(BF16)


def _inproj(x, norm_w, scale, shift, w_perm, layer):
    b, t, d = x.shape
    n = w_perm.shape[2]
    tm = min(t, 2048)
    tn = 1024
    return pl.pallas_call(
        _inproj_kernel,
        grid=(b, t // tm, n // tn),
        in_specs=[pl.BlockSpec((1, tm, d), lambda bi, i, j: (bi, i, 0)),
                  pl.BlockSpec((1, d), lambda bi, i, j: (0, 0)),
                  pl.BlockSpec((1, 1, d), lambda bi, i, j: (bi, 0, 0)),
                  pl.BlockSpec((1, 1, d), lambda bi, i, j: (bi, 0, 0)),
                  pl.BlockSpec((None, d, tn), lambda bi, i, j: (layer, 0, j))],
        out_specs=pl.BlockSpec((1, tm, tn), lambda bi, i, j: (bi, i, j)),
        out_shape=jax.ShapeDtypeStruct((b, t, n), BF16),
        scratch_shapes=[pltpu.VMEM((tm, d), BF16)],
        compiler_params=_cparams(("parallel", "parallel", "arbitrary")),
        name="inproj",
    )(x, norm_w.reshape(1, d), scale, shift, w_perm)


def _merge_kernel(ya, yb, yc, yd, ga, gb, gc, gd, m0, m1, m2, m3, wb_ref, wo_ref, g_ref, x_ref, fw_ref,
                  o_ref, *, final):
    acc = None
    for i, (y, g, gm) in enumerate(((ya, ga, m0), (yb, gb, m1), (yc, gc, m2), (yd, gd, m3))):
        hg = g[0].astype(F32)
        yg = y[0].astype(F32) * hg
        ys = (yg + yg * jnp.tanh(hg)).astype(BF16)
        hp = _dot(ys, wb_ref[i])
        t = hp + hp * jnp.tanh(gm[0].astype(F32))
        acc = t if acc is None else acc + t
    out = x_ref[0] + g_ref[0] * _dot(acc.astype(BF16), wo_ref[...])
    if final:
        ms = jnp.mean(out * out, axis=-1, keepdims=True)
        out = out * lax.rsqrt(ms + EPS) * fw_ref[...]
    o_ref[0] = out


def _merge(ys, proj, w_branch, w_out, gate, x, final_w, final):
    b, t, d = x.shape
    bw = BRANCH_WIDTH
    tm = min(t, 512)
    yspec = pl.BlockSpec((1, tm, bw), lambda bi, i: (bi, i, 0))

    def pspec(col, width):
        blk = col // width
        return pl.BlockSpec((1, tm, width), lambda bi, i: (bi, i, blk))

    in_specs = ([yspec] * 4
                + [pspec(P_AG, bw), pspec(P_BG, bw), pspec(P_CG, bw), pspec(P_DG, bw)]
                + [pspec(P_GM + k * d, d) for k in range(N_BRANCH)]
                + [pl.BlockSpec((N_BRANCH, bw, d), lambda bi, i: (0, 0, 0)),
                   pl.BlockSpec((d, d), lambda bi, i: (0, 0)),
                   pl.BlockSpec((1, 1, d), lambda bi, i: (bi, 0, 0)),
                   pl.BlockSpec((1, tm, d), lambda bi, i: (bi, i, 0)),
                   pl.BlockSpec((1, d), lambda bi, i: (0, 0))])
    return pl.pallas_call(
        functools.partial(_merge_kernel, final=final),
        grid=(b, t // tm),
        in_specs=in_specs,
        out_specs=pl.BlockSpec((1, tm, d), lambda bi, i: (bi, i, 0)),
        out_shape=jax.ShapeDtypeStruct((b, t, d), F32),
        compiler_params=_cparams(("parallel", "parallel")),
        name="merge",
    )(*ys, *([proj] * 8), w_branch, w_out, gate, x, final_w.reshape(1, d))


def _rope_kernel(q_ref, k_ref, cos_ref, sin_ref, qo_ref, ko_ref):
    cos = cos_ref[...]
    sin = sin_ref[...]

    def rot(x):
        return x * cos + pltpu.roll(x, 64, 1) * sin

    for t in range(4):
        q = q_ref[0, :, 128 * t:128 * (t + 1)].astype(F32)
        qo_ref[0, :, 128 * t:128 * (t + 1)] = (rot(q) * (HEAD_DIM ** -0.5 * LOG2E)).astype(BF16)
    ko_ref[0] = rot(k_ref[0].astype(F32)).astype(BF16)


def _rope(proj, cos128, sin128):
    b, t, _ = proj.shape
    tm = min(t, 1024)
    return pl.pallas_call(
        _rope_kernel,
        grid=(b, t // tm),
        in_specs=[pl.BlockSpec((1, tm, 512), lambda bi, i: (bi, i, P_BQ // 512)),
                  pl.BlockSpec((1, tm, 128), lambda bi, i: (bi, i, P_BK // 128)),
                  pl.BlockSpec((tm, 128), lambda bi, i: (i, 0)),
                  pl.BlockSpec((tm, 128), lambda bi, i: (i, 0))],
        out_specs=[pl.BlockSpec((1, tm, 512), lambda bi, i: (bi, i, 0)),
                   pl.BlockSpec((1, tm, 128), lambda bi, i: (bi, i, 0))],
        out_shape=[jax.ShapeDtypeStruct((b, t, 512), BF16), jax.ShapeDtypeStruct((b, t, 128), BF16)],
        compiler_params=_cparams(("parallel", "parallel")),
        name="rope",
    )(proj, proj, cos128, sin128)


def _rope_tables(length):
    t = jnp.arange(length, dtype=jnp.int32)
    row = (t // GRID_W).astype(F32)
    col = (t % GRID_W).astype(F32)
    n_freq = HEAD_DIM // 4
    inv = ROPE_BASE ** (-jnp.arange(n_freq, dtype=F32) / n_freq)
    ang = jnp.concatenate([row[:, None] * inv, col[:, None] * inv], axis=-1)
    cos, sin = jnp.cos(ang), jnp.sin(ang)
    return jnp.tile(cos, (1, 4)), jnp.concatenate([-sin, -sin, sin, sin], axis=-1)


def _swa_kernel(q_ref, k_ref, v_ref, kc_ref, vc_ref, sink_ref, mask_ref, o_ref, *, length):
    blk = SW_BLOCK
    kw = 3 * blk
    nb = length // blk
    kc = kc_ref[0]
    vc = vc_ref[0]
    lane = lax.broadcasted_iota(jnp.int32, (blk, LANES), 1)
    a_lanes = (lane // 32) % 2 == 0
    lo_lanes = lane < 64
    top = lax.broadcasted_iota(jnp.int32, (2 * blk, 1), 0) < blk
    nsb = q_ref.shape[1] // blk
    units = [(sb, t) for sb in range(nsb) for t in range(4)]

    def stacked_q(sb, t):
        qt = q_ref[0, sb * blk:(sb + 1) * blk, 128 * t:128 * (t + 1)]
        zero = jnp.zeros_like(qt)
        return jnp.concatenate([jnp.where(a_lanes, qt, zero), jnp.where(a_lanes, zero, qt)], axis=0)

    sc_all = _dot_nt(jnp.concatenate([stacked_q(sb, t) for sb, t in units], axis=0), kc)
    o_win, pcs, rdens = [], [], []

    for u, (sb, t) in enumerate(units):
        i = pl.program_id(1) * nsb + sb
        start = pl.multiple_of(jnp.clip((i - 1) * blk, 0, length - kw), blk)
        variant = jnp.where(i == 0, 0, jnp.where(i == nb - 1, 2, 1))
        sw = _dot_nt(stacked_q(sb, t), k_ref[0, pl.ds(start, kw), :]) + mask_ref[variant]
        sc = sc_all[2 * blk * u:2 * blk * (u + 1)]
        sk = jnp.where(top, sink_ref[t:t + 1, 0:1], sink_ref[t + 4:t + 5, 0:1])
        m = jnp.maximum(jnp.maximum(jnp.max(sw, axis=-1, keepdims=True), jnp.max(sc, axis=-1, keepdims=True)), sk)
        pw = jnp.exp2(sw - m)
        pc = jnp.exp2(sc - m)
        den = jnp.sum(pw, axis=-1, keepdims=True) + jnp.sum(pc, axis=-1, keepdims=True) + jnp.exp2(sk - m)
        o_win.append(_dot(pw.astype(BF16), v_ref[0, pl.ds(start, kw), :]))
        pcs.append(pc.astype(BF16))
        rdens.append(1.0 / den)
    o_ctx = _dot(jnp.concatenate(pcs, axis=0), vc)
    for u, (sb, t) in enumerate(units):
        o = (o_win[u] + o_ctx[2 * blk * u:2 * blk * (u + 1)]) * rdens[u]
        o_ref[0, sb * blk:(sb + 1) * blk, 128 * t:128 * (t + 1)] = jnp.where(lo_lanes, o[:blk], o[blk:]).astype(BF16)


def _swa(qr, kr, proj_l, kc_rot, proj_c, sink128):
    b, length, _ = qr.shape
    lc = proj_c.shape[1]
    assert length >= 3 * SW_BLOCK
    tq = 8 * SW_BLOCK if length % (8 * SW_BLOCK) == 0 else SW_BLOCK
    r = np.arange(2 * SW_BLOCK)[:, None] % SW_BLOCK
    c = np.arange(3 * SW_BLOCK)[None, :]
    mask_tab = jnp.asarray(np.stack([np.where(np.abs(v * SW_BLOCK + r - c) <= SW_WINDOW, 0.0, NEG_INF)
                                     for v in range(3)]), F32)
    return pl.pallas_call(
        functools.partial(_swa_kernel, length=length),
        grid=(b, length // tq),
        in_specs=[pl.BlockSpec((1, tq, 512), lambda bi, i: (bi, i, 0)),
                  pl.BlockSpec((1, length, 128), lambda bi, i: (bi, 0, 0)),
                  pl.BlockSpec((1, length, 128), lambda bi, i: (bi, 0, P_BV // 128)),
                  pl.BlockSpec((1, lc, 128), lambda bi, i: (bi, 0, P_BK // 128)),
                  pl.BlockSpec((1, lc, 128), lambda bi, i: (bi, 0, P_BV // 128)),
                  pl.BlockSpec((8, 128), lambda bi, i: (0, 0)),
                  pl.BlockSpec((3, 2 * SW_BLOCK, 3 * SW_BLOCK), lambda bi, i: (0, 0, 0))],
        out_specs=pl.BlockSpec((1, tq, 512), lambda bi, i: (bi, i, 0)),
        out_shape=jax.ShapeDtypeStruct((b, length, 512), BF16),
        compiler_params=_cparams(("parallel", "arbitrary")),
        name="swa",
    )(qr, kr, proj_l, kc_rot, proj_c, sink128, mask_tab)


def _ctx_attn_kernel(q_ref, k_ref, v_ref, sink_ref, o_ref, *, swa):
    t_len = q_ref.shape[1]
    lane = lax.broadcasted_iota(jnp.int32, (t_len, LANES), 1)
    lo_lanes = lane < 64
    a_lanes = ((lane // 32) % 2 == 0) if swa else lo_lanes
    top = lax.broadcasted_iota(jnp.int32, (2 * t_len, 1), 0) < t_len
    for t in range(4):
        qt = (q_ref[0, :, 128 * t:128 * (t + 1)].astype(F32) * HEAD_DIM ** -0.5).astype(BF16)
        zero = jnp.zeros_like(qt)
        qq = jnp.concatenate([jnp.where(a_lanes, qt, zero), jnp.where(a_lanes, zero, qt)], axis=0)
        if swa:
            kt, vt = k_ref[0], v_ref[0]
        else:
            kt, vt = k_ref[0, :, 128 * t:128 * (t + 1)], v_ref[0, :, 128 * t:128 * (t + 1)]
        s = _dot_nt(qq, kt)
        m = jnp.max(s, axis=-1, keepdims=True)
        if swa:
            sk = jnp.where(top, sink_ref[t:t + 1, 0:1], sink_ref[t + 4:t + 5, 0:1])
            m = jnp.maximum(m, sk)
        p = jnp.exp(s - m)
        den = jnp.sum(p, axis=-1, keepdims=True)
        if swa:
            den = den + jnp.exp(sk - m)
        o = _dot(p.astype(BF16), vt) / den
        o_ref[0, :, 128 * t:128 * (t + 1)] = jnp.where(lo_lanes, o[:t_len], o[t_len:]).astype(BF16)


def _ctx_attn(proj_c, sink128, swa):
    b, lc, _ = proj_c.shape
    if swa:
        qs = pl.BlockSpec((1, lc, 512), lambda bi: (bi, 0, P_BQ // 512))
        ks = pl.BlockSpec((1, lc, 128), lambda bi: (bi, 0, P_BK // 128))
        vs = pl.BlockSpec((1, lc, 128), lambda bi: (bi, 0, P_BV // 128))
    else:
        qs = pl.BlockSpec((1, lc, 512), lambda bi: (bi, 0, P_AQ // 512))
        ks = pl.BlockSpec((1, lc, 512), lambda bi: (bi, 0, P_AK // 512))
        vs = pl.BlockSpec((1, lc, 512), lambda bi: (bi, 0, P_AV // 512))
    return pl.pallas_call(
        functools.partial(_ctx_attn_kernel, swa=swa),
        grid=(b,),
        in_specs=[qs, ks, vs, pl.BlockSpec((8, 128), lambda bi: (0, 0))],
        out_specs=pl.BlockSpec((1, lc, 512), lambda bi: (bi, 0, 0)),
        out_shape=jax.ShapeDtypeStruct((b, lc, 512), BF16),
        compiler_params=_cparams(("parallel",)),
        name="ctx_attn_swa" if swa else "ctx_attn_na",
    )(proj_c, proj_c, proj_c, sink128)


def _natten_kernel(q_ref, k_ref, v_ref, kc_ref, vc_ref, bias_ref, o_ref, *, rows, rows_per_step):
    blk = pl.program_id(2)
    kc = kc_ref[0]
    vc = vc_ref[0]
    w = GRID_W
    nkeys = NA_KR * w
    lo_all = lax.broadcasted_iota(jnp.int32, (rows_per_step * w, LANES), 1) < 64
    lo_row = lax.broadcasted_iota(jnp.int32, (w, LANES), 1) < 64
    q_all = (q_ref[0].astype(F32) * (HEAD_DIM ** -0.5 * LOG2E)).astype(BF16)
    zero = jnp.zeros_like(q_all)
    q_lo = jnp.where(lo_all, q_all, zero)
    q_hi = jnp.where(lo_all, zero, q_all)
    sc_lo = _dot_nt(q_lo, kc)
    sc_hi = _dot_nt(q_hi, kc)
    o_rows, pc_lo, pc_hi, rd_lo, rd_hi = [], [], [], [], []

    def scores(rr):
        r = blk * rows_per_step + rr
        rs = jnp.clip(r - NA_KR // 2, 0, rows - NA_KR)
        off = rs - r + (NA_KR - 1)
        k0 = pl.multiple_of(rs * w, w)
        sl = slice(rr * w, (rr + 1) * w)
        qq = jnp.concatenate([q_lo[sl], q_hi[sl]], axis=0)
        s = _dot_nt(qq, k_ref[0, pl.ds(k0, nkeys), :]) + bias_ref[off, 0]
        sc = jnp.concatenate([sc_lo[sl], sc_hi[sl]], axis=0)
        return s, sc, k0

    ahead = 2
    queue = [scores(rr) for rr in range(ahead)]
    for rr in range(rows_per_step):
        s, sc, k0 = queue.pop(0)
        if rr + ahead < rows_per_step:
            queue.append(scores(rr + ahead))
        vrows = v_ref[0, pl.ds(k0, nkeys), :]
        m = jnp.maximum(jnp.max(s, axis=-1, keepdims=True), jnp.max(sc, axis=-1, keepdims=True))
        p = jnp.exp2(s - m)
        pc = jnp.exp2(sc - m)
        rden = 1.0 / (jnp.sum(p, axis=-1, keepdims=True) + jnp.sum(pc, axis=-1, keepdims=True))
        o = _dot(p.astype(BF16), vrows)
        pcb = pc.astype(BF16)
        o_rows.append(jnp.where(lo_row, o[:w], o[w:]))
        rd_lo.append(rden[:w])
        rd_hi.append(rden[w:])
        pc_lo.append(pcb[:w])
        pc_hi.append(pcb[w:])
    oc = jnp.where(lo_all, _dot(jnp.concatenate(pc_lo, axis=0), vc), _dot(jnp.concatenate(pc_hi, axis=0), vc))
    rd = jnp.where(lo_all, jnp.concatenate(rd_lo, axis=0), jnp.concatenate(rd_hi, axis=0))
    o_ref[0] = ((jnp.concatenate(o_rows, axis=0) + oc) * rd).astype(BF16)


def _natten(proj_l, proj_c, bias_tab):
    b, length, _ = proj_l.shape
    lc = proj_c.shape[1]
    rows = length // GRID_W
    assert rows >= NA_KR
    rps = 32 if rows % 32 == 0 else 8
    tq = rps * GRID_W
    nk = NA_KR * GRID_W
    return pl.pallas_call(
        functools.partial(_natten_kernel, rows=rows, rows_per_step=rps),
        grid=(b, 4, rows // rps),
        in_specs=[pl.BlockSpec((1, tq, 128), lambda bi, hp, i: (bi, i, P_AQ // 128 + hp)),
                  pl.BlockSpec((1, length, 128), lambda bi, hp, i: (bi, 0, P_AK // 128 + hp)),
                  pl.BlockSpec((1, length, 128), lambda bi, hp, i: (bi, 0, P_AV // 128 + hp)),
                  pl.BlockSpec((1, lc, 128), lambda bi, hp, i: (bi, 0, P_AK // 128 + hp)),
                  pl.BlockSpec((1, lc, 128), lambda bi, hp, i: (bi, 0, P_AV // 128 + hp)),
                  pl.BlockSpec((NA_KR, 1, 2 * GRID_W, nk), lambda bi, hp, i: (0, hp, 0, 0))],
        out_specs=pl.BlockSpec((1, tq, 128), lambda bi, hp, i: (bi, i, hp)),
        out_shape=jax.ShapeDtypeStruct((b, length, 512), BF16),
        compiler_params=_cparams(("parallel", "parallel", "arbitrary")),
        name="natten",
    )(proj_l, proj_l, proj_l, proj_c, proj_c, bias_tab)


def _natten_bias_table(rpb):
    w = GRID_W
    nd = 2 * NA_KC - 1
    qc = np.arange(w)[:, None]
    kcol = np.arange(w)[None, :]
    cs = np.clip(qc - NA_KC // 2, 0, w - NA_KC)
    cmask = (kcol >= cs) & (kcol < cs + NA_KC)
    period = nd + w
    padded = jnp.pad(rpb.astype(F32), ((0, 0), (0, 0), (0, w)))
    flat = jnp.tile(padded, (1, 1, w))[:, :, :w * (period - 1)]
    t = flat.reshape(NA_HEADS, 2 * NA_KR - 1, w, period - 1)[..., NA_KC - 1:NA_KC - 1 + w]
    g = jnp.stack([t[:, off:off + NA_KR] for off in range(NA_KR)])
    g = jnp.where(cmask[None, None, None], g * LOG2E, NEG_INF)
    return g.transpose(0, 1, 3, 2, 4).reshape(NA_KR, NA_HEADS // 2, 2 * w, NA_KR * w)


def _gla_kernel(*refs, rev, ncb, final):
    if final:
        (q_ref, k_ref, v_ref, lr_ref, whi_ref, wlo_ref, b_ref, s0_ref, of_ref, gnw_ref,
         o_ref, sfin_ref, st_ref) = refs
    else:
        (q_ref, k_ref, v_ref, lr_ref, whi_ref, wlo_ref, b_ref, s0_ref,
         o_ref, sfin_ref, st_ref) = refs
    ch = GLA_CHUNK
    streams = [(bb, p) for bb in range(q_ref.shape[0]) for p in range(2)]

    @pl.when(pl.program_id(0) == 0)
    def _():
        st_ref[...] = s0_ref[...]

    tb = ncb * ch
    rix = lax.broadcasted_iota(jnp.int32, (ch, ch), 0)
    cix = lax.broadcasted_iota(jnp.int32, (ch, ch), 1)
    tri = (rix <= cix) if rev else (rix >= cix)
    lo_lanes = lax.broadcasted_iota(jnp.int32, (tb, LANES), 1) < 64
    end = 0 if rev else ch - 1
    pos = lax.broadcasted_iota(jnp.int32, (tb, 1), 0) % ch
    order = [(ncb - 1 - cc) if rev else cc for cc in range(ncb)]

    def prefix(s):
        bb, p = s
        lr = lr_ref[bb]
        arg = _dot(lr, whi_ref[p]) + _dot(lr, wlo_ref[p]) + b_ref[p]
        cum = (jnp.minimum(arg, 0.0) - jnp.log(1.0 + jnp.exp(-jnp.abs(arg)))) * (1.0 / GLA_TAU)
        step = 1
        while step < ch:
            if rev:
                cum = cum + jnp.where(pos < ch - step, pltpu.roll(cum, tb - step, 0), 0.0)
            else:
                cum = cum + jnp.where(pos >= step, pltpu.roll(cum, step, 0), 0.0)
            step *= 2
        q = q_ref[bb, :, LANES * p:LANES * (p + 1)].astype(F32)
        k = k_ref[bb, :, LANES * p:LANES * (p + 1)].astype(F32)
        qd = (q * (HEAD_DIM ** -0.5) * jnp.exp(cum)).astype(BF16)
        kd = (k * jnp.exp(-cum)).astype(BF16)
        zero = jnp.zeros_like(qd)
        return cum, k, kd, (jnp.where(lo_lanes, qd, zero), jnp.where(lo_lanes, zero, qd))

    def intra_chunks(s, pre):
        cum, k, kd, qd_h = pre
        bb, p = s
        res = {}
        for c in order:
            sl = slice(c * ch, (c + 1) * ch)
            cum_c = cum[sl]
            cum_end = cum_c[end:end + 1, :]
            kdec = (k[sl] * jnp.exp(cum_end - cum_c)).astype(BF16)
            for h in range(2):
                vh = v_ref[bb, sl, 128 * (2 * p + h):128 * (2 * p + h + 1)]
                a = jnp.where(tri, _dot_nt(qd_h[h][sl], kd[sl]), 0.0)
                res[c, h] = (_dot(a.astype(BF16), vh), _dot_tn(vh, kdec), jnp.exp(cum_end))
        return res

    def recurrence(s, pre, res):
        qd_h = pre[3]
        bb, p = s
        st = [st_ref[bb, p, 0], st_ref[bb, p, 1]]
        for c in order:
            sl = slice(c * ch, (c + 1) * ch)
            for h in range(2):
                o_intra, kv_t, gdec = res[c, h]
                o = o_intra + _dot_nt(qd_h[h][sl], st[h].astype(BF16))
                st[h] = st[h] * gdec + kv_t
                cols = slice(128 * (2 * p + h), 128 * (2 * p + h + 1))
                if final:
                    o = o + of_ref[bb, sl, cols]
                    ms = jnp.mean(o * o, axis=-1, keepdims=True)
                    o = o * lax.rsqrt(ms + EPS) * gnw_ref[...]
                    o_ref[bb, sl, cols] = o.astype(BF16)
                else:
                    o_ref[bb, sl, cols] = o
        st_ref[bb, p, 0] = st[0]
        st_ref[bb, p, 1] = st[1]

    pres = [prefix(s) for s in streams]
    ress = [intra_chunks(s, pre) for s, pre in zip(streams, pres)]
    for s, pre, res in zip(streams, pres, ress):
        recurrence(s, pre, res)
    sfin_ref[...] = st_ref[...]


def _gla_pass(proj, w_pad_hi, w_pad_lo, b_pad, s0, rev, o_fwd=None, gnw=None):
    b, t, _ = proj.shape
    final = o_fwd is not None
    tb = min(t, 512)
    nblk = t // tb
    ncb = tb // GLA_CHUNK
    d = 1 if rev else 0

    def bi_map(i):
        return (nblk - 1 - i) if rev else i

    state_spec = pl.BlockSpec((b, 2, 2, 128, 128), lambda i: (0, 0, 0, 0, 0))
    in_specs = [pl.BlockSpec((b, tb, 256), lambda i: (0, bi_map(i), P_CQ // 256)),
                pl.BlockSpec((b, tb, 256), lambda i: (0, bi_map(i), P_CK // 256)),
                pl.BlockSpec((b, tb, 512), lambda i: (0, bi_map(i), P_CV // 512)),
                pl.BlockSpec((b, tb, 128), lambda i: (0, bi_map(i), P_LR // 128)),
                pl.BlockSpec((2, 128, 128), lambda i: (d, 0, 0)),
                pl.BlockSpec((2, 128, 128), lambda i: (d, 0, 0)),
                pl.BlockSpec((2, 1, 128), lambda i: (d, 0, 0)),
                state_spec]
    args = [proj, proj, proj, proj, w_pad_hi, w_pad_lo, b_pad, s0]
    if final:
        in_specs += [pl.BlockSpec((b, tb, 512), lambda i: (0, bi_map(i), 0)),
                     pl.BlockSpec((1, 128), lambda i: (0, 0))]
        args += [o_fwd, gnw.reshape(1, 128)]
    out, s_fin = pl.pallas_call(
        functools.partial(_gla_kernel, rev=rev, ncb=ncb, final=final),
        grid=(nblk,),
        in_specs=in_specs,
        out_specs=[pl.BlockSpec((b, tb, 512), lambda i: (0, bi_map(i), 0)), state_spec],
        out_shape=[jax.ShapeDtypeStruct((b, t, 512), BF16 if final else F32),
                   jax.ShapeDtypeStruct((b, 2, 2, 128, 128), F32)],
        scratch_shapes=[pltpu.VMEM((b, 2, 2, 128, 128), F32)],
        compiler_params=_cparams(("arbitrary",)),
        name="gla_rev" if rev else "gla_fwd",
    )(*args)
    return out, s_fin


def _gla_decay_weights(w_alpha_up, b_alpha):
    w = jnp.zeros((2, 2, 128, 128), F32)
    for d in range(2):
        for p in range(2):
            w = w.at[d, p, 16 * d:16 * d + 16, :].set(w_alpha_up[d][:, 128 * p:128 * (p + 1)].astype(F32))
    w = w.reshape(4, 128, 128)
    hi = w.astype(BF16)
    lo = (w - hi.astype(F32)).astype(BF16)
    return hi, lo, b_alpha.astype(F32).reshape(4, 1, 128)


def _hy_pre_kernel(u0, u1, u2, p0, p1, p2, n0, n1, n2, w_ref, b_ref, sup_ref, sdn_ref, x0_ref, z_ref, *, nblk):
    i = pl.program_id(1)
    tm = u0.shape[1]
    sb = LANES
    nsb = tm // sb
    row8 = lax.broadcasted_iota(jnp.int32, (8, 1), 0)
    has_prev = (i > 0).astype(F32)
    has_next = (i < nblk - 1).astype(F32)
    sup = sup_ref[...]
    sdn = sdn_ref[...]

    def conv(u_ref, p_ref, n_ref, j):
        ub = u_ref[0]
        u = ub.astype(F32)
        prev = p_ref[0, 7:8, :].astype(F32) * has_prev
        nxt = n_ref[0, 0:1, :].astype(F32) * has_next
        ups = [_dot(sup[:, sb:], ub[:sb])] + [_dot(sup, ub[sb * (k - 1):sb * (k + 1)]) for k in range(1, nsb)]
        dns = [_dot(sdn, ub[sb * k:sb * (k + 2)]) for k in range(nsb - 1)] + [_dot(sdn[:, :sb], ub[sb * (nsb - 1):])]
        up = jnp.concatenate(ups, axis=0)
        dn = jnp.concatenate(dns, axis=0)
        up = jnp.concatenate([up[:8] + jnp.where(row8 == 0, prev, 0.0), up[8:]], axis=0)
        dn = jnp.concatenate([dn[:tm - 8], dn[tm - 8:] + jnp.where(row8 == 7, nxt, 0.0)], axis=0)
        w = w_ref[:, 512 * j:512 * (j + 1)]
        return up * w[0:1] + u * w[1:2] + dn * w[2:3] + b_ref[:, 512 * j:512 * (j + 1)]

    x0_ref[0] = conv(u0, p0, n0, 0).astype(BF16)
    z_ref[0] = (conv(u1, p1, n1, 1) * conv(u2, p2, n2, 2)).astype(BF16)


def _hy_pre(proj, conv_w, conv_b):
    b, t, _ = proj.shape
    tm = min(t, 512)
    nblk = t // tm
    hb = tm // 8
    nrb = t // 8
    c0 = P_DU // 512
    main = [pl.BlockSpec((1, tm, 512), functools.partial(lambda bi, i, j: (bi, i, c0 + j), j=j)) for j in range(3)]
    prev = [pl.BlockSpec((1, 8, 512), functools.partial(lambda bi, i, j: (bi, jnp.maximum(i * hb - 1, 0), c0 + j), j=j))
            for j in range(3)]
    nxt = [pl.BlockSpec((1, 8, 512),
                        functools.partial(lambda bi, i, j: (bi, jnp.minimum((i + 1) * hb, nrb - 1), c0 + j), j=j))
           for j in range(3)]
    return pl.pallas_call(
        functools.partial(_hy_pre_kernel, nblk=nblk),
        grid=(b, nblk),
        in_specs=main + prev + nxt + [pl.BlockSpec((3, 1536), lambda bi, i: (0, 0)),
                                      pl.BlockSpec((1, 1536), lambda bi, i: (0, 0)),
                                      pl.BlockSpec((LANES, 2 * LANES), lambda bi, i: (0, 0)),
                                      pl.BlockSpec((LANES, 2 * LANES), lambda bi, i: (0, 0))],
        out_specs=[pl.BlockSpec((1, tm, 512), lambda bi, i: (bi, i, 0))] * 2,
        out_shape=[jax.ShapeDtypeStruct((b, t, 512), BF16)] * 2,
        compiler_params=_cparams(("parallel", "parallel")),
        name="hyena_pre",
    )(*([proj] * 9), conv_w.astype(F32), conv_b.astype(F32).reshape(1, 1536),
      jnp.asarray(np.eye(LANES, 2 * LANES, LANES - 1), F32).astype(BF16),
      jnp.asarray(np.eye(LANES, 2 * LANES, 1), F32).astype(BF16))


def _hy_filter_kernel(z_ref, w1, b1, w2, b2, w3, b3, fr, wo, dl, h_ref, l1_ref, *, tm, t_len):
    i = pl.program_id(0)
    f = fr[...]
    half = tm // 2
    z = z_ref[...]
    hh = jnp.concatenate([z[:half], z[half:]], axis=1)
    hh = jnp.sin(f * (_dot_x3(hh, w1[...]) + b1[...]))
    hh = jnp.sin(f * (_dot_x3(hh, w2[...]) + b2[...]))
    hh = jnp.sin(f * (_dot_x3(hh, w3[...]) + b3[...]))
    hh = _dot_x3(hh, wo[...])
    hh = jnp.concatenate([hh[:, :2 * HY_WIDTH], hh[:, 2 * HY_WIDTH:]], axis=0)
    row = i * tm + lax.broadcasted_iota(jnp.int32, (tm, 1), 0)
    t = row.astype(F32) / (t_len - 1)
    decay = jnp.exp(-t * dl[...])
    h_f = hh[:, :HY_WIDTH] * decay
    h_b = jnp.where(row == 0, 0.0, hh[:, HY_WIDTH:] * decay)
    h_ref[0] = h_f.astype(BF16)
    h_ref[1] = h_b.astype(BF16)
    part = jnp.sum(jnp.abs(h_f) + jnp.abs(h_b), axis=0, keepdims=True)

    @pl.when(i == 0)
    def _():
        l1_ref[...] = part

    @pl.when(i > 0)
    def _():
        l1_ref[...] = l1_ref[...] + part


def _hy_filter(t_len, w1, b1, w2, b2, w3, b3, freq, wout):
    tm = min(t_len, 512)
    t = jnp.linspace(0.0, 1.0, t_len, dtype=F32)[:, None]
    bands = (HY_EMB - 1) // 2
    w_ang = 2.0 * math.pi * jnp.arange(t_len, dtype=F32)[:, None] / t_len
    f = jnp.linspace(1e-4, bands - 1, bands, dtype=F32)[None, :]
    z = jnp.concatenate([t, jnp.cos(f * w_ang), -jnp.sin(f * w_ang),
                         jnp.zeros((t_len, LANES - HY_EMB), F32)], axis=-1)
    w1p = jnp.concatenate([w1.astype(F32), jnp.zeros((LANES - HY_EMB, HY_FFN), F32)], axis=0)
    deltas = jnp.abs(jnp.linspace(math.log(HY_TARGET) / HY_SLOW_PCT, math.log(HY_TARGET) / HY_FAST_PCT,
                                  HY_WIDTH, dtype=F32))[None, :]
    full = lambda shape: pl.BlockSpec(shape, lambda i: tuple(0 for _ in shape))
    r = lambda v: jnp.tile(v.astype(F32).reshape(1, -1), (1, 2))

    def bdiag(w):
        w = w.astype(F32)
        zero = jnp.zeros_like(w)
        return jnp.concatenate([jnp.concatenate([w, zero], axis=1), jnp.concatenate([zero, w], axis=1)], axis=0)

    hid = 2 * HY_FFN
    return pl.pallas_call(
        functools.partial(_hy_filter_kernel, tm=tm, t_len=t_len),
        grid=(t_len // tm,),
        in_specs=[pl.BlockSpec((tm, LANES), lambda i: (i, 0)),
                  full((2 * LANES, hid)), full((1, hid)), full((hid, hid)), full((1, hid)),
                  full((hid, hid)), full((1, hid)), full((1, hid)), full((hid, 4 * HY_WIDTH)),
                  full((1, HY_WIDTH))],
        out_specs=[pl.BlockSpec((2, tm, HY_WIDTH), lambda i: (0, i, 0)),
                   pl.BlockSpec((1, HY_WIDTH), lambda i: (0, 0))],
        out_shape=[jax.ShapeDtypeStruct((2, t_len, HY_WIDTH), BF16), jax.ShapeDtypeStruct((1, HY_WIDTH), F32)],
        compiler_params=_cparams(("arbitrary",)),
        name="hyena_filter",
    )(z, bdiag(w1p), r(b1), bdiag(w2), r(b2), bdiag(w3), r(b3), r(freq), bdiag(wout), deltas)


BF16_TILE_ROWS = 16
FFT_K1_PER_STEP = 8


def _used_k1(n1):
    need = n1 // 2 + 1
    return min(n1, -(-need // BF16_TILE_ROWS) * BF16_TILE_ROWS)


def _dft_tables(t_len):
    n = 2 * t_len
    n2 = FFT_N2
    n1 = n // n2
    half = n1 // 2
    k1u = _used_k1(n1)
    k1 = np.arange(k1u)[:, None]
    j1 = np.arange(half)[None, :]
    ang1 = 2.0 * np.pi * ((k1 * j1) % n1) / n1
    f1 = np.concatenate([np.cos(ang1), -np.sin(ang1)], axis=0)
    wgt = np.where((k1 == 0) | (k1 == half), 1.0, np.where(k1 < half, 2.0, 0.0))
    i1 = np.concatenate([(wgt * np.cos(ang1)).T, (-wgt * np.sin(ang1)).T], axis=1)
    n1 = k1u
    a = np.arange(n2)
    ang2 = 2.0 * np.pi * ((a[:, None] * a[None, :]) % n2) / n2
    fr, fi = np.cos(ang2), -np.sin(ang2)
    fwd = np.block([[fr, -fi], [fi, fr]])
    inv = np.block([[fr, fi], [-fi, fr]])
    angt = 2.0 * np.pi * ((np.arange(n1)[:, None] * a[None, :]) % n) / n
    twc = np.cos(angt)[:, :, None]
    tws = np.sin(angt)[:, :, None]
    return (jnp.asarray(f1, F32).astype(BF16), jnp.asarray(i1, F32).astype(BF16), jnp.asarray(fwd, F32).astype(BF16),
            jnp.asarray(inv, F32).astype(BF16),
            jnp.broadcast_to(jnp.asarray(twc, F32), (n1, n2, LANES)),
            jnp.broadcast_to(jnp.asarray(tws, F32), (n1, n2, LANES)))


FFT_GROUP = BF16_TILE_ROWS


def _swap_table():
    g = FFT_GROUP
    r = np.arange(g * g)
    p = np.zeros((g * g, g * g), np.float32)
    p[r, (r % g) * g + r // g] = 1.0
    return jnp.asarray(p, F32).astype(BF16)


def _regroup(p, ref, n_rows, width):
    g = FFT_GROUP
    swapped = [_dot(p, ref[0, g * blk:g * (blk + 1)].reshape(g * g, width)).astype(BF16)
               for blk in range(n_rows // g)]
    return [jnp.concatenate([s[g * j:g * (j + 1)] for s in swapped], axis=0) for j in range(g)]


def _hy_s1_kernel(z_ref, f_ref, p_ref, xr_ref, xi_ref):
    g = FFT_GROUP
    half, width = z_ref.shape[1], z_ref.shape[3]
    k1u = xr_ref.shape[1]
    p = p_ref[...]
    ys = [_dot(f_ref[...], zj).astype(BF16) for zj in _regroup(p, z_ref, half, width)]
    for part, ref in enumerate((xr_ref, xi_ref)):
        for kb in range(k1u // g):
            r0 = part * k1u + g * kb
            rows = jnp.concatenate([y[r0:r0 + g] for y in ys], axis=0)
            ref[0, g * kb:g * (kb + 1)] = _dot(p, rows).astype(BF16).reshape(g, g, width)


def _hy_s1(z4, f1, swap):
    b, half, n2, w = z4.shape
    g = FFT_GROUP
    k1u = f1.shape[0] // 2
    assert half % g == 0 and n2 % g == 0 and k1u % g == 0
    return pl.pallas_call(
        _hy_s1_kernel,
        grid=(b, n2 // g),
        in_specs=[pl.BlockSpec((1, half, g, w), lambda bi, j: (bi, 0, j, 0)),
                  pl.BlockSpec((2 * k1u, half), lambda bi, j: (0, 0)),
                  pl.BlockSpec((g * g, g * g), lambda bi, j: (0, 0))],
        out_specs=[pl.BlockSpec((1, k1u, g, w), lambda bi, j: (bi, 0, j, 0))] * 2,
        out_shape=[jax.ShapeDtypeStruct((b, k1u, n2, w), BF16)] * 2,
        compiler_params=_cparams(("parallel", "parallel")),
        name="hyena_dft_outer",
    )(z4, f1, swap)


def _twiddle_fwd(xr, xi, c, s):
    return xr * c + xi * s, xi * c - xr * s


def _lane_tile(ref, j, reps):
    v = ref[j]
    return jnp.concatenate([v] * reps, axis=1)


def _hy_filt_spec_kernel(xr_ref, xi_ref, twc_ref, tws_ref, fwd_ref, l1_ref, kr_ref, ki_ref, *, n):
    reps = xr_ref.shape[-1] // LANES
    n2 = FFT_N2
    scale = 1.0 / (l1_ref[...] * n)
    for j in range(FFT_K1_PER_STEP):
        c = _lane_tile(twc_ref, j, reps)
        s = _lane_tile(tws_ref, j, reps)
        parts = []
        for f in range(2):
            ar, ai = _twiddle_fwd(xr_ref[f, j].astype(F32), xi_ref[f, j].astype(F32), c, s)
            y = _dot(fwd_ref[...], jnp.concatenate([ar, ai], axis=0).astype(BF16))
            parts.append((y[:n2], y[n2:]))
        kr_ref[j] = ((parts[0][0] + parts[1][0]) * scale).astype(BF16)
        ki_ref[j] = ((parts[0][1] - parts[1][1]) * scale).astype(BF16)


def _hy_filt_spec(xr, xi, twc, tws, fwd, l1, n):
    _, n1, n2, w = xr.shape
    kb = FFT_K1_PER_STEP
    return pl.pallas_call(
        functools.partial(_hy_filt_spec_kernel, n=n),
        grid=(n1 // kb,),
        in_specs=[pl.BlockSpec((2, kb, n2, w), lambda k: (0, k, 0, 0)),
                  pl.BlockSpec((2, kb, n2, w), lambda k: (0, k, 0, 0)),
                  pl.BlockSpec((kb, n2, LANES), lambda k: (k, 0, 0)),
                  pl.BlockSpec((kb, n2, LANES), lambda k: (k, 0, 0)),
                  pl.BlockSpec((2 * n2, 2 * n2), lambda k: (0, 0)),
                  pl.BlockSpec((1, w), lambda k: (0, 0))],
        out_specs=[pl.BlockSpec((kb, n2, w), lambda k: (k, 0, 0))] * 2,
        out_shape=[jax.ShapeDtypeStruct((n1, n2, w), BF16)] * 2,
        compiler_params=_cparams(("parallel",)),
        name="hyena_filter_spectrum",
    )(xr, xi, twc, tws, fwd, l1)


def _hy_mid_kernel(xr_ref, xi_ref, twc_ref, tws_ref, fwd_ref, inv_ref, kr_ref, ki_ref, tr_ref, ti_ref):
    reps = xr_ref.shape[-1] // LANES
    n2 = FFT_N2
    for j in range(FFT_K1_PER_STEP):
        c = _lane_tile(twc_ref, j, reps)
        s = _lane_tile(tws_ref, j, reps)
        ar, ai = _twiddle_fwd(xr_ref[0, j].astype(F32), xi_ref[0, j].astype(F32), c, s)
        y = _dot(fwd_ref[...], jnp.concatenate([ar, ai], axis=0).astype(BF16))
        yr, yi = y[:n2], y[n2:]
        kr = kr_ref[j].astype(F32)
        ki = ki_ref[j].astype(F32)
        zr = yr * kr - yi * ki
        zi = yr * ki + yi * kr
        u = _dot(inv_ref[...], jnp.concatenate([zr, zi], axis=0).astype(BF16))
        ur, ui = u[:n2], u[n2:]
        tr_ref[0, j] = (ur * c - ui * s).astype(BF16)
        ti_ref[0, j] = (ur * s + ui * c).astype(BF16)


def _hy_mid(xr, xi, twc, tws, fwd, inv, kr, ki):
    b, n1, n2, w = xr.shape
    kb = FFT_K1_PER_STEP
    xspec = pl.BlockSpec((1, kb, n2, w), lambda bi, k: (bi, k, 0, 0))
    return pl.pallas_call(
        _hy_mid_kernel,
        grid=(b, n1 // kb),
        in_specs=[xspec, xspec,
                  pl.BlockSpec((kb, n2, LANES), lambda bi, k: (k, 0, 0)),
                  pl.BlockSpec((kb, n2, LANES), lambda bi, k: (k, 0, 0)),
                  pl.BlockSpec((2 * n2, 2 * n2), lambda bi, k: (0, 0)),
                  pl.BlockSpec((2 * n2, 2 * n2), lambda bi, k: (0, 0)),
                  pl.BlockSpec((kb, n2, w), lambda bi, k: (k, 0, 0)),
                  pl.BlockSpec((kb, n2, w), lambda bi, k: (k, 0, 0))],
        out_specs=[xspec, xspec],
        out_shape=[jax.ShapeDtypeStruct((b, n1, n2, w), BF16)] * 2,
        compiler_params=_cparams(("parallel", "parallel")),
        name="hyena_dft_inner",
    )(xr, xi, twc, tws, fwd, inv, kr, ki)


def _hy_post_kernel(tr_ref, ti_ref, i1_ref, p_ref, x0_ref, z_ref, fb_ref, o_ref):
    g = FFT_GROUP
    k1u, width = tr_ref.shape[1], tr_ref.shape[3]
    half = x0_ref.shape[1]
    p = p_ref[...]
    t_re = _regroup(p, tr_ref, k1u, width)
    t_im = _regroup(p, ti_ref, k1u, width)
    ys = [_dot(i1_ref[...], jnp.concatenate([t_re[j], t_im[j]], axis=0)).astype(BF16) for j in range(g)]
    for nb in range(half // g):
        rows = jnp.concatenate([y[g * nb:g * (nb + 1)] for y in ys], axis=0)
        y_nat = _dot(p, rows)
        sl = slice(g * nb, g * (nb + 1))
        x0 = x0_ref[0, sl].reshape(g * g, width).astype(F32)
        z = z_ref[0, sl].reshape(g * g, width).astype(F32)
        o_ref[0, sl] = (x0 * (y_nat + z * fb_ref[...])).astype(BF16).reshape(g, g, width)


def _hy_post(tr, ti, i1, swap, x0_4, z4, fb):
    b, k1u, n2, w = tr.shape
    half = i1.shape[0]
    g = FFT_GROUP
    hspec = pl.BlockSpec((1, half, g, w), lambda bi, j: (bi, 0, j, 0))
    tspec = pl.BlockSpec((1, k1u, g, w), lambda bi, j: (bi, 0, j, 0))
    return pl.pallas_call(
        _hy_post_kernel,
        grid=(b, n2 // g),
        in_specs=[tspec, tspec, pl.BlockSpec((half, 2 * k1u), lambda bi, j: (0, 0)),
                  pl.BlockSpec((g * g, g * g), lambda bi, j: (0, 0)), hspec, hspec,
                  pl.BlockSpec((1, w), lambda bi, j: (0, 0))],
        out_specs=hspec,
        out_shape=jax.ShapeDtypeStruct((b, half, n2, w), BF16),
        compiler_params=_cparams(("parallel", "parallel")),
        name="hyena_dft_outer_inv",
    )(tr, ti, i1, swap, x0_4, z4, fb)


def _hyena_long(proj, conv_w, conv_b, filt, filt_bias):
    b, t, _ = proj.shape
    w = HY_WIDTH
    n = 2 * t
    n2 = FFT_N2
    n1 = n // n2
    k1u = _used_k1(n1)
    f1, i1, fwd, inv, twc, tws = _dft_tables(t)
    swap = _swap_table()
    hfb, l1 = _hy_filter(t, *filt)
    hr, hi = _hy_s1(hfb.reshape(2, n1 // 2, n2, w), f1, swap)
    kr, ki = _hy_filt_spec(hr, hi, twc, tws, fwd, l1, n)
    x0, z = _hy_pre(proj, conv_w, conv_b)
    z4 = z.reshape(b, n1 // 2, n2, w)
    xr, xi = _hy_s1(z4, f1, swap)
    tr, ti = _hy_mid(xr, xi, twc, tws, fwd, inv, kr, ki)
    d = _hy_post(tr, ti, i1, swap, x0.reshape(b, n1 // 2, n2, w), z4, filt_bias.astype(F32).reshape(1, w))
    return d.reshape(b, t, w)


def _hy_ctx_kernel(x0_ref, z_ref, h_ref, l1_ref, cm_ref, sm_ref, cmt_ref, smt_ref, fb_ref, o_ref, *, n):
    cm = cm_ref[...]
    sm = sm_ref[...]

    def spectrum(v):
        return _dot(cm, v), -_dot(sm, v)

    z = z_ref[0]
    zr, zi = spectrum(z)
    ar, ai = spectrum(h_ref[0])
    br, bi = spectrum(h_ref[1])
    scale = 1.0 / (l1_ref[...] * n)
    kr = (ar + br) * scale
    ki = (ai - bi) * scale
    yr = (zr * kr - zi * ki).astype(BF16)
    yi = (zr * ki + zi * kr).astype(BF16)
    y = _dot(cmt_ref[...], yr) - _dot(smt_ref[...], yi)
    o_ref[0] = (x0_ref[0].astype(F32) * (y + z.astype(F32) * fb_ref[...])).astype(BF16)


def _hyena_short(proj, conv_w, conv_b, filt, filt_bias):
    b, t, _ = proj.shape
    w = HY_WIDTH
    n = 2 * t
    hfb, l1 = _hy_filter(t, *filt)
    x0, z = _hy_pre(proj, conv_w, conv_b)
    k = np.arange(n)[:, None]
    j = np.arange(t)[None, :]
    ang = 2.0 * np.pi * ((k * j) % n) / n
    cm, sm = np.cos(ang), np.sin(ang)
    tabs = [jnp.asarray(a, F32).astype(BF16) for a in (cm, sm, cm.T, sm.T)]
    full2 = lambda shape: pl.BlockSpec(shape, lambda bi: (0, 0))
    bspec = pl.BlockSpec((1, t, w), lambda bi: (bi, 0, 0))
    return pl.pallas_call(
        functools.partial(_hy_ctx_kernel, n=n),
        grid=(b,),
        in_specs=[bspec, bspec, pl.BlockSpec((2, t, w), lambda bi: (0, 0, 0)), full2((1, w)),
                  full2((n, t)), full2((n, t)), full2((t, n)), full2((t, n)), full2((1, w))],
        out_specs=bspec,
        out_shape=jax.ShapeDtypeStruct((b, t, w), BF16),
        compiler_params=_cparams(("parallel",)),
        name="hyena_ctx",
    )(x0, z, hfb, l1, *tabs, filt_bias.astype(F32).reshape(1, w))


def _layer(xc, xl, mod, cos128, sin128, norm_w, w_perm, rpb, sink, w_alpha_up, b_alpha, gla_norm_w, conv_w, conv_b,
           filt, filt_bias, w_branch, w_out, final_w, layer, with_ctx_out):
    b, length, d = xl.shape
    mod_l = mod[:b].reshape(b, 1, 3 * d)
    mod_c = jnp.broadcast_to(mod[b].reshape(1, 1, 3 * d), (b, 1, 3 * d))
    sh_l, sc_l, g_l = mod_l[..., :d], mod_l[..., d:2 * d], mod_l[..., 2 * d:]
    sh_c, sc_c, g_c = mod_c[..., :d], mod_c[..., d:2 * d], mod_c[..., 2 * d:]

    wbf = w_branch.astype(BF16) * 0.5
    wb = jnp.stack([wbf[0], _permute(wbf[1], _SWA_OUT_PERM, axis=0), wbf[2], wbf[3]])
    wo = w_out.astype(BF16)

    proj_l = _inproj(xl, norm_w, sc_l, sh_l, w_perm, layer)
    proj_c = _inproj(xc, norm_w, sc_c, sh_c, w_perm, layer)

    y_a = _natten(proj_l, proj_c, _natten_bias_table(rpb))
    sink128 = jnp.broadcast_to(sink.astype(F32)[:, None], (SW_HEADS, LANES))
    qr, kr = _rope(proj_l, cos128, sin128)
    y_b = _swa(qr, kr, proj_l, proj_c, proj_c, sink128 * LOG2E)
    whi, wlo, bpad = _gla_decay_weights(w_alpha_up, b_alpha)
    s_zero = jnp.zeros((b, 2, 2, 128, 128), F32)
    of_c, s_cf = _gla_pass(proj_c, whi, wlo, bpad, s_zero, rev=False)
    y_cc, s_cb = _gla_pass(proj_c, whi, wlo, bpad, s_zero, rev=True, o_fwd=of_c, gnw=gla_norm_w.astype(F32))
    of_l, _ = _gla_pass(proj_l, whi, wlo, bpad, s_cf, rev=False)
    y_c, _ = _gla_pass(proj_l, whi, wlo, bpad, s_cb, rev=True, o_fwd=of_l, gnw=gla_norm_w.astype(F32))
    y_d = _hyena_long(proj_l, conv_w, conv_b, filt, filt_bias)

    xl_new = _merge([y_a, y_b, y_c, y_d], proj_l, wb, wo, g_l, xl, final_w, final=not with_ctx_out)
    if with_ctx_out:
        ya_c = _ctx_attn(proj_c, sink128, swa=False)
        yb_c = _ctx_attn(proj_c, sink128, swa=True)
        yd_c = _hyena_short(proj_c, conv_w, conv_b, filt, filt_bias)
        xc = _merge([ya_c, yb_c, y_cc, yd_c], proj_c, wb, wo, g_c, xc, final_w, final=False)
    return xc, xl_new


def kernel(x, c, ctx, c_ctx, norm_w, w_mod, b_mod, w_in, rpb, sink, w_alpha_up, b_alpha, gla_norm_w, conv_w, conv_b,
           filt_w1, filt_b1, filt_w2, filt_b2, filt_w3, filt_b3, filt_freq, filt_wout, filt_bias, w_branch, w_out,
           final_norm_w):
    b, length, d = x.shape
    depth = norm_w.shape[0]
    cvec = jnp.zeros((8, d), F32).at[:b].set(c.astype(F32)).at[b].set(c_ctx.astype(F32))
    mod = _modulation(cvec, w_mod.astype(F32), b_mod.astype(F32))
    cos128, sin128 = _rope_tables(length)
    w_perm = _win_prep(w_in.astype(F32))
    xc, xl = ctx, x
    for i in range(depth):
        filt = (filt_w1[i], filt_b1[i], filt_w2[i], filt_b2[i], filt_w3[i], filt_b3[i], filt_freq[i], filt_wout[i])
        xc, xl = _layer(xc, xl, mod[i], cos128, sin128, norm_w[i], w_perm, rpb[i], sink[i], w_alpha_up[i],
                        b_alpha[i], gla_norm_w[i], conv_w[i], conv_b[i], filt, filt_bias[i], w_branch[i], w_out[i],
                        final_norm_w, layer=i, with_ctx_out=(i < depth - 1))
    return xl
```
